```python
import jax, jax.numpy as jnp
from jax import lax
import numpy as np

D_MODEL = 2048
BATCH = 1
SEQ = 8192
DEPTH = 4

N_MIXERS = 2
CONV_WIDTH = 31
FOX_HEAD_DIM = 128
FOX_HEADS = D_MODEL // FOX_HEAD_DIM
ATTN_BLOCK = 128
FFN_DIM = (7 * D_MODEL) // 2
MOE_EXPERTS = 8
MOE_TOP_K = 2
MOE_BLOCK = 128
NORM_EPS = 1e-6

kernel_name = 'hybrid_conv_fox_moe_trunk'


def _rms_norm(x, g):
    xf = x.astype(jnp.float32)
    y = xf * lax.rsqrt(jnp.mean(xf * xf, axis=-1, keepdims=True) + NORM_EPS)
    return (y * g.astype(jnp.float32)).astype(x.dtype)


def _layer_norm(x, g, b):
    xf = x.astype(jnp.float32)
    mu = jnp.mean(xf, axis=-1, keepdims=True)
    xc = xf - mu
    y = xc * lax.rsqrt(jnp.mean(xc * xc, axis=-1, keepdims=True) + NORM_EPS)
    return (y * g.astype(jnp.float32) + b.astype(jnp.float32)).astype(x.dtype)


def _modulate(h, shift, scale):
    return h * (1.0 + scale[:, None, :]) + shift[:, None, :]


def conformer_conv(h, pw1_w, pw1_b, dw_w, dw_b, ln_g, ln_b, pw2_w, pw2_b):
    a = h @ pw1_w + pw1_b
    u = a[..., :D_MODEL] * jax.nn.sigmoid(a[..., D_MODEL:])
    u = lax.conv_general_dilated(
        u, dw_w[:, None, :], window_strides=(1,), padding=[(CONV_WIDTH - 1, 0)],
        dimension_numbers=('NWC', 'WIO', 'NWC'), feature_group_count=D_MODEL) + dw_b
    u = jax.nn.silu(_layer_norm(u, ln_g, ln_b))
    return u @ pw2_w + pw2_b


def _head_rms_norm(t, g):
    tf = t.astype(jnp.float32)
    y = tf * lax.rsqrt(jnp.mean(tf * tf, axis=-1, keepdims=True) + NORM_EPS)
    return (y * g.astype(jnp.float32)).astype(t.dtype)


def forgetting_attention(h, qkv_w, o_w, fgate_w, fgate_b, q_gain, k_gain):
    B, S, _ = h.shape
    qkv = (h @ qkv_w).reshape(B, S, 3, FOX_HEADS, FOX_HEAD_DIM)
    q = _head_rms_norm(qkv[:, :, 0], q_gain).transpose(0, 2, 1, 3)
    k = _head_rms_norm(qkv[:, :, 1], k_gain).transpose(0, 2, 1, 3)
    v = qkv[:, :, 2].transpose(0, 2, 1, 3)
    log_f = jax.nn.log_sigmoid((h @ fgate_w).astype(jnp.float32) + fgate_b.astype(jnp.float32))
    cum = jnp.cumsum(log_f, axis=1).transpose(0, 2, 1)
    nb = S // ATTN_BLOCK
    q_blocks = q.reshape(B, FOX_HEADS, nb, ATTN_BLOCK, FOX_HEAD_DIM).transpose(2, 0, 1, 3, 4)
    cum_q_blocks = cum.reshape(B, FOX_HEADS, nb, ATTN_BLOCK).transpose(2, 0, 1, 3)
    q_pos = jnp.arange(S, dtype=jnp.int32).reshape(nb, ATTN_BLOCK)
    k_pos = jnp.arange(S, dtype=jnp.int32)
    scale = FOX_HEAD_DIM ** -0.5

    def one_block(args):
        qb, cqb, qp = args
        logits = jnp.einsum('bhqd,bhkd->bhqk', qb, k).astype(jnp.float32) * scale
        logits = logits + (cqb[..., :, None] - cum[..., None, :])
        logits = jnp.where(k_pos[None, :] <= qp[:, None], logits, -jnp.inf)
        p = jax.nn.softmax(logits, axis=-1)
        return jnp.einsum('bhqk,bhkd->bhqd', p.astype(v.dtype), v)

    o = lax.map(one_block, (q_blocks, cum_q_blocks, q_pos))
    o = o.transpose(1, 0, 3, 2, 4).reshape(B, S, FOX_HEADS * FOX_HEAD_DIM)
    return o @ o_w


def dense_swiglu(h, gate_w, up_w, down_w):
    return (jax.nn.silu(h @ gate_w) * (h @ up_w)) @ down_w


def moe_swiglu(h, router_w, router_b, gate_w, up_w, down_w):
    B, S, D = h.shape
    T = B * S
    xt = h.reshape(T, D)
    logits = (xt @ router_w).astype(jnp.float32) + router_b.astype(jnp.float32)
    top_logit, top_idx = lax.top_k(logits, MOE_TOP_K)
    top_w = jax.nn.softmax(top_logit, axis=-1)
    n_assign = T * MOE_TOP_K
    flat_e = top_idx.reshape(n_assign).astype(jnp.int32)
    flat_tok = jnp.repeat(jnp.arange(T, dtype=jnp.int32), MOE_TOP_K)
    flat_w = top_w.reshape(n_assign)
    order = jnp.argsort(flat_e)
    e_sorted = flat_e[order]
    tok_sorted = flat_tok[order]
    w_sorted = flat_w[order]
    counts = jnp.bincount(flat_e, length=MOE_EXPERTS).astype(jnp.int32)
    padded = (counts + MOE_BLOCK - 1) // MOE_BLOCK * MOE_BLOCK
    start_raw = jnp.cumsum(counts) - counts
    end_pad = jnp.cumsum(padded)
    start_pad = end_pad - padded
    dest = start_pad[e_sorted] + jnp.arange(n_assign, dtype=jnp.int32) - start_raw[e_sorted]
    n_blocks = -(-n_assign // MOE_BLOCK) + MOE_EXPERTS
    n_slots = n_blocks * MOE_BLOCK
    slot_tok = jnp.zeros((n_slots,), jnp.int32).at[dest].set(tok_sorted)
    slot_w = jnp.zeros((n_slots,), jnp.float32).at[dest].set(w_sorted)
    block_start = jnp.arange(n_blocks, dtype=jnp.int32) * MOE_BLOCK
    block_e = jnp.minimum(jnp.searchsorted(end_pad, block_start, side='right'), MOE_EXPERTS - 1)
    x_blocks = xt[slot_tok].reshape(n_blocks, MOE_BLOCK, D)

    def expert_block(args):
        xb, e = args
        return (jax.nn.silu(xb @ gate_w[e]) * (xb @ up_w[e])) @ down_w[e]

    y_blocks = lax.map(expert_block, (x_blocks, block_e))
    contrib = y_blocks.reshape(n_slots, D) * slot_w[:, None].astype(y_blocks.dtype)
    y = jnp.zeros((T, D), h.dtype).at[slot_tok].add(contrib.astype(h.dtype))
    return y.reshape(B, S, D)


def _w(key, shape, fan_in, gain=1.0):
    return jax.random.normal(key, shape, jnp.float32) * (gain * fan_in ** -0.5)


def _n(key, shape, s):
    return jax.random.normal(key, shape, jnp.float32) * s


def setup_inputs(seed: int = 0) -> dict:
    key = jax.random.key(seed)
    ks = list(jax.random.split(key, 32))
    L = DEPTH
    Le = (DEPTH + 1) // 2
    Lo = DEPTH // 2
    D = D_MODEL
    H = FOX_HEADS
    Dh = FOX_HEAD_DIM
    F = FFN_DIM
    E = MOE_EXPERTS
    return {
        'x': _n(ks[0], (BATCH, SEQ, D), 1.0),
        'c': _n(ks[1], (BATCH, D), 1.0),
        'ada_w': _w(ks[2], (L, D, 6 * D), D, 0.5),
        'ada_b': _n(ks[3], (L, 6 * D), 0.02),
        'mix_norm': 1.0 + _n(ks[4], (L, D), 0.05),
        'ffn_norm': 1.0 + _n(ks[5], (L, D), 0.05),
        'conv_pw1_w': _w(ks[6], (Le, D, 2 * D), D),
        'conv_pw1_b': _n(ks[7], (Le, 2 * D), 0.02),
        'conv_dw_w': _w(ks[8], (Le, CONV_WIDTH, D), CONV_WIDTH),
        'conv_dw_b': _n(ks[9], (Le, D), 0.02),
        'conv_ln_g': 1.0 + _n(ks[10], (Le, D), 0.05),
        'conv_ln_b': _n(ks[11], (Le, D), 0.02),
        'conv_pw2_w': _w(ks[12], (Le, D, D), D),
        'conv_pw2_b': _n(ks[13], (Le, D), 0.02),
        'fox_qkv_w': _w(ks[14], (Lo, D, 3 * H * Dh), D),
        'fox_o_w': _w(ks[15], (Lo, H * Dh, D), H * Dh),
        'fox_fgate_w': _w(ks[16], (Lo, D, H), D, 0.5),
        'fox_fgate_b': jnp.linspace(1.0, 6.0, H, dtype=jnp.float32)[None, :] + _n(ks[17], (Lo, H), 0.1),
        'fox_q_norm': 1.0 + _n(ks[18], (Lo, Dh), 0.05),
        'fox_k_norm': 1.0 + _n(ks[19], (Lo, Dh), 0.05),
        'ffn_gate_w': _w(ks[20], (Le, D, F), D),
        'ffn_up_w': _w(ks[21], (Le, D, F), D),
        'ffn_down_w': _w(ks[22], (Le, F, D), F),
        'moe_router_w': _w(ks[23], (Lo, D, E), D),
        'moe_router_b': _n(ks[24], (Lo, E), 0.01),
        'moe_gate_w': _w(ks[25], (Lo, E, D, F), D),
        'moe_up_w': _w(ks[26], (Lo, E, D, F), D),
        'moe_down_w': _w(ks[27], (Lo, E, F, D), F),
    }


def reference(x, c, ada_w, ada_b, mix_norm, ffn_norm,
              conv_pw1_w, conv_pw1_b, conv_dw_w, conv_dw_b, conv_ln_g, conv_ln_b, conv_pw2_w, conv_pw2_b,
              fox_qkv_w, fox_o_w, fox_fgate_w, fox_fgate_b, fox_q_norm, fox_k_norm,
              ffn_gate_w, ffn_up_w, ffn_down_w,
              moe_router_w, moe_router_b, moe_gate_w, moe_up_w, moe_down_w):
    c_act = jax.nn.silu(c)
    for i in range(DEPTH):
        j = i // N_MIXERS
        mod = c_act @ ada_w[i] + ada_b[i]
        sh1, sc1, g1, sh2, sc2, g2 = jnp.split(mod, 6, axis=-1)
        h = _modulate(_rms_norm(x, mix_norm[i]), sh1, sc1)
        if i % N_MIXERS == 0:
            y = conformer_conv(h, conv_pw1_w[j], conv_pw1_b[j], conv_dw_w[j], conv_dw_b[j],
                               conv_ln_g[j], conv_ln_b[j], conv_pw2_w[j], conv_pw2_b[j])
        else:
            y = forgetting_attention(h, fox_qkv_w[j], fox_o_w[j], fox_fgate_w[j], fox_fgate_b[j],
                                     fox_q_norm[j], fox_k_norm[j])
        x = x + g1[:, None, :] * y
        h = _modulate(_rms_norm(x, ffn_norm[i]), sh2, sc2)
        if i % 2 == 0:
            y = dense_swiglu(h, ffn_gate_w[j], ffn_up_w[j], ffn_down_w[j])
        else:
            y = moe_swiglu(h, moe_router_w[j], moe_router_b[j], moe_gate_w[j], moe_up_w[j], moe_down_w[j])
        x = x + g2[:, None, :] * y
    return x
```

```python
import functools

import jax
import jax.numpy as jnp
from jax import lax
from jax.experimental import pallas as pl
from jax.experimental.pallas import tpu as pltpu

F32 = jnp.float32
BF16 = jnp.bfloat16

NORM_EPS = 1e-6
HEAD_DIM = 128
CONV_WIDTH = 31
CONV_HALO = 32
MOE_TOP_K = 2
LOG2E = 1.4426950408889634
MASK_VALUE = -1e30
LANES = 128
MOE_TM = 256


def _cparams(n_axes, vmem_mb):
    return pltpu.CompilerParams(
        dimension_semantics=("arbitrary",) * n_axes,
        vmem_limit_bytes=vmem_mb << 20)


def _norm_modulate(x, g, sc, sh):
    ms = jnp.mean(x * x, axis=-1, keepdims=True)
    y = x * lax.rsqrt(ms + NORM_EPS) * g
    return y * (1.0 + sc) + sh


def _silu(x):
    return x * jax.nn.sigmoid(x)


def _adaln_kernel(c_ref, w_ref, b_ref, o_ref):
    c = c_ref[...]
    o_ref[0] = jnp.sum(w_ref[0] * _silu(c), axis=0, keepdims=True) + b_ref[0]


def _adaln(c_col, ada_w, ada_b):
    L, D, N = ada_w.shape
    tn = 1024
    return pl.pallas_call(
        _adaln_kernel,
        grid=(L, N // tn),
        in_specs=[pl.BlockSpec((D, 1), lambda l, j: (0, 0)),
                  pl.BlockSpec((1, D, tn), lambda l, j: (l, 0, j)),
                  pl.BlockSpec((1, 1, tn), lambda l, j: (l, 0, j))],
        out_specs=pl.BlockSpec((1, 1, tn), lambda l, j: (l, 0, j)),
        out_shape=jax.ShapeDtypeStruct((L, 1, N), F32),
        compiler_params=_cparams(2, 40),
        name="adaln",
    )(c_col, ada_w, ada_b.reshape(L, 1, N))


def _mod_spec(layer, which, D):
    return pl.BlockSpec((1, 1, D), lambda *_: (layer, 0, which))


def _vec_spec(layer, D):
    return pl.BlockSpec((1, 1, D), lambda *_: (layer, 0, 0))


def _norm_mod_kernel(x_ref, g_ref, sc_ref, sh_ref, o_ref):
    o_ref[...] = _norm_modulate(x_ref[...], g_ref[0], sc_ref[0], sh_ref[0]).astype(o_ref.dtype)


def _norm_mod(x, norm_g, mod, layer, which_shift):
    S, D = x.shape
    tm = 512
    return pl.pallas_call(
        _norm_mod_kernel,
        grid=(S // tm,),
        in_specs=[pl.BlockSpec((tm, D), lambda i: (i, 0)),
                  _vec_spec(layer, D),
                  _mod_spec(layer, which_shift + 1, D),
                  _mod_spec(layer, which_shift, D)],
        out_specs=pl.BlockSpec((tm, D), lambda i: (i, 0)),
        out_shape=jax.ShapeDtypeStruct((S, D), BF16),
        compiler_params=_cparams(1, 32),
        name="norm_mod",
    )(x, norm_g, mod, mod)


def _dot(a, w):
    return jnp.dot(a, w, preferred_element_type=F32)


def _pw1_glu_kernel(a_ref, wv_ref, wg_ref, bv_ref, bg_ref, o_ref):
    a = a_ref[...]
    val = _dot(a, wv_ref[...].astype(BF16)) + bv_ref[0]
    gate = _dot(a, wg_ref[...].astype(BF16)) + bg_ref[0]
    o_ref[...] = val * jax.nn.sigmoid(gate)


def _pw1_glu(h, pw1_w, pw1_b, j):
    S, D = h.shape
    tm, tn = 1024, 512
    nt = D // tn
    return pl.pallas_call(
        _pw1_glu_kernel,
        grid=(S // tm, nt),
        in_specs=[pl.BlockSpec((tm, D), lambda i, n: (i, 0)),
                  pl.BlockSpec((None, D, tn), lambda i, n: (j, 0, n)),
                  pl.BlockSpec((None, D, tn), lambda i, n: (j, 0, n + nt)),
                  pl.BlockSpec((1, 1, tn), lambda i, n: (j, 0, n)),
                  pl.BlockSpec((1, 1, tn), lambda i, n: (j, 0, n + nt))],
        out_specs=pl.BlockSpec((tm, tn), lambda i, n: (i, n)),
        out_shape=jax.ShapeDtypeStruct((S, D), F32),
        compiler_params=_cparams(2, 48),
        name="pw1_glu",
    )(h, pw1_w, pw1_w, pw1_b, pw1_b)


def _gate_up_kernel(a_ref, wg_ref, wu_ref, o_ref):
    a = a_ref[...]
    g = _dot(a, wg_ref[...].astype(BF16))
    u = _dot(a, wu_ref[...].astype(BF16))
    o_ref[...] = (_silu(g) * u).astype(o_ref.dtype)


def _gate_up(h, gate_w, up_w, j):
    S, D = h.shape
    F = gate_w.shape[-1]
    tm, tn = 1024, 512
    return pl.pallas_call(
        _gate_up_kernel,
        grid=(S // tm, F // tn),
        in_specs=[pl.BlockSpec((tm, D), lambda i, n: (i, 0)),
                  pl.BlockSpec((None, D, tn), lambda i, n: (j, 0, n)),
                  pl.BlockSpec((None, D, tn), lambda i, n: (j, 0, n))],
        out_specs=pl.BlockSpec((tm, tn), lambda i, n: (i, n)),
        out_shape=jax.ShapeDtypeStruct((S, F), BF16),
        compiler_params=_cparams(2, 48),
        name="ffn_gate_up",
    )(h, gate_w, up_w)


def _mm_res_kernel(a_ref, w_ref, b_ref, x_ref, g_ref, o_ref):
    y = _dot(a_ref[...], w_ref[...].astype(BF16)) + b_ref[0]
    o_ref[...] = x_ref[...] + g_ref[0] * y


def _mm_res(a, w, bias, x, mod, j, layer, which_gate, tm, tn, name):
    S, K = a.shape
    D = w.shape[-1]
    bj = j if bias.shape[0] > 1 else 0
    return pl.pallas_call(
        _mm_res_kernel,
        grid=(S // tm, D // tn),
        in_specs=[pl.BlockSpec((tm, K), lambda i, n: (i, 0)),
                  pl.BlockSpec((None, K, tn), lambda i, n: (j, 0, n)),
                  pl.BlockSpec((1, 1, tn), lambda i, n: (bj, 0, n)),
                  pl.BlockSpec((tm, tn), lambda i, n: (i, n)),
                  pl.BlockSpec((1, 1, tn), lambda i, n: (layer, 0, which_gate * (D // tn) + n))],
        out_specs=pl.BlockSpec((tm, tn), lambda i, n: (i, n)),
        out_shape=jax.ShapeDtypeStruct((S, D), F32),
        compiler_params=_cparams(2, 56),
        name=name,
    )(a, w, bias, x, mod)


def _conv_ln_kernel(halo_ref, u_ref, w_ref, b_ref, g_ref, beta_ref, o_ref, win_ref, acc_ref, *, tm):
    i = pl.program_id(0)

    @pl.when(i == 0)
    def _():
        win_ref[0:CONV_HALO, :] = jnp.zeros((CONV_HALO, win_ref.shape[1]), F32)

    @pl.when(i > 0)
    def _():
        win_ref[0:CONV_HALO, :] = halo_ref[...]

    win_ref[CONV_HALO:CONV_HALO + tm, :] = u_ref[...]
    off = CONV_HALO - (CONV_WIDTH - 1)
    acc = win_ref[pl.ds(off, tm), :] * w_ref[0, 0:1, :]
    for k in range(1, CONV_WIDTH):
        acc = acc + win_ref[pl.ds(off + k, tm), :] * w_ref[0, k:k + 1, :]
    acc_ref[...] = acc + b_ref[0]
    y = acc_ref[...]
    mu = jnp.mean(y, axis=-1, keepdims=True)
    yc = y - mu
    var = jnp.mean(yc * yc, axis=-1, keepdims=True)
    z = yc * lax.rsqrt(var + NORM_EPS) * g_ref[0] + beta_ref[0]
    o_ref[...] = _silu(z).astype(o_ref.dtype)


def _conv_ln(u, dw_w, dw_b, ln_g, ln_b, j):
    S, D = u.shape
    tm = 128
    hb = tm // CONV_HALO
    return pl.pallas_call(
        functools.partial(_conv_ln_kernel, tm=tm),
        grid=(S // tm,),
        in_specs=[pl.BlockSpec((CONV_HALO, D), lambda i: (jnp.maximum(i * hb - 1, 0), 0)),
                  pl.BlockSpec((tm, D), lambda i: (i, 0)),
                  pl.BlockSpec((1, CONV_WIDTH, D), lambda i: (j, 0, 0)),
                  _vec_spec(j, D), _vec_spec(j, D), _vec_spec(j, D)],
        out_specs=pl.BlockSpec((tm, D), lambda i: (i, 0)),
        out_shape=jax.ShapeDtypeStruct((S, D), BF16),
        scratch_shapes=[pltpu.VMEM((CONV_HALO + tm, D), F32), pltpu.VMEM((tm, D), F32)],
        compiler_params=_cparams(1, 32),
        name="conv_ln",
    )(u, u, dw_w, dw_b, ln_g, ln_b)


def _qkv_kernel(a_ref, w_ref, gain_ref, o_ref, *, n_norm_tiles):
    n = pl.program_id(1)
    acc = _dot(a_ref[...], w_ref[...].astype(BF16))

    @pl.when(n < n_norm_tiles)
    def _():
        gain = gain_ref[...]
        for h in range(acc.shape[1] // HEAD_DIM):
            sl = slice(h * HEAD_DIM, (h + 1) * HEAD_DIM)
            t = acc[:, sl]
            ms = jnp.mean(t * t, axis=-1, keepdims=True)
            o_ref[:, sl] = (t * lax.rsqrt(ms + NORM_EPS) * gain[:, sl]).astype(o_ref.dtype)

    @pl.when(n >= n_norm_tiles)
    def _():
        o_ref[...] = acc.astype(o_ref.dtype)


def _qkv(h, qkv_w, gain, j):
    S, D = h.shape
    N = qkv_w.shape[-1]
    tm, tn = 1024, 512
    return pl.pallas_call(
        functools.partial(_qkv_kernel, n_norm_tiles=2 * D // tn),
        grid=(S // tm, N // tn),
        in_specs=[pl.BlockSpec((tm, D), lambda i, n: (i, 0)),
                  pl.BlockSpec((None, D, tn), lambda i, n: (j, 0, n)),
                  pl.BlockSpec((1, tn), lambda i, n: (0, n))],
        out_specs=pl.BlockSpec((tm, tn), lambda i, n: (i, n)),
        out_shape=jax.ShapeDtypeStruct((S, N), BF16),
        compiler_params=_cparams(2, 48),
        name="fox_qkv",
    )(h, qkv_w, gain)


def _fgate_kernel(h_ref, fwt_ref, fb_ref, tri_ref, o_ref, carry_ref):
    i = pl.program_id(0)

    @pl.when(i == 0)
    def _():
        carry_ref[...] = jnp.zeros(carry_ref.shape, F32)

    z = lax.dot_general(fwt_ref[...].astype(BF16), h_ref[...], (((1,), (1,)), ((), ())),
                        preferred_element_type=F32) + fb_ref[...][:, 0:1]
    lf = jnp.minimum(z, 0.0) - jnp.log1p(jnp.exp(-jnp.abs(z)))
    hi = lf.astype(BF16)
    r1 = lf - hi.astype(F32)
    mid = r1.astype(BF16)
    lo = (r1 - mid.astype(F32)).astype(BF16)
    tri = tri_ref[...]
    cs = _dot(hi, tri) + _dot(mid, tri) + _dot(lo, tri) + carry_ref[...][:, 0:1]
    o_ref[...] = cs
    carry_ref[...] = jnp.broadcast_to(cs[:, cs.shape[1] - 1:], carry_ref.shape)


def _fgate_cum(h, fgate_wt, fgate_b, j):
    S, D = h.shape
    H = fgate_wt.shape[1]
    tm = 512
    tri = jnp.triu(jnp.ones((tm, tm), F32)).astype(BF16)
    fb = jnp.broadcast_to(fgate_b[j][:, None], (H, LANES))
    return pl.pallas_call(
        _fgate_kernel,
        grid=(S // tm,),
        in_specs=[pl.BlockSpec((tm, D), lambda i: (i, 0)),
                  pl.BlockSpec((None, H, D), lambda i: (j, 0, 0)),
                  pl.BlockSpec((H, LANES), lambda i: (0, 0)),
                  pl.BlockSpec((tm, tm), lambda i: (0, 0))],
        out_specs=pl.BlockSpec((H, tm), lambda i: (0, i)),
        out_shape=jax.ShapeDtypeStruct((H, S), F32),
        scratch_shapes=[pltpu.VMEM((H, LANES), F32)],
        compiler_params=_cparams(1, 32),
        name="fox_fgate",
    )(h, fgate_wt, fb, tri)


def _attn_kernel(q_ref, k_ref, v_ref, ck_ref, o_ref, acc_ref, m_ref, vaug_ref, *, tq):
    qi = pl.program_id(1)
    q = q_ref[...]
    acc_ref[...] = jnp.zeros(acc_ref.shape, F32)
    m_ref[...] = jnp.full(m_ref.shape, MASK_VALUE, F32)
    vaug_ref[:, HEAD_DIM:] = jnp.ones((tq, HEAD_DIM), BF16)

    def step(ki, masked):
        ks = pl.multiple_of(ki * tq, tq)
        k = k_ref[pl.ds(ks, tq), :]
        s = lax.dot_general(q, k, (((1,), (1,)), ((), ())), preferred_element_type=F32)
        t = s - ck_ref[:, pl.ds(ks, tq)]
        if masked:
            row = lax.broadcasted_iota(jnp.int32, (tq, tq), 0)
            col = lax.broadcasted_iota(jnp.int32, (tq, tq), 1)
            t = jnp.where(col <= row, t, MASK_VALUE)
        m_prev = m_ref[...]
        m_new = jnp.maximum(m_prev, jnp.max(t, axis=-1, keepdims=True))
        alpha = jnp.exp2(m_prev - m_new)
        p = jnp.exp2(t - m_new[:, 0:1])
        vaug_ref[:, 0:HEAD_DIM] = v_ref[pl.ds(ks, tq), :]
        pv = _dot(p.astype(BF16), vaug_ref[...])
        acc_ref[...] = acc_ref[...] * jnp.concatenate([alpha, alpha], axis=1) + pv
        m_ref[...] = m_new

    def body(ki, carry):
        step(ki, False)
        return carry

    lax.fori_loop(0, qi, body, 0)
    step(qi, True)
    acc = acc_ref[...]
    o_ref[...] = (acc[:, 0:HEAD_DIM] / acc[:, HEAD_DIM:]).astype(o_ref.dtype)


def _attention(qkv, ck2):
    S = qkv.shape[0]
    H = ck2.shape[0]
    tq = 512
    return pl.pallas_call(
        functools.partial(_attn_kernel, tq=tq),
        grid=(H, S // tq),
        in_specs=[pl.BlockSpec((tq, HEAD_DIM), lambda h, i: (i, h)),
                  pl.BlockSpec((S, HEAD_DIM), lambda h, i: (0, H + h)),
                  pl.BlockSpec((S, HEAD_DIM), lambda h, i: (0, 2 * H + h)),
                  pl.BlockSpec((None, 1, S), lambda h, i: (h, 0, 0))],
        out_specs=pl.BlockSpec((tq, HEAD_DIM), lambda h, i: (i, h)),
        out_shape=jax.ShapeDtypeStruct((S, H * HEAD_DIM), BF16),
        scratch_shapes=[pltpu.VMEM((tq, 2 * HEAD_DIM), F32),
                        pltpu.VMEM((tq, HEAD_DIM), F32),
                        pltpu.VMEM((tq, 2 * HEAD_DIM), BF16)],
        compiler_params=_cparams(2, 32),
        name="fox_attn",
    )(qkv, qkv, qkv, ck2)


def _router_kernel(x_ref, g_ref, sc_ref, sh_ref, rw_ref, rb_ref, o_ref):
    h = _norm_modulate(x_ref[...], g_ref[0], sc_ref[0], sh_ref[0])
    o_ref[...] = jnp.dot(h, rw_ref[...], preferred_element_type=F32,
                         precision=lax.Precision.HIGHEST) + rb_ref[...]


def _router(x, norm_g, mod, layer, rw_pad, rb_pad):
    S, D = x.shape
    tm = 512
    return pl.pallas_call(
        _router_kernel,
        grid=(S // tm,),
        in_specs=[pl.BlockSpec((tm, D), lambda i: (i, 0)),
                  _vec_spec(layer, D), _mod_spec(layer, 4, D), _mod_spec(layer, 3, D),
                  pl.BlockSpec((D, LANES), lambda i: (0, 0)),
                  pl.BlockSpec((1, LANES), lambda i: (0, 0))],
        out_specs=pl.BlockSpec((tm, LANES), lambda i: (i, 0)),
        out_shape=jax.ShapeDtypeStruct((S, LANES), F32),
        compiler_params=_cparams(1, 32),
        name="moe_router",
    )(x, norm_g, mod, mod, rw_pad, rb_pad)


def _row_copy(src_hbm, dst, src_row, dst_row, sem):
    return pltpu.make_async_copy(src_hbm.at[pl.ds(src_row, 1), :], dst.at[pl.ds(dst_row, 1), :], sem)


def _gather_norm_kernel(tok_ref, x_hbm, g_ref, sc_ref, sh_ref, o_ref, buf, sem, *, tm):
    base = pl.program_id(0) * tm

    def issue(r, c):
        _row_copy(x_hbm, buf, tok_ref[base + r], r, sem).start()
        return c

    def wait(r, c):
        _row_copy(x_hbm, buf, 0, r, sem).wait()
        return c

    lax.fori_loop(0, tm, issue, 0)
    lax.fori_loop(0, tm, wait, 0)
    o_ref[...] = _norm_modulate(buf[...], g_ref[0], sc_ref[0], sh_ref[0]).astype(o_ref.dtype)


def _gather_norm(tok_sorted, x, norm_g, mod, layer):
    S, D = x.shape
    n_rows = tok_sorted.shape[0]
    tm = MOE_TM
    grid_spec = pltpu.PrefetchScalarGridSpec(
        num_scalar_prefetch=1,
        grid=(n_rows // tm,),
        in_specs=[pl.BlockSpec(memory_space=pl.ANY),
                  pl.BlockSpec((1, 1, D), lambda i, t: (layer, 0, 0)),
                  pl.BlockSpec((1, 1, D), lambda i, t: (layer, 0, 4)),
                  pl.BlockSpec((1, 1, D), lambda i, t: (layer, 0, 3))],
        out_specs=pl.BlockSpec((tm, D), lambda i, t: (i, 0)),
        scratch_shapes=[pltpu.VMEM((tm, D), F32), pltpu.SemaphoreType.DMA(())])
    return pl.pallas_call(
        functools.partial(_gather_norm_kernel, tm=tm),
        grid_spec=grid_spec,
        out_shape=jax.ShapeDtypeStruct((n_rows, D), BF16),
        compiler_params=_cparams(1, 32),
        name="moe_gather",
    )(tok_sorted, x, norm_g, mod, mod)


def _store_rows(o_ref, val, tile, lo, hi, first):
    tm = val.shape[0]

    @pl.when(first == 1)
    def _():
        o_ref[...] = val

    @pl.when(first == 0)
    def _():
        rows = tile * tm + lax.broadcasted_iota(jnp.int32, (tm, 1), 0)
        keep = (rows >= lo) & (rows < hi)
        o_ref[...] = jnp.where(keep, val, o_ref[...])


def _moe_gu_kernel(tile_ref, exp_ref, first_ref, valid_ref, lo_ref, hi_ref,
                   a_ref, wg_ref, wu_ref, o_ref, wgb, wub):
    i = pl.program_id(1)
    e = exp_ref[i]
    changed = (i == 0) | (e != exp_ref[jnp.maximum(i - 1, 0)])

    @pl.when(changed)
    def _():
        wgb[...] = wg_ref[...].astype(BF16)
        wub[...] = wu_ref[...].astype(BF16)

    @pl.when(valid_ref[i] == 1)
    def _():
        a = a_ref[...]
        hid = (_silu(_dot(a, wgb[...])) * _dot(a, wub[...])).astype(o_ref.dtype)
        _store_rows(o_ref, hid, tile_ref[i], lo_ref[i], hi_ref[i], first_ref[i])


def _moe_gate_up(items, xs, gate_w, up_w, j):
    n_rows, D = xs.shape
    F = gate_w.shape[-1]
    tm, tn = MOE_TM, 512
    n_items = items[0].shape[0]
    w_spec = pl.BlockSpec((None, None, D, tn), lambda n, i, tile, exp, *_: (j, exp[i], 0, n))
    grid_spec = pltpu.PrefetchScalarGridSpec(
        num_scalar_prefetch=6,
        grid=(F // tn, n_items),
        in_specs=[pl.BlockSpec((tm, D), lambda n, i, tile, *_: (tile[i], 0)), w_spec, w_spec],
        out_specs=pl.BlockSpec((tm, tn), lambda n, i, tile, *_: (tile[i], n)),
        scratch_shapes=[pltpu.VMEM((D, tn), BF16), pltpu.VMEM((D, tn), BF16)])
    return pl.pallas_call(
        _moe_gu_kernel,
        grid_spec=grid_spec,
        out_shape=jax.ShapeDtypeStruct((n_rows, F), BF16),
        compiler_params=_cparams(2, 48),
        name="moe_gate_up",
    )(*items, xs, gate_w, up_w)


def _moe_down_kernel(tile_ref, exp_ref, first_ref, valid_ref, lo_ref, hi_ref,
                     a_ref, w_ref, o_ref, wb):
    i = pl.program_id(1)
    e = exp_ref[i]
    changed = (i == 0) | (e != exp_ref[jnp.maximum(i - 1, 0)])

    @pl.when(changed)
    def _():
        wb[...] = w_ref[...].astype(BF16)

    @pl.when(valid_ref[i] == 1)
    def _():
        y = _dot(a_ref[...], wb[...])
        _store_rows(o_ref, y, tile_ref[i], lo_ref[i], hi_ref[i], first_ref[i])


def _moe_down(items, hid, down_w, j):
    n_rows, F = hid.shape
    D = down_w.shape[-1]
    tm, tn = MOE_TM, 512
    n_items = items[0].shape[0]
    grid_spec = pltpu.PrefetchScalarGridSpec(
        num_scalar_prefetch=6,
        grid=(D // tn, n_items),
        in_specs=[pl.BlockSpec((tm, F), lambda n, i, tile, *_: (tile[i], 0)),
                  pl.BlockSpec((None, None, F, tn), lambda n, i, tile, exp, *_: (j, exp[i], 0, n))],
        out_specs=pl.BlockSpec((tm, tn), lambda n, i, tile, *_: (tile[i], n)),
        scratch_shapes=[pltpu.VMEM((F, tn), BF16)])
    return pl.pallas_call(
        _moe_down_kernel,
        grid_spec=grid_spec,
        out_shape=jax.ShapeDtypeStruct((n_rows, D), F32),
        compiler_params=_cparams(2, 56),
        name="moe_down",
    )(*items, hid, down_w)


def _combine_kernel(pos_ref, y_hbm, x_ref, g_ref, w_ref, o_ref, buf, sem, *, tm):
    base = pl.program_id(0) * tm

    def issue(r, c):
        for kk in range(MOE_TOP_K):
            _row_copy(y_hbm, buf.at[kk], pos_ref[(base + r) * MOE_TOP_K + kk], r, sem).start()
        return c

    def wait(r, c):
        for kk in range(MOE_TOP_K):
            _row_copy(y_hbm, buf.at[kk], 0, r, sem).wait()
        return c

    lax.fori_loop(0, tm, issue, 0)
    lax.fori_loop(0, tm, wait, 0)
    w = w_ref[...]
    y = buf[0] * w[:, 0:1] + buf[1] * w[:, 1:2]
    o_ref[...] = x_ref[...] + g_ref[0] * y


def _combine(pos, y_sorted, x, mod, layer, top_w):
    S, D = x.shape
    tm = 256
    grid_spec = pltpu.PrefetchScalarGridSpec(
        num_scalar_prefetch=1,
        grid=(S // tm,),
        in_specs=[pl.BlockSpec(memory_space=pl.ANY),
                  pl.BlockSpec((tm, D), lambda i, p: (i, 0)),
                  pl.BlockSpec((1, 1, D), lambda i, p: (layer, 0, 5)),
                  pl.BlockSpec((tm, MOE_TOP_K), lambda i, p: (i, 0))],
        out_specs=pl.BlockSpec((tm, D), lambda i, p: (i, 0)),
        scratch_shapes=[pltpu.VMEM((MOE_TOP_K, tm, D), F32), pltpu.SemaphoreType.DMA(())])
    return pl.pallas_call(
        functools.partial(_combine_kernel, tm=tm),
        grid_spec=grid_spec,
        out_shape=jax.ShapeDtypeStruct((S, D), F32),
        compiler_params=_cparams(1, 32),
        name="moe_combine",
    )(pos, y_sorted, x, mod, top_w)


def _route(logits, n_experts, tm):
    S = logits.shape[0]
    top_logit, top_idx = lax.top_k(logits, MOE_TOP_K)
    top_w = jax.nn.softmax(top_logit, axis=-1)
    n_assign = S * MOE_TOP_K
    flat_e = top_idx.reshape(n_assign).astype(jnp.int32)
    onehot = (flat_e[:, None] == jnp.arange(n_experts, dtype=jnp.int32)[None, :]).astype(jnp.int32)
    csum = jnp.cumsum(onehot, axis=0)
    rank = jnp.sum((csum - 1) * onehot, axis=1)
    counts = csum[-1]
    ends = jnp.cumsum(counts)
    starts = ends - counts
    pos = (starts[flat_e] + rank).astype(jnp.int32)
    flat_tok = jnp.arange(n_assign, dtype=jnp.int32) // MOE_TOP_K
    tok_sorted = jnp.zeros((n_assign,), jnp.int32).at[pos].set(flat_tok)

    n_tiles = n_assign // tm
    n_items = n_tiles + n_experts - 1
    first_tile = starts // tm
    n_e = jnp.where(counts > 0, (ends + tm - 1) // tm - first_tile, 0)
    item_end = jnp.cumsum(n_e)
    item_start = item_end - n_e
    total = item_end[-1]
    idx = jnp.arange(n_items, dtype=jnp.int32)
    valid = idx < total
    idx_c = jnp.minimum(idx, total - 1)
    item_e = jnp.minimum(jnp.searchsorted(item_end, idx_c, side='right'), n_experts - 1).astype(jnp.int32)
    item_tile = (first_tile[item_e] + idx_c - item_start[item_e]).astype(jnp.int32)
    prev_tile = jnp.concatenate([jnp.full((1,), -1, jnp.int32), item_tile[:-1]])
    first = (valid & (item_tile != prev_tile)).astype(jnp.int32)
    items = (item_tile, item_e, first, valid.astype(jnp.int32),
             starts[item_e].astype(jnp.int32), ends[item_e].astype(jnp.int32))
    return tok_sorted, pos, top_w, items


def kernel(x, c, ada_w, ada_b, mix_norm, ffn_norm, conv_pw1_w, conv_pw1_b, conv_dw_w, conv_dw_b, conv_ln_g, conv_ln_b, conv_pw2_w, conv_pw2_b, fox_qkv_w, fox_o_w, fox_fgate_w, fox_fgate_b, fox_q_norm, fox_k_norm, ffn_gate_w, ffn_up_w, ffn_down_w, moe_router_w, moe_router_b, moe_gate_w, moe_up_w, moe_down_w):
    B, S, D = x.shape
    assert B == 1, "kernels are written for a single sequence"
    L = ada_w.shape[0]
    H = fox_fgate_w.shape[-1]
    E = moe_router_w.shape[-1]

    def as_rows(v):
        return v.reshape(v.shape[0], 1, v.shape[1])

    mod = _adaln(c.reshape(D, 1), ada_w, ada_b)
    mix_g, ffn_g = as_rows(mix_norm), as_rows(ffn_norm)
    zero_bias = jnp.zeros((1, 1, D), F32)
    xs = x.reshape(S, D)

    for i in range(L):
        j = i // 2
        h = _norm_mod(xs, mix_g, mod, i, 0)
        if i % 2 == 0:
            u = _pw1_glu(h, conv_pw1_w, as_rows(conv_pw1_b), j)
            v = _conv_ln(u, conv_dw_w, as_rows(conv_dw_b), as_rows(conv_ln_g), as_rows(conv_ln_b), j)
            xs = _mm_res(v, conv_pw2_w, as_rows(conv_pw2_b), xs, mod, j, i, 2, 1024, 512, "conv_pw2")
        else:
            q_gain = fox_q_norm[j] * (HEAD_DIM ** -0.5 * LOG2E)
            gain = jnp.concatenate([jnp.tile(q_gain, H), jnp.tile(fox_k_norm[j], H),
                                    jnp.ones((D,), F32)])[None, :]
            qkv = _qkv(h, fox_qkv_w, gain, j)
            cum = _fgate_cum(h, jnp.swapaxes(fox_fgate_w, 1, 2), fox_fgate_b, j)
            o = _attention(qkv, (cum * LOG2E).reshape(H, 1, S))
            xs = _mm_res(o, fox_o_w, zero_bias, xs, mod, j, i, 2, 1024, 512, "fox_o")
        if i % 2 == 0:
            h = _norm_mod(xs, ffn_g, mod, i, 3)
            hid = _gate_up(h, ffn_gate_w, ffn_up_w, j)
            xs = _mm_res(hid, ffn_down_w, zero_bias, xs, mod, j, i, 5, 1024, 256, "ffn_down")
        else:
            rw_pad = jnp.zeros((D, LANES), F32).at[:, :E].set(moe_router_w[j])
            rb_pad = jnp.zeros((1, LANES), F32).at[0, :E].set(moe_router_b[j])
            logits = _router(xs, ffn_g, mod, i, rw_pad, rb_pad)[:, :E]
            tok_sorted, pos, top_w, items = _route(logits, E, MOE_TM)
            xg = _gather_norm(tok_sorted, xs, ffn_g, mod, i)
            hid = _moe_gate_up(items, xg, moe_gate_w, moe_up_w, j)
            y_sorted = _moe_down(items, hid, moe_down_w, j)
            xs = _combine(pos, y_sorted, xs, mod, i, top_w)
    return xs.reshape(B, S, D)
```

```python
import functools

import jax
import jax.numpy as jnp
from jax import lax
from jax.experimental import pallas as pl
from jax.experimental.pallas import tpu as pltpu

F32 = jnp.float32
BF16 = jnp.bfloat16

NORM_EPS = 1e-6
HEAD_DIM = 128
CONV_WIDTH = 31
CONV_HALO = 32
MOE_TOP_K = 2
LOG2E = 1.4426950408889634
MASK_VALUE = -1e30
ATTN_FAST_MAX_BOUND = 40.0
LANES = 128
SUBLANES = 8
MOE_SUB = 256
MOE_GU_TM = 1024
MOE_DOWN_TM = 256


def _cparams(n_axes, vmem_mb):
    return pltpu.CompilerParams(
        dimension_semantics=("arbitrary",) * n_axes,
        vmem_limit_bytes=vmem_mb << 20)


def _norm_modulate(x, g, sc, sh):
    ms = jnp.mean(x * x, axis=-1, keepdims=True)
    y = x * lax.rsqrt(ms + NORM_EPS) * g
    return y * (1.0 + sc) + sh


def _silu(x):
    return x * jax.nn.sigmoid(x)


def _adaln_kernel(c_ref, w_ref, b_ref, o_ref):
    c = c_ref[...]
    o_ref[0] = jnp.sum(w_ref[0] * _silu(c), axis=0, keepdims=True) + b_ref[0]


def _adaln(c_col, ada_w, ada_b):
    L, D, N = ada_w.shape
    tn = 1024
    return pl.pallas_call(
        _adaln_kernel,
        grid=(L, N // tn),
        in_specs=[pl.BlockSpec((D, 1), lambda l, j: (0, 0)),
                  pl.BlockSpec((1, D, tn), lambda l, j: (l, 0, j)),
                  pl.BlockSpec((1, 1, tn), lambda l, j: (l, 0, j))],
        out_specs=pl.BlockSpec((1, 1, tn), lambda l, j: (l, 0, j)),
        out_shape=jax.ShapeDtypeStruct((L, 1, N), F32),
        compiler_params=_cparams(2, 40),
        name="adaln",
    )(c_col, ada_w, ada_b.reshape(L, 1, N))


def _mod_spec(layer, which, D):
    return pl.BlockSpec((1, 1, D), lambda *_: (layer, 0, which))


def _vec_spec(layer, D):
    return pl.BlockSpec((1, 1, D), lambda *_: (layer, 0, 0))


def _norm_mod_kernel(x_ref, g_ref, sc_ref, sh_ref, o_ref):
    o_ref[...] = _norm_modulate(x_ref[...], g_ref[0], sc_ref[0], sh_ref[0]).astype(o_ref.dtype)


def _norm_mod(x, norm_g, mod, layer, which_shift):
    S, D = x.shape
    tm = 512
    return pl.pallas_call(
        _norm_mod_kernel,
        grid=(S // tm,),
        in_specs=[pl.BlockSpec((tm, D), lambda i: (i, 0)),
                  _vec_spec(layer, D),
                  _mod_spec(layer, which_shift + 1, D),
                  _mod_spec(layer, which_shift, D)],
        out_specs=pl.BlockSpec((tm, D), lambda i: (i, 0)),
        out_shape=jax.ShapeDtypeStruct((S, D), BF16),
        compiler_params=_cparams(1, 32),
        name="norm_mod",
    )(x, norm_g, mod, mod)


def _dot(a, w):
    return jnp.dot(a, w, preferred_element_type=F32)


def _pw1_glu_kernel(a_ref, wv_ref, wg_ref, bv_ref, bg_ref, o_ref):
    a = a_ref[...]
    val = _dot(a, wv_ref[...].astype(BF16)) + bv_ref[0]
    gate = _dot(a, wg_ref[...].astype(BF16)) + bg_ref[0]
    o_ref[...] = val * jax.nn.sigmoid(gate)


def _pw1_glu(h, pw1_w, pw1_b, j):
    S, D = h.shape
    tm, tn = 1024, 512
    nt = D // tn
    return pl.pallas_call(
        _pw1_glu_kernel,
        grid=(S // tm, nt),
        in_specs=[pl.BlockSpec((tm, D), lambda i, n: (i, 0)),
                  pl.BlockSpec((None, D, tn), lambda i, n: (j, 0, n)),
                  pl.BlockSpec((None, D, tn), lambda i, n: (j, 0, n + nt)),
                  pl.BlockSpec((1, 1, tn), lambda i, n: (j, 0, n)),
                  pl.BlockSpec((1, 1, tn), lambda i, n: (j, 0, n + nt))],
        out_specs=pl.BlockSpec((tm, tn), lambda i, n: (i, n)),
        out_shape=jax.ShapeDtypeStruct((S, D), F32),
        compiler_params=_cparams(2, 48),
        name="pw1_glu",
    )(h, pw1_w, pw1_w, pw1_b, pw1_b)


def _gate_up_kernel(a_ref, wg_ref, wu_ref, o_ref):
    a = a_ref[...]
    g = _dot(a, wg_ref[...].astype(BF16))
    u = _dot(a, wu_ref[...].astype(BF16))
    o_ref[...] = (_silu(g) * u).astype(o_ref.dtype)


def _gate_up(h, gate_w, up_w, j):
    S, D = h.shape
    F = gate_w.shape[-1]
    tm, tn = 1024, 512
    return pl.pallas_call(
        _gate_up_kernel,
        grid=(S // tm, F // tn),
        in_specs=[pl.BlockSpec((tm, D), lambda i, n: (i, 0)),
                  pl.BlockSpec((None, D, tn), lambda i, n: (j, 0, n)),
                  pl.BlockSpec((None, D, tn), lambda i, n: (j, 0, n))],
        out_specs=pl.BlockSpec((tm, tn), lambda i, n: (i, n)),
        out_shape=jax.ShapeDtypeStruct((S, F), BF16),
        compiler_params=_cparams(2, 48),
        name="ffn_gate_up",
    )(h, gate_w, up_w)


def _mm_res_kernel(a_ref, w_ref, b_ref, x_ref, g_ref, o_ref):
    y = _dot(a_ref[...], w_ref[...].astype(BF16)) + b_ref[0]
    o_ref[...] = x_ref[...] + g_ref[0] * y


def _mm_res(a, w, bias, x, mod, j, layer, which_gate, tm, tn, name):
    S, K = a.shape
    D = w.shape[-1]
    bj = j if bias.shape[0] > 1 else 0
    return pl.pallas_call(
        _mm_res_kernel,
        grid=(S // tm, D // tn),
        in_specs=[pl.BlockSpec((tm, K), lambda i, n: (i, 0)),
                  pl.BlockSpec((None, K, tn), lambda i, n: (j, 0, n)),
                  pl.BlockSpec((1, 1, tn), lambda i, n: (bj, 0, n)),
                  pl.BlockSpec((tm, tn), lambda i, n: (i, n)),
                  pl.BlockSpec((1, 1, tn), lambda i, n: (layer, 0, which_gate * (D // tn) + n))],
        out_specs=pl.BlockSpec((tm, tn), lambda i, n: (i, n)),
        out_shape=jax.ShapeDtypeStruct((S, D), F32),
        compiler_params=_cparams(2, 56),
        name=name,
    )(a, w, bias, x, mod)


def _conv_ln_kernel(halo_ref, u_ref, w_ref, b_ref, g_ref, beta_ref, o_ref, win_ref, acc_ref, *, tm):
    i = pl.program_id(0)

    @pl.when(i == 0)
    def _():
        win_ref[0:CONV_HALO, :] = jnp.zeros((CONV_HALO, win_ref.shape[1]), F32)

    @pl.when(i > 0)
    def _():
        win_ref[0:CONV_HALO, :] = halo_ref[...]

    win_ref[CONV_HALO:CONV_HALO + tm, :] = u_ref[...]
    first = CONV_HALO - (CONV_WIDTH - 1)
    for c in range(u_ref.shape[1] // LANES):
        cs = slice(c * LANES, (c + 1) * LANES)
        out = None
        for b in range(SUBLANES):
            rows = tm if b == 0 else tm + SUBLANES
            y = None
            for j in range(first, first + CONV_WIDTH):
                if j % SUBLANES != b:
                    continue
                term = win_ref[j - b:j - b + rows, cs] * w_ref[0, j - first:j - first + 1, cs]
                y = term if y is None else y + term
            yb = y if b == 0 else y[b:b + tm]
            out = yb if out is None else out + yb
        acc_ref[:, cs] = out + b_ref[0][:, cs]
    y = acc_ref[...]
    mu = jnp.mean(y, axis=-1, keepdims=True)
    yc = y - mu
    var = jnp.mean(yc * yc, axis=-1, keepdims=True)
    z = yc * lax.rsqrt(var + NORM_EPS) * g_ref[0] + beta_ref[0]
    o_ref[...] = _silu(z).astype(o_ref.dtype)


def _conv_ln(u, dw_w, dw_b, ln_g, ln_b, j):
    S, D = u.shape
    tm = 128
    hb = tm // CONV_HALO
    return pl.pallas_call(
        functools.partial(_conv_ln_kernel, tm=tm),
        grid=(S // tm,),
        in_specs=[pl.BlockSpec((CONV_HALO, D), lambda i: (jnp.maximum(i * hb - 1, 0), 0)),
                  pl.BlockSpec((tm, D), lambda i: (i, 0)),
                  pl.BlockSpec((1, CONV_WIDTH, D), lambda i: (j, 0, 0)),
                  _vec_spec(j, D), _vec_spec(j, D), _vec_spec(j, D)],
        out_specs=pl.BlockSpec((tm, D), lambda i: (i, 0)),
        out_shape=jax.ShapeDtypeStruct((S, D), BF16),
        scratch_shapes=[pltpu.VMEM((CONV_HALO + tm, D), F32), pltpu.VMEM((tm, D), F32)],
        compiler_params=_cparams(1, 32),
        name="conv_ln",
    )(u, u, dw_w, dw_b, ln_g, ln_b)


def _qkv_kernel(a_ref, w_ref, gain_ref, o_ref, *, n_norm_tiles):
    n = pl.program_id(1)
    acc = _dot(a_ref[...], w_ref[...].astype(BF16))

    @pl.when(n < n_norm_tiles)
    def _():
        gain = gain_ref[...]
        for h in range(acc.shape[1] // HEAD_DIM):
            sl = slice(h * HEAD_DIM, (h + 1) * HEAD_DIM)
            t = acc[:, sl]
            ms = jnp.mean(t * t, axis=-1, keepdims=True)
            o_ref[:, sl] = (t * lax.rsqrt(ms + NORM_EPS) * gain[:, sl]).astype(o_ref.dtype)

    @pl.when(n >= n_norm_tiles)
    def _():
        o_ref[...] = acc.astype(o_ref.dtype)


def _qkv(h, qkv_w, gain, j):
    S, D = h.shape
    N = qkv_w.shape[-1]
    tm, tn = 1024, 512
    return pl.pallas_call(
        functools.partial(_qkv_kernel, n_norm_tiles=2 * D // tn),
        grid=(S // tm, N // tn),
        in_specs=[pl.BlockSpec((tm, D), lambda i, n: (i, 0)),
                  pl.BlockSpec((None, D, tn), lambda i, n: (j, 0, n)),
                  pl.BlockSpec((1, tn), lambda i, n: (0, n))],
        out_specs=pl.BlockSpec((tm, tn), lambda i, n: (i, n)),
        out_shape=jax.ShapeDtypeStruct((S, N), BF16),
        compiler_params=_cparams(2, 48),
        name="fox_qkv",
    )(h, qkv_w, gain)


def _fgate_kernel(h_ref, fwt_ref, fb_ref, tri_ref, o_ref, carry_ref):
    i = pl.program_id(0)

    @pl.when(i == 0)
    def _():
        carry_ref[...] = jnp.zeros(carry_ref.shape, F32)

    z = lax.dot_general(fwt_ref[...].astype(BF16), h_ref[...], (((1,), (1,)), ((), ())),
                        preferred_element_type=F32) + fb_ref[...][:, 0:1]
    lf = jnp.minimum(z, 0.0) - jnp.log1p(jnp.exp(-jnp.abs(z)))
    hi = lf.astype(BF16)
    r1 = lf - hi.astype(F32)
    mid = r1.astype(BF16)
    lo = (r1 - mid.astype(F32)).astype(BF16)
    tri = tri_ref[...]
    cs = _dot(hi, tri) + _dot(mid, tri) + _dot(lo, tri) + carry_ref[...][:, 0:1]
    o_ref[...] = cs
    carry_ref[...] = jnp.broadcast_to(cs[:, cs.shape[1] - 1:], carry_ref.shape)


def _fgate_cum(h, fgate_wt, fgate_b, j):
    S, D = h.shape
    H = fgate_wt.shape[1]
    tm = 512
    tri = jnp.triu(jnp.ones((tm, tm), F32)).astype(BF16)
    fb = jnp.broadcast_to(fgate_b[j][:, None], (H, LANES))
    return pl.pallas_call(
        _fgate_kernel,
        grid=(S // tm,),
        in_specs=[pl.BlockSpec((tm, D), lambda i: (i, 0)),
                  pl.BlockSpec((None, H, D), lambda i: (j, 0, 0)),
                  pl.BlockSpec((H, LANES), lambda i: (0, 0)),
                  pl.BlockSpec((tm, tm), lambda i: (0, 0))],
        out_specs=pl.BlockSpec((H, tm), lambda i: (0, i)),
        out_shape=jax.ShapeDtypeStruct((H, S), F32),
        scratch_shapes=[pltpu.VMEM((H, LANES), F32)],
        compiler_params=_cparams(1, 32),
        name="fox_fgate",
    )(h, fgate_wt, fb, tri)


def _attn_kernel(flag_ref, q_ref, k_ref, v_ref, ck_ref, b_ref, o_ref,
                 acc_ref, m_ref, vaug_ref, rt_ref, *, tq, heads):
    qi = pl.program_id(1)
    S = k_ref.shape[0]
    n_chunks = tq // LANES
    hs = [slice(g * HEAD_DIM, (g + 1) * HEAD_DIM) for g in range(heads)]
    acc_ref[...] = jnp.zeros(acc_ref.shape, F32)

    @pl.when(qi == 0)
    def _():
        for g in range(heads):
            vaug_ref[g, :, 0:HEAD_DIM] = v_ref[:, hs[g]]
            vaug_ref[g, :, HEAD_DIM:] = jnp.ones((S, HEAD_DIM), BF16)

    def scores(g, ks):
        return lax.dot_general(q_ref[:, hs[g]], k_ref[pl.ds(ks, tq), hs[g]],
                               (((1,), (1,)), ((), ())), preferred_element_type=F32)

    def visible(c):
        row = lax.broadcasted_iota(jnp.int32, (tq, LANES), 0)
        col = lax.broadcasted_iota(jnp.int32, (tq, LANES), 1) + c * LANES
        return col <= row

    def fast_step(ki, masked):
        ks = pl.multiple_of(ki * tq, tq)
        for g in range(heads):
            s = scores(g, ks)
            ckr = ck_ref[g, :, pl.ds(ks, tq)]
            rt = rt_ref[g]
            ps = []
            for c in range(n_chunks):
                sl = slice(c * LANES, (c + 1) * LANES)
                t = s[:, sl] + (rt - ckr[:, sl])
                if masked:
                    t = jnp.where(visible(c), t, MASK_VALUE)
                ps.append(jnp.exp2(t).astype(BF16))
            p = jnp.concatenate(ps, axis=1)
            acc_ref[g] += _dot(p, vaug_ref[g, pl.ds(ks, tq), :])

    def slow_step(ki, masked):
        ks = pl.multiple_of(ki * tq, tq)
        for g in range(heads):
            t = scores(g, ks) - ck_ref[g, :, pl.ds(ks, tq)]
            if masked:
                t = jnp.concatenate(
                    [jnp.where(visible(c), t[:, c * LANES:(c + 1) * LANES], MASK_VALUE) for c in range(n_chunks)],
                    axis=1)
            m_prev = m_ref[g]
            m_new = jnp.maximum(m_prev, jnp.max(t, axis=-1, keepdims=True))
            alpha = jnp.exp2(m_prev - m_new)
            p = jnp.exp2(t - m_new[:, 0:1])
            pv = _dot(p.astype(BF16), vaug_ref[g, pl.ds(ks, tq), :])
            acc_ref[g] = acc_ref[g] * jnp.concatenate([alpha, alpha], axis=1) + pv
            m_ref[g] = m_new

    def sweep(step):
        def body(ki, carry):
            step(ki, False)
            return carry
        lax.fori_loop(0, qi, body, 0)
        step(qi, True)

    @pl.when(flag_ref[0] == 1)
    def _():
        qs = pl.multiple_of(qi * tq, tq)
        for g in range(heads):
            ckq = jnp.broadcast_to(ck_ref[g, :, pl.ds(qs, tq)], (LANES, tq))
            rt_ref[g] = ckq.T - b_ref[...]
        sweep(fast_step)

    @pl.when(flag_ref[0] != 1)
    def _():
        m_ref[...] = jnp.full(m_ref.shape, MASK_VALUE, F32)
        sweep(slow_step)

    for g in range(heads):
        acc = acc_ref[g]
        o_ref[:, hs[g]] = (acc[:, 0:HEAD_DIM] / acc[:, HEAD_DIM:]).astype(o_ref.dtype)


def _attention(qkv, ck2, bound):
    S = qkv.shape[0]
    H = ck2.shape[0]
    tq, heads = 512, 2
    hw = heads * HEAD_DIM
    flag = (bound <= ATTN_FAST_MAX_BOUND).astype(jnp.int32).reshape(1)
    bvec = jnp.broadcast_to(bound.astype(F32), (1, LANES))
    grid_spec = pltpu.PrefetchScalarGridSpec(
        num_scalar_prefetch=1,
        grid=(H // heads, S // tq),
        in_specs=[pl.BlockSpec((tq, hw), lambda h, i, f: (i, h)),
                  pl.BlockSpec((S, hw), lambda h, i, f: (0, H // heads + h)),
                  pl.BlockSpec((S, hw), lambda h, i, f: (0, 2 * (H // heads) + h)),
                  pl.BlockSpec((heads, 1, S), lambda h, i, f: (h, 0, 0)),
                  pl.BlockSpec((1, LANES), lambda h, i, f: (0, 0))],
        out_specs=pl.BlockSpec((tq, hw), lambda h, i, f: (i, h)),
        scratch_shapes=[pltpu.VMEM((heads, tq, 2 * HEAD_DIM), F32),
                        pltpu.VMEM((heads, tq, HEAD_DIM), F32),
                        pltpu.VMEM((heads, S, 2 * HEAD_DIM), BF16),
                        pltpu.VMEM((heads, tq, LANES), F32)])
    return pl.pallas_call(
        functools.partial(_attn_kernel, tq=tq, heads=heads),
        grid_spec=grid_spec,
        out_shape=jax.ShapeDtypeStruct((S, H * HEAD_DIM), BF16),
        compiler_params=_cparams(2, 48),
        name="fox_attn",
    )(flag, qkv, qkv, qkv, ck2, bvec)


def _router_kernel(x_ref, g_ref, sc_ref, sh_ref, rw_ref, rb_ref, o_ref):
    h = _norm_modulate(x_ref[...], g_ref[0], sc_ref[0], sh_ref[0])
    o_ref[...] = jnp.dot(h, rw_ref[...], preferred_element_type=F32,
                         precision=lax.Precision.HIGHEST) + rb_ref[...]


def _router(x, norm_g, mod, layer, rw_pad, rb_pad):
    S, D = x.shape
    tm = 512
    return pl.pallas_call(
        _router_kernel,
        grid=(S // tm,),
        in_specs=[pl.BlockSpec((tm, D), lambda i: (i, 0)),
                  _vec_spec(layer, D), _mod_spec(layer, 4, D), _mod_spec(layer, 3, D),
                  pl.BlockSpec((D, LANES), lambda i: (0, 0)),
                  pl.BlockSpec((1, LANES), lambda i: (0, 0))],
        out_specs=pl.BlockSpec((tm, LANES), lambda i: (i, 0)),
        out_shape=jax.ShapeDtypeStruct((S, LANES), F32),
        compiler_params=_cparams(1, 32),
        name="moe_router",
    )(x, norm_g, mod, mod, rw_pad, rb_pad)


def _row_copy(src_hbm, dst, src_row, dst_row, sem):
    return pltpu.make_async_copy(src_hbm.at[pl.ds(src_row, 1), :], dst.at[pl.ds(dst_row, 1), :], sem)


def _gathered_rows(idx_ref, src_hbm, buf, sem, tm, n_per_row):
    i = pl.program_id(0)

    def copies(step, slot, start):
        def body(r, c):
            for k in range(n_per_row):
                src_row = idx_ref[(step * tm + r) * n_per_row + k] if start else 0
                cp = _row_copy(src_hbm, buf.at[slot, k], src_row, r, sem.at[slot])
                cp.start() if start else cp.wait()
            return c
        lax.fori_loop(0, tm, body, 0, unroll=8)

    @pl.when(i == 0)
    def _():
        copies(0, 0, True)

    @pl.when(i + 1 < pl.num_programs(0))
    def _():
        copies(i + 1, (i + 1) % 2, True)

    slot = i % 2
    copies(i, slot, False)
    return slot


def _gather_norm_kernel(tok_ref, x_hbm, g_ref, sc_ref, sh_ref, o_ref, buf, sem, *, tm):
    slot = _gathered_rows(tok_ref, x_hbm, buf, sem, tm, 1)
    o_ref[...] = _norm_modulate(buf[slot, 0], g_ref[0], sc_ref[0], sh_ref[0]).astype(o_ref.dtype)


def _gather_norm(tok_sorted, x, norm_g, mod, layer):
    S, D = x.shape
    n_rows = tok_sorted.shape[0]
    tm = 256
    grid_spec = pltpu.PrefetchScalarGridSpec(
        num_scalar_prefetch=1,
        grid=(n_rows // tm,),
        in_specs=[pl.BlockSpec(memory_space=pl.ANY),
                  pl.BlockSpec((1, 1, D), lambda i, t: (layer, 0, 0)),
                  pl.BlockSpec((1, 1, D), lambda i, t: (layer, 0, 4)),
                  pl.BlockSpec((1, 1, D), lambda i, t: (layer, 0, 3))],
        out_specs=pl.BlockSpec((tm, D), lambda i, t: (i, 0)),
        scratch_shapes=[pltpu.VMEM((2, 1, tm, D), F32), pltpu.SemaphoreType.DMA((2,))])
    return pl.pallas_call(
        functools.partial(_gather_norm_kernel, tm=tm),
        grid_spec=grid_spec,
        out_shape=jax.ShapeDtypeStruct((n_rows, D), BF16),
        compiler_params=_cparams(1, 32),
        name="moe_gather",
    )(tok_sorted, x, norm_g, mod, mod)


def _for_expert_sub_blocks(o_ref, tile, lo, hi, compute):
    tm = o_ref.shape[0]
    for sb in range(tm // MOE_SUB):
        rs = slice(sb * MOE_SUB, (sb + 1) * MOE_SUB)
        start = tile * tm + sb * MOE_SUB

        @pl.when((hi > start) & (lo < start + MOE_SUB))
        def _(rs=rs, start=start):
            val = compute(rs).astype(o_ref.dtype)

            @pl.when(lo <= start)
            def _():
                o_ref[rs, :] = val

            @pl.when(lo > start)
            def _():
                rows = start + lax.broadcasted_iota(jnp.int32, (MOE_SUB, 1), 0)
                keep = (rows >= lo) & (rows < hi)
                o_ref[rs, :] = jnp.where(keep, val, o_ref[rs, :])


def _expert_changed(exp_ref, i):
    return (i == 0) | (exp_ref[i] != exp_ref[jnp.maximum(i - 1, 0)])


def _moe_gu_kernel(tile_ref, exp_ref, valid_ref, lo_ref, hi_ref, a_ref, wg_ref, wu_ref, o_ref, wgb, wub):
    i = pl.program_id(1)

    @pl.when(_expert_changed(exp_ref, i))
    def _():
        wgb[...] = wg_ref[...].astype(BF16)
        wub[...] = wu_ref[...].astype(BF16)

    def compute(rs):
        a = a_ref[rs, :]
        return _silu(_dot(a, wgb[...])) * _dot(a, wub[...])

    @pl.when(valid_ref[i] == 1)
    def _():
        _for_expert_sub_blocks(o_ref, tile_ref[i], lo_ref[i], hi_ref[i], compute)


def _moe_gate_up(items, xs, gate_w, up_w, j):
    n_rows, D = xs.shape
    F = gate_w.shape[-1]
    tm, tn = MOE_GU_TM, 512
    n_items = items[0].shape[0]
    w_spec = pl.BlockSpec((None, None, D, tn), lambda n, i, tile, exp, *_: (j, exp[i], 0, n))
    grid_spec = pltpu.PrefetchScalarGridSpec(
        num_scalar_prefetch=len(items),
        grid=(F // tn, n_items),
        in_specs=[pl.BlockSpec((tm, D), lambda n, i, tile, *_: (tile[i], 0)), w_spec, w_spec],
        out_specs=pl.BlockSpec((tm, tn), lambda n, i, tile, *_: (tile[i], n)),
        scratch_shapes=[pltpu.VMEM((D, tn), BF16), pltpu.VMEM((D, tn), BF16)])
    return pl.pallas_call(
        _moe_gu_kernel,
        grid_spec=grid_spec,
        out_shape=jax.ShapeDtypeStruct((n_rows, F), BF16),
        compiler_params=_cparams(2, 48),
        name="moe_gate_up",
    )(*items, xs, gate_w, up_w)


def _moe_down_kernel(tile_ref, exp_ref, valid_ref, lo_ref, hi_ref, a_ref, w_ref, o_ref, wb):
    i = pl.program_id(1)

    @pl.when(_expert_changed(exp_ref, i))
    def _():
        wb[...] = w_ref[...].astype(BF16)

    @pl.when(valid_ref[i] == 1)
    def _():
        _for_expert_sub_blocks(o_ref, tile_ref[i], lo_ref[i], hi_ref[i],
                               lambda rs: _dot(a_ref[rs, :], wb[...]))


def _moe_down(items, hid, down_w, j):
    n_rows, F = hid.shape
    D = down_w.shape[-1]
    tm, tn = MOE_DOWN_TM, 512
    n_items = items[0].shape[0]
    grid_spec = pltpu.PrefetchScalarGridSpec(
        num_scalar_prefetch=len(items),
        grid=(D // tn, n_items),
        in_specs=[pl.BlockSpec((tm, F), lambda n, i, tile, *_: (tile[i], 0)),
                  pl.BlockSpec((None, None, F, tn), lambda n, i, tile, exp, *_: (j, exp[i], 0, n))],
        out_specs=pl.BlockSpec((tm, tn), lambda n, i, tile, *_: (tile[i], n)),
        scratch_shapes=[pltpu.VMEM((F, tn), BF16)])
    return pl.pallas_call(
        _moe_down_kernel,
        grid_spec=grid_spec,
        out_shape=jax.ShapeDtypeStruct((n_rows, D), F32),
        compiler_params=_cparams(2, 56),
        name="moe_down",
    )(*items, hid, down_w)


def _combine_kernel(pos_ref, y_hbm, x_ref, g_ref, w_ref, o_ref, buf, sem, *, tm):
    slot = _gathered_rows(pos_ref, y_hbm, buf, sem, tm, MOE_TOP_K)
    w = w_ref[...]
    y = buf[slot, 0] * w[:, 0:1] + buf[slot, 1] * w[:, 1:2]
    o_ref[...] = x_ref[...] + g_ref[0] * y


def _combine(pos, y_sorted, x, mod, layer, top_w):
    S, D = x.shape
    tm = 256
    grid_spec = pltpu.PrefetchScalarGridSpec(
        num_scalar_prefetch=1,
        grid=(S // tm,),
        in_specs=[pl.BlockSpec(memory_space=pl.ANY),
                  pl.BlockSpec((tm, D), lambda i, p: (i, 0)),
                  pl.BlockSpec((1, 1, D), lambda i, p: (layer, 0, 5)),
                  pl.BlockSpec((tm, MOE_TOP_K), lambda i, p: (i, 0))],
        out_specs=pl.BlockSpec((tm, D), lambda i, p: (i, 0)),
        scratch_shapes=[pltpu.VMEM((2, MOE_TOP_K, tm, D), F32), pltpu.SemaphoreType.DMA((2,))])
    return pl.pallas_call(
        functools.partial(_combine_kernel, tm=tm),
        grid_spec=grid_spec,
        out_shape=jax.ShapeDtypeStruct((S, D), F32),
        compiler_params=_cparams(1, 32),
        name="moe_combine",
    )(pos, y_sorted, x, mod, top_w)


def _work_items(starts, ends, counts, n_rows, tm):
    n_experts = counts.shape[0]
    n_items = n_rows // tm + n_experts - 1
    first_tile = starts // tm
    n_e = jnp.where(counts > 0, (ends + tm - 1) // tm - first_tile, 0)
    item_end = jnp.cumsum(n_e)
    item_start = item_end - n_e
    total = item_end[-1]
    idx = jnp.arange(n_items, dtype=jnp.int32)
    idx_c = jnp.minimum(idx, total - 1)
    item_e = jnp.minimum(jnp.searchsorted(item_end, idx_c, side='right'), n_experts - 1).astype(jnp.int32)
    item_tile = (first_tile[item_e] + idx_c - item_start[item_e]).astype(jnp.int32)
    return (item_tile, item_e, (idx < total).astype(jnp.int32),
            starts[item_e].astype(jnp.int32), ends[item_e].astype(jnp.int32))


def _route(logits, n_experts):
    S = logits.shape[0]
    top_logit, top_idx = lax.top_k(logits, MOE_TOP_K)
    top_w = jax.nn.softmax(top_logit, axis=-1)
    n_assign = S * MOE_TOP_K
    flat_e = top_idx.reshape(n_assign).astype(jnp.int32)
    onehot = (flat_e[:, None] == jnp.arange(n_experts, dtype=jnp.int32)[None, :]).astype(jnp.int32)
    csum = jnp.cumsum(onehot, axis=0)
    rank = jnp.sum((csum - 1) * onehot, axis=1)
    counts = csum[-1]
    ends = jnp.cumsum(counts)
    starts = ends - counts
    pos = (starts[flat_e] + rank).astype(jnp.int32)
    flat_tok = jnp.arange(n_assign, dtype=jnp.int32) // MOE_TOP_K
    tok_sorted = jnp.zeros((n_assign,), jnp.int32).at[pos].set(flat_tok)
    return tok_sorted, pos, top_w, (starts, ends, counts)


def kernel(x, c, ada_w, ada_b, mix_norm, ffn_norm, conv_pw1_w, conv_pw1_b, conv_dw_w, conv_dw_b, conv_ln_g, conv_ln_b, conv_pw2_w, conv_pw2_b, fox_qkv_w, fox_o_w, fox_fgate_w, fox_fgate_b, fox_q_norm, fox_k_norm, ffn_gate_w, ffn_up_w, ffn_down_w, moe_router_w, moe_router_b, moe_gate_w, moe_up_w, moe_down_w):
    B, S, D = x.shape
    assert B == 1, "kernels are written for a single sequence"
    L = ada_w.shape[0]
    H = fox_fgate_w.shape[-1]
    E = moe_router_w.shape[-1]

    def as_rows(v):
        return v.reshape(v.shape[0], 1, v.shape[1])

    mod = _adaln(c.reshape(D, 1), ada_w, ada_b)
    mix_g, ffn_g = as_rows(mix_norm), as_rows(ffn_norm)
    zero_bias = jnp.zeros((1, 1, D), F32)
    xs = x.reshape(S, D)

    for i in range(L):
        j = i // 2
        h = _norm_mod(xs, mix_g, mod, i, 0)
        if i % 2 == 0:
            u = _pw1_glu(h, conv_pw1_w, as_rows(conv_pw1_b), j)
            v = _conv_ln(u, conv_dw_w, as_rows(conv_dw_b), as_rows(conv_ln_g), as_rows(conv_ln_b), j)
            xs = _mm_res(v, conv_pw2_w, as_rows(conv_pw2_b), xs, mod, j, i, 2, 1024, 512, "conv_pw2")
        else:
            q_gain = fox_q_norm[j] * (HEAD_DIM ** -0.5 * LOG2E)
            gain = jnp.concatenate([jnp.tile(q_gain, H), jnp.tile(fox_k_norm[j], H),
                                    jnp.ones((D,), F32)])[None, :]
            qkv = _qkv(h, fox_qkv_w, gain, j)
            cum = _fgate_cum(h, jnp.swapaxes(fox_fgate_w, 1, 2), fox_fgate_b, j)
            bound = 1.02 * HEAD_DIM * jnp.max(jnp.abs(q_gain)) * jnp.max(jnp.abs(fox_k_norm[j]))
            o = _attention(qkv, (cum * LOG2E).reshape(H, 1, S), bound)
            xs = _mm_res(o, fox_o_w, zero_bias, xs, mod, j, i, 2, 1024, 512, "fox_o")
        if i % 2 == 0:
            h = _norm_mod(xs, ffn_g, mod, i, 3)
            hid = _gate_up(h, ffn_gate_w, ffn_up_w, j)
            xs = _mm_res(hid, ffn_down_w, zero_bias, xs, mod, j, i, 5, 1024, 256, "ffn_down")
        else:
            rw_pad = jnp.zeros((D, LANES), F32).at[:, :E].set(moe_router_w[j])
            rb_pad = jnp.zeros((1, LANES), F32).at[0, :E].set(moe_router_b[j])
            logits = _router(xs, ffn_g, mod, i, rw_pad, rb_pad)[:, :E]
            tok_sorted, pos, top_w, ranges = _route(logits, E)
            n_rows = tok_sorted.shape[0]
            xg = _gather_norm(tok_sorted, xs, ffn_g, mod, i)
            hid = _moe_gate_up(_work_items(*ranges, n_rows, MOE_GU_TM), xg, moe_gate_w, moe_up_w, j)
            y_sorted = _moe_down(_work_items(*ranges, n_rows, MOE_DOWN_TM), hid, moe_down_w, j)
            xs = _combine(pos, y_sorted, xs, mod, i, top_w)
    return xs.reshape(B, S, D)
```

```python
import functools

import jax
import jax.numpy as jnp
from jax import lax
from jax.experimental import pallas as pl
from jax.experimental.pallas import tpu as pltpu

F32 = jnp.float32
BF16 = jnp.bfloat16

NORM_EPS = 1e-6
HEAD_DIM = 128
CONV_WIDTH = 31
CONV_HALO = 32
MOE_TOP_K = 2
LOG2E = 1.4426950408889634
MASK_VALUE = -1e30
ATTN_FAST_MAX_BOUND = 40.0
LANES = 128
SUBLANES = 8
BIG_VMEM_MB = 57
DENSE_TM = 2048
MOE_SUB = 256
MOE_GU_TM = 512
MOE_DOWN_TM = 512


def _cparams(n_axes, vmem_mb):
    return pltpu.CompilerParams(
        dimension_semantics=("arbitrary",) * n_axes,
        vmem_limit_bytes=vmem_mb << 20)


def _norm_modulate(x, g, sc, sh):
    ms = jnp.mean(x * x, axis=-1, keepdims=True)
    y = x * lax.rsqrt(ms + NORM_EPS) * g
    return y * (1.0 + sc) + sh


def _silu(x):
    return x * jax.nn.sigmoid(x)


def _adaln_kernel(c_ref, w_ref, b_ref, o_ref):
    c = c_ref[...]
    o_ref[0] = jnp.sum(w_ref[0] * _silu(c), axis=0, keepdims=True) + b_ref[0]


def _adaln(c_col, ada_w, ada_b):
    L, D, N = ada_w.shape
    tn = 1024
    return pl.pallas_call(
        _adaln_kernel,
        grid=(L, N // tn),
        in_specs=[pl.BlockSpec((D, 1), lambda l, j: (0, 0)),
                  pl.BlockSpec((1, D, tn), lambda l, j: (l, 0, j)),
                  pl.BlockSpec((1, 1, tn), lambda l, j: (l, 0, j))],
        out_specs=pl.BlockSpec((1, 1, tn), lambda l, j: (l, 0, j)),
        out_shape=jax.ShapeDtypeStruct((L, 1, N), F32),
        compiler_params=_cparams(2, 40),
        name="adaln",
    )(c_col, ada_w, ada_b.reshape(L, 1, N))


def _mod_spec(layer, which, D):
    return pl.BlockSpec((1, 1, D), lambda *_: (layer, 0, which))


def _vec_spec(layer, D):
    return pl.BlockSpec((1, 1, D), lambda *_: (layer, 0, 0))


def _norm_mod_kernel(x_ref, g_ref, sc_ref, sh_ref, o_ref):
    o_ref[...] = _norm_modulate(x_ref[...], g_ref[0], sc_ref[0], sh_ref[0]).astype(o_ref.dtype)


def _norm_mod(x, norm_g, mod, layer, which_shift):
    S, D = x.shape
    tm = 512
    return pl.pallas_call(
        _norm_mod_kernel,
        grid=(S // tm,),
        in_specs=[pl.BlockSpec((tm, D), lambda i: (i, 0)),
                  _vec_spec(layer, D),
                  _mod_spec(layer, which_shift + 1, D),
                  _mod_spec(layer, which_shift, D)],
        out_specs=pl.BlockSpec((tm, D), lambda i: (i, 0)),
        out_shape=jax.ShapeDtypeStruct((S, D), BF16),
        compiler_params=_cparams(1, 32),
        name="norm_mod",
    )(x, norm_g, mod, mod)


def _dot(a, w):
    return jnp.dot(a, w, preferred_element_type=F32)


def _pw1_glu_kernel(a_ref, wv_ref, wg_ref, bv_ref, bg_ref, o_ref):
    a = a_ref[...]
    val = _dot(a, wv_ref[...].astype(BF16)) + bv_ref[0]
    gate = _dot(a, wg_ref[...].astype(BF16)) + bg_ref[0]
    o_ref[...] = val * jax.nn.sigmoid(gate)


def _pw1_glu(h, pw1_w, pw1_b, j):
    S, D = h.shape
    tm, tn = DENSE_TM, 512
    nt = D // tn
    return pl.pallas_call(
        _pw1_glu_kernel,
        grid=(S // tm, nt),
        in_specs=[pl.BlockSpec((tm, D), lambda i, n: (i, 0)),
                  pl.BlockSpec((None, D, tn), lambda i, n: (j, 0, n)),
                  pl.BlockSpec((None, D, tn), lambda i, n: (j, 0, n + nt)),
                  pl.BlockSpec((1, 1, tn), lambda i, n: (j, 0, n)),
                  pl.BlockSpec((1, 1, tn), lambda i, n: (j, 0, n + nt))],
        out_specs=pl.BlockSpec((tm, tn), lambda i, n: (i, n)),
        out_shape=jax.ShapeDtypeStruct((S, D), F32),
        compiler_params=_cparams(2, BIG_VMEM_MB),
        name="pw1_glu",
    )(h, pw1_w, pw1_w, pw1_b, pw1_b)


def _gate_up_kernel(a_ref, wg_ref, wu_ref, o_ref):
    a = a_ref[...]
    g = _dot(a, wg_ref[...].astype(BF16))
    u = _dot(a, wu_ref[...].astype(BF16))
    o_ref[...] = (_silu(g) * u).astype(o_ref.dtype)


def _gate_up(h, gate_w, up_w, j):
    S, D = h.shape
    F = gate_w.shape[-1]
    tm, tn = DENSE_TM, 512
    return pl.pallas_call(
        _gate_up_kernel,
        grid=(S // tm, F // tn),
        in_specs=[pl.BlockSpec((tm, D), lambda i, n: (i, 0)),
                  pl.BlockSpec((None, D, tn), lambda i, n: (j, 0, n)),
                  pl.BlockSpec((None, D, tn), lambda i, n: (j, 0, n))],
        out_specs=pl.BlockSpec((tm, tn), lambda i, n: (i, n)),
        out_shape=jax.ShapeDtypeStruct((S, F), BF16),
        compiler_params=_cparams(2, BIG_VMEM_MB),
        name="ffn_gate_up",
    )(h, gate_w, up_w)


def _mm_res_kernel(a_ref, w_ref, b_ref, x_ref, g_ref, o_ref):
    y = _dot(a_ref[...], w_ref[...].astype(BF16)) + b_ref[0]
    o_ref[...] = x_ref[...] + g_ref[0] * y


def _mm_res(a, w, bias, x, mod, j, layer, which_gate, tm, tn, name):
    S, K = a.shape
    D = w.shape[-1]
    bj = j if bias.shape[0] > 1 else 0
    return pl.pallas_call(
        _mm_res_kernel,
        grid=(S // tm, D // tn),
        in_specs=[pl.BlockSpec((tm, K), lambda i, n: (i, 0)),
                  pl.BlockSpec((None, K, tn), lambda i, n: (j, 0, n)),
                  pl.BlockSpec((1, 1, tn), lambda i, n: (bj, 0, n)),
                  pl.BlockSpec((tm, tn), lambda i, n: (i, n)),
                  pl.BlockSpec((1, 1, tn), lambda i, n: (layer, 0, which_gate * (D // tn) + n))],
        out_specs=pl.BlockSpec((tm, tn), lambda i, n: (i, n)),
        out_shape=jax.ShapeDtypeStruct((S, D), F32),
        compiler_params=_cparams(2, BIG_VMEM_MB),
        name=name,
    )(a, w, bias, x, mod)


def _conv_ln_kernel(halo_ref, u_ref, w_ref, b_ref, g_ref, beta_ref, o_ref, win_ref, acc_ref, *, tm):
    i = pl.program_id(0)

    @pl.when(i == 0)
    def _():
        win_ref[0:CONV_HALO, :] = jnp.zeros((CONV_HALO, win_ref.shape[1]), F32)

    @pl.when(i > 0)
    def _():
        win_ref[0:CONV_HALO, :] = halo_ref[...]

    win_ref[CONV_HALO:CONV_HALO + tm, :] = u_ref[...]
    first = CONV_HALO - (CONV_WIDTH - 1)
    for c in range(u_ref.shape[1] // LANES):
        cs = slice(c * LANES, (c + 1) * LANES)
        out = None
        for b in range(SUBLANES):
            rows = tm if b == 0 else tm + SUBLANES
            y = None
            for j in range(first, first + CONV_WIDTH):
                if j % SUBLANES != b:
                    continue
                term = win_ref[j - b:j - b + rows, cs] * w_ref[0, j - first:j - first + 1, cs]
                y = term if y is None else y + term
            yb = y if b == 0 else y[b:b + tm]
            out = yb if out is None else out + yb
        acc_ref[:, cs] = out + b_ref[0][:, cs]
    y = acc_ref[...]
    mu = jnp.mean(y, axis=-1, keepdims=True)
    yc = y - mu
    var = jnp.mean(yc * yc, axis=-1, keepdims=True)
    z = yc * lax.rsqrt(var + NORM_EPS) * g_ref[0] + beta_ref[0]
    o_ref[...] = _silu(z).astype(o_ref.dtype)


def _conv_ln(u, dw_w, dw_b, ln_g, ln_b, j):
    S, D = u.shape
    tm = 128
    hb = tm // CONV_HALO
    return pl.pallas_call(
        functools.partial(_conv_ln_kernel, tm=tm),
        grid=(S // tm,),
        in_specs=[pl.BlockSpec((CONV_HALO, D), lambda i: (jnp.maximum(i * hb - 1, 0), 0)),
                  pl.BlockSpec((tm, D), lambda i: (i, 0)),
                  pl.BlockSpec((1, CONV_WIDTH, D), lambda i: (j, 0, 0)),
                  _vec_spec(j, D), _vec_spec(j, D), _vec_spec(j, D)],
        out_specs=pl.BlockSpec((tm, D), lambda i: (i, 0)),
        out_shape=jax.ShapeDtypeStruct((S, D), BF16),
        scratch_shapes=[pltpu.VMEM((CONV_HALO + tm, D), F32), pltpu.VMEM((tm, D), F32)],
        compiler_params=_cparams(1, 32),
        name="conv_ln",
    )(u, u, dw_w, dw_b, ln_g, ln_b)


def _qkv_kernel(a_ref, w_ref, gain_ref, o_ref, *, n_norm_tiles):
    n = pl.program_id(1)
    acc = _dot(a_ref[...], w_ref[...].astype(BF16))

    @pl.when(n < n_norm_tiles)
    def _():
        gain = gain_ref[...]
        for h in range(acc.shape[1] // HEAD_DIM):
            sl = slice(h * HEAD_DIM, (h + 1) * HEAD_DIM)
            t = acc[:, sl]
            ms = jnp.mean(t * t, axis=-1, keepdims=True)
            o_ref[:, sl] = (t * lax.rsqrt(ms + NORM_EPS) * gain[:, sl]).astype(o_ref.dtype)

    @pl.when(n >= n_norm_tiles)
    def _():
        o_ref[...] = acc.astype(o_ref.dtype)


def _qkv(h, qkv_w, gain, j):
    S, D = h.shape
    N = qkv_w.shape[-1]
    tm, tn = DENSE_TM, 512
    return pl.pallas_call(
        functools.partial(_qkv_kernel, n_norm_tiles=2 * D // tn),
        grid=(S // tm, N // tn),
        in_specs=[pl.BlockSpec((tm, D), lambda i, n: (i, 0)),
                  pl.BlockSpec((None, D, tn), lambda i, n: (j, 0, n)),
                  pl.BlockSpec((1, tn), lambda i, n: (0, n))],
        out_specs=pl.BlockSpec((tm, tn), lambda i, n: (i, n)),
        out_shape=jax.ShapeDtypeStruct((S, N), BF16),
        compiler_params=_cparams(2, BIG_VMEM_MB),
        name="fox_qkv",
    )(h, qkv_w, gain)


def _fgate_kernel(h_ref, fwt_ref, fb_ref, tri_ref, o_ref, carry_ref):
    i = pl.program_id(0)

    @pl.when(i == 0)
    def _():
        carry_ref[...] = jnp.zeros(carry_ref.shape, F32)

    z = lax.dot_general(fwt_ref[...].astype(BF16), h_ref[...], (((1,), (1,)), ((), ())),
                        preferred_element_type=F32) + fb_ref[...][:, 0:1]
    lf = jnp.minimum(z, 0.0) - jnp.log1p(jnp.exp(-jnp.abs(z)))
    hi = lf.astype(BF16)
    r1 = lf - hi.astype(F32)
    mid = r1.astype(BF16)
    lo = (r1 - mid.astype(F32)).astype(BF16)
    tri = tri_ref[...]
    cs = _dot(hi, tri) + _dot(mid, tri) + _dot(lo, tri) + carry_ref[...][:, 0:1]
    o_ref[...] = cs
    carry_ref[...] = jnp.broadcast_to(cs[:, cs.shape[1] - 1:], carry_ref.shape)


def _fgate_cum(h, fgate_wt, fgate_b, j):
    S, D = h.shape
    H = fgate_wt.shape[1]
    tm = 512
    tri = jnp.triu(jnp.ones((tm, tm), F32)).astype(BF16)
    fb = jnp.broadcast_to(fgate_b[j][:, None], (H, LANES))
    return pl.pallas_call(
        _fgate_kernel,
        grid=(S // tm,),
        in_specs=[pl.BlockSpec((tm, D), lambda i: (i, 0)),
                  pl.BlockSpec((None, H, D), lambda i: (j, 0, 0)),
                  pl.BlockSpec((H, LANES), lambda i: (0, 0)),
                  pl.BlockSpec((tm, tm), lambda i: (0, 0))],
        out_specs=pl.BlockSpec((H, tm), lambda i: (0, i)),
        out_shape=jax.ShapeDtypeStruct((H, S), F32),
        scratch_shapes=[pltpu.VMEM((H, LANES), F32)],
        compiler_params=_cparams(1, 32),
        name="fox_fgate",
    )(h, fgate_wt, fb, tri)


def _attn_kernel(flag_ref, q_ref, k_ref, v_ref, ck_ref, b_ref, o_ref,
                 acc_ref, m_ref, vaug_ref, rt_ref, *, tq, heads):
    qi = pl.program_id(1)
    S = k_ref.shape[0]
    hs = [slice(g * HEAD_DIM, (g + 1) * HEAD_DIM) for g in range(heads)]
    acc_ref[...] = jnp.zeros(acc_ref.shape, F32)

    @pl.when(qi == 0)
    def _():
        for g in range(heads):
            vaug_ref[g, :, 0:HEAD_DIM] = v_ref[:, hs[g]]
            vaug_ref[g, :, HEAD_DIM:] = jnp.ones((S, HEAD_DIM), BF16)

    def scores(g, ks, width):
        return lax.dot_general(q_ref[:, hs[g]], k_ref[pl.ds(ks, width), hs[g]],
                               (((1,), (1,)), ((), ())), preferred_element_type=F32)

    def visible(c):
        row = lax.broadcasted_iota(jnp.int32, (tq, LANES), 0)
        col = lax.broadcasted_iota(jnp.int32, (tq, LANES), 1) + c * LANES
        return col <= row

    def fast_step(ki, width, masked):
        ks = pl.multiple_of(ki * tq, tq)
        for g in range(heads):
            s = scores(g, ks, width)
            ckr = ck_ref[g, :, pl.ds(ks, width)]
            rt = rt_ref[g]
            ps = []
            for c in range(width // LANES):
                sl = slice(c * LANES, (c + 1) * LANES)
                t = s[:, sl] + (rt - ckr[:, sl])
                if masked:
                    t = jnp.where(visible(c), t, MASK_VALUE)
                ps.append(jnp.exp2(t).astype(BF16))
            p = jnp.concatenate(ps, axis=1)
            acc_ref[g] += _dot(p, vaug_ref[g, pl.ds(ks, width), :])

    def slow_step(ki, width, masked):
        ks = pl.multiple_of(ki * tq, tq)
        for g in range(heads):
            t = scores(g, ks, width) - ck_ref[g, :, pl.ds(ks, width)]
            if masked:
                t = jnp.concatenate(
                    [jnp.where(visible(c), t[:, c * LANES:(c + 1) * LANES], MASK_VALUE)
                     for c in range(width // LANES)], axis=1)
            m_prev = m_ref[g]
            m_new = jnp.maximum(m_prev, jnp.max(t, axis=-1, keepdims=True))
            alpha = jnp.exp2(m_prev - m_new)
            p = jnp.exp2(t - m_new[:, 0:1])
            pv = _dot(p.astype(BF16), vaug_ref[g, pl.ds(ks, width), :])
            acc_ref[g] = acc_ref[g] * jnp.concatenate([alpha, alpha], axis=1) + pv
            m_ref[g] = m_new

    def sweep(step):
        def body(kk, carry):
            step(2 * kk, 2 * tq, False)
            return carry
        lax.fori_loop(0, qi // 2, body, 0)

        @pl.when(qi % 2 == 1)
        def _():
            step(qi - 1, tq, False)

        step(qi, tq, True)

    @pl.when(flag_ref[0] == 1)
    def _():
        qs = pl.multiple_of(qi * tq, tq)
        for g in range(heads):
            ckq = jnp.broadcast_to(ck_ref[g, :, pl.ds(qs, tq)], (LANES, tq))
            rt_ref[g] = ckq.T - b_ref[...]
        sweep(fast_step)

    @pl.when(flag_ref[0] != 1)
    def _():
        m_ref[...] = jnp.full(m_ref.shape, MASK_VALUE, F32)
        sweep(slow_step)

    for g in range(heads):
        acc = acc_ref[g]
        o_ref[:, hs[g]] = (acc[:, 0:HEAD_DIM] / acc[:, HEAD_DIM:]).astype(o_ref.dtype)


def _attention(qkv, ck2, bound):
    S = qkv.shape[0]
    H = ck2.shape[0]
    tq, heads = 512, 2
    hw = heads * HEAD_DIM
    flag = (bound <= ATTN_FAST_MAX_BOUND).astype(jnp.int32).reshape(1)
    bvec = jnp.broadcast_to(bound.astype(F32), (1, LANES))
    grid_spec = pltpu.PrefetchScalarGridSpec(
        num_scalar_prefetch=1,
        grid=(H // heads, S // tq),
        in_specs=[pl.BlockSpec((tq, hw), lambda h, i, f: (i, h)),
                  pl.BlockSpec((S, hw), lambda h, i, f: (0, H // heads + h)),
                  pl.BlockSpec((S, hw), lambda h, i, f: (0, 2 * (H // heads) + h)),
                  pl.BlockSpec((heads, 1, S), lambda h, i, f: (h, 0, 0)),
                  pl.BlockSpec((1, LANES), lambda h, i, f: (0, 0))],
        out_specs=pl.BlockSpec((tq, hw), lambda h, i, f: (i, h)),
        scratch_shapes=[pltpu.VMEM((heads, tq, 2 * HEAD_DIM), F32),
                        pltpu.VMEM((heads, tq, HEAD_DIM), F32),
                        pltpu.VMEM((heads, S, 2 * HEAD_DIM), BF16),
                        pltpu.VMEM((heads, tq, LANES), F32)])
    return pl.pallas_call(
        functools.partial(_attn_kernel, tq=tq, heads=heads),
        grid_spec=grid_spec,
        out_shape=jax.ShapeDtypeStruct((S, H * HEAD_DIM), BF16),
        compiler_params=_cparams(2, BIG_VMEM_MB),
        name="fox_attn",
    )(flag, qkv, qkv, qkv, ck2, bvec)


def _router_kernel(x_ref, g_ref, sc_ref, sh_ref, rw_ref, rb_ref, o_ref):
    h = _norm_modulate(x_ref[...], g_ref[0], sc_ref[0], sh_ref[0])
    o_ref[...] = jnp.dot(h, rw_ref[...], preferred_element_type=F32,
                         precision=lax.Precision.HIGHEST) + rb_ref[...]


def _router(x, norm_g, mod, layer, rw_pad, rb_pad):
    S, D = x.shape
    tm = 512
    return pl.pallas_call(
        _router_kernel,
        grid=(S // tm,),
        in_specs=[pl.BlockSpec((tm, D), lambda i: (i, 0)),
                  _vec_spec(layer, D), _mod_spec(layer, 4, D), _mod_spec(layer, 3, D),
                  pl.BlockSpec((D, LANES), lambda i: (0, 0)),
                  pl.BlockSpec((1, LANES), lambda i: (0, 0))],
        out_specs=pl.BlockSpec((tm, LANES), lambda i: (i, 0)),
        out_shape=jax.ShapeDtypeStruct((S, LANES), F32),
        compiler_params=_cparams(1, 32),
        name="moe_router",
    )(x, norm_g, mod, mod, rw_pad, rb_pad)


def _row_copy(src_hbm, dst, src_row, dst_row, sem):
    return pltpu.make_async_copy(src_hbm.at[pl.ds(src_row, 1), :], dst.at[pl.ds(dst_row, 1), :], sem)


def _gathered_rows(idx_ref, src_hbm, buf, sem, tm, n_per_row):
    i = pl.program_id(0)

    def copies(step, slot, start):
        def body(r, c):
            for k in range(n_per_row):
                src_row = idx_ref[(step * tm + r) * n_per_row + k] if start else 0
                cp = _row_copy(src_hbm, buf.at[slot, k], src_row, r, sem.at[slot])
                cp.start() if start else cp.wait()
            return c
        lax.fori_loop(0, tm, body, 0, unroll=8)

    @pl.when(i == 0)
    def _():
        copies(0, 0, True)

    @pl.when(i + 1 < pl.num_programs(0))
    def _():
        copies(i + 1, (i + 1) % 2, True)

    slot = i % 2
    copies(i, slot, False)
    return slot


def _gather_norm_kernel(tok_ref, x_hbm, g_ref, sc_ref, sh_ref, o_ref, buf, sem, *, tm):
    slot = _gathered_rows(tok_ref, x_hbm, buf, sem, tm, 1)
    o_ref[...] = _norm_modulate(buf[slot, 0], g_ref[0], sc_ref[0], sh_ref[0]).astype(o_ref.dtype)


def _gather_norm(tok_sorted, x, norm_g, mod, layer):
    S, D = x.shape
    n_rows = tok_sorted.shape[0]
    tm = 256
    grid_spec = pltpu.PrefetchScalarGridSpec(
        num_scalar_prefetch=1,
        grid=(n_rows // tm,),
        in_specs=[pl.BlockSpec(memory_space=pl.ANY),
                  pl.BlockSpec((1, 1, D), lambda i, t: (layer, 0, 0)),
                  pl.BlockSpec((1, 1, D), lambda i, t: (layer, 0, 4)),
                  pl.BlockSpec((1, 1, D), lambda i, t: (layer, 0, 3))],
        out_specs=pl.BlockSpec((tm, D), lambda i, t: (i, 0)),
        scratch_shapes=[pltpu.VMEM((2, 1, tm, D), F32), pltpu.SemaphoreType.DMA((2,))])
    return pl.pallas_call(
        functools.partial(_gather_norm_kernel, tm=tm),
        grid_spec=grid_spec,
        out_shape=jax.ShapeDtypeStruct((n_rows, D), BF16),
        compiler_params=_cparams(1, 32),
        name="moe_gather",
    )(tok_sorted, x, norm_g, mod, mod)


def _for_expert_sub_blocks(o_ref, tile, lo, hi, compute):
    tm = o_ref.shape[0]
    whole = (lo <= tile * tm) & (hi >= (tile + 1) * tm)

    @pl.when(whole)
    def _():
        o_ref[...] = compute(slice(0, tm)).astype(o_ref.dtype)

    for sb in range(tm // MOE_SUB):
        rs = slice(sb * MOE_SUB, (sb + 1) * MOE_SUB)
        start = tile * tm + sb * MOE_SUB

        @pl.when(jnp.logical_not(whole) & (hi > start) & (lo < start + MOE_SUB))
        def _(rs=rs, start=start):
            val = compute(rs).astype(o_ref.dtype)

            @pl.when(lo <= start)
            def _():
                o_ref[rs, :] = val

            @pl.when(lo > start)
            def _():
                rows = start + lax.broadcasted_iota(jnp.int32, (MOE_SUB, 1), 0)
                keep = (rows >= lo) & (rows < hi)
                o_ref[rs, :] = jnp.where(keep, val, o_ref[rs, :])


def _expert_changed(exp_ref, i):
    return (i == 0) | (exp_ref[i] != exp_ref[jnp.maximum(i - 1, 0)])


def _moe_gu_kernel(tile_ref, exp_ref, valid_ref, lo_ref, hi_ref, a_ref, wg_ref, wu_ref, o_ref, wgb, wub):
    i = pl.program_id(1)

    @pl.when(_expert_changed(exp_ref, i))
    def _():
        wgb[...] = wg_ref[...].astype(BF16)
        wub[...] = wu_ref[...].astype(BF16)

    def compute(rs):
        a = a_ref[rs, :]
        return _silu(_dot(a, wgb[...])) * _dot(a, wub[...])

    @pl.when(valid_ref[i] == 1)
    def _():
        _for_expert_sub_blocks(o_ref, tile_ref[i], lo_ref[i], hi_ref[i], compute)


def _moe_gate_up(items, xs, gate_w, up_w, j):
    n_rows, D = xs.shape
    F = gate_w.shape[-1]
    tm, tn = MOE_GU_TM, 1024
    n_items = items[0].shape[0]
    w_spec = pl.BlockSpec((None, None, D, tn), lambda n, i, tile, exp, *_: (j, exp[i], 0, n))
    grid_spec = pltpu.PrefetchScalarGridSpec(
        num_scalar_prefetch=len(items),
        grid=(F // tn, n_items),
        in_specs=[pl.BlockSpec((tm, D), lambda n, i, tile, *_: (tile[i], 0)), w_spec, w_spec],
        out_specs=pl.BlockSpec((tm, tn), lambda n, i, tile, *_: (tile[i], n)),
        scratch_shapes=[pltpu.VMEM((D, tn), BF16), pltpu.VMEM((D, tn), BF16)])
    return pl.pallas_call(
        _moe_gu_kernel,
        grid_spec=grid_spec,
        out_shape=jax.ShapeDtypeStruct((n_rows, F), BF16),
        compiler_params=_cparams(2, BIG_VMEM_MB),
        name="moe_gate_up",
    )(*items, xs, gate_w, up_w)


def _moe_down_kernel(tile_ref, exp_ref, valid_ref, lo_ref, hi_ref, a_ref, w_ref, o_ref, wb):
    i = pl.program_id(1)

    @pl.when(_expert_changed(exp_ref, i))
    def _():
        wb[...] = w_ref[...].astype(BF16)

    @pl.when(valid_ref[i] == 1)
    def _():
        _for_expert_sub_blocks(o_ref, tile_ref[i], lo_ref[i], hi_ref[i],
                               lambda rs: _dot(a_ref[rs, :], wb[...]))


def _moe_down(items, hid, down_w, j):
    n_rows, F = hid.shape
    D = down_w.shape[-1]
    tm, tn = MOE_DOWN_TM, 512
    n_items = items[0].shape[0]
    grid_spec = pltpu.PrefetchScalarGridSpec(
        num_scalar_prefetch=len(items),
        grid=(D // tn, n_items),
        in_specs=[pl.BlockSpec((tm, F), lambda n, i, tile, *_: (tile[i], 0)),
                  pl.BlockSpec((None, None, F, tn), lambda n, i, tile, exp, *_: (j, exp[i], 0, n))],
        out_specs=pl.BlockSpec((tm, tn), lambda n, i, tile, *_: (tile[i], n)),
        scratch_shapes=[pltpu.VMEM((F, tn), BF16)])
    return pl.pallas_call(
        _moe_down_kernel,
        grid_spec=grid_spec,
        out_shape=jax.ShapeDtypeStruct((n_rows, D), F32),
        compiler_params=_cparams(2, BIG_VMEM_MB),
        name="moe_down",
    )(*items, hid, down_w)


def _combine_kernel(pos_ref, y_hbm, x_ref, g_ref, w_ref, o_ref, buf, sem, *, tm):
    slot = _gathered_rows(pos_ref, y_hbm, buf, sem, tm, MOE_TOP_K)
    w = w_ref[...]
    y = buf[slot, 0] * w[:, 0:1] + buf[slot, 1] * w[:, 1:2]
    o_ref[...] = x_ref[...] + g_ref[0] * y


def _combine(pos, y_sorted, x, mod, layer, top_w):
    S, D = x.shape
    tm = 256
    grid_spec = pltpu.PrefetchScalarGridSpec(
        num_scalar_prefetch=1,
        grid=(S // tm,),
        in_specs=[pl.BlockSpec(memory_space=pl.ANY),
                  pl.BlockSpec((tm, D), lambda i, p: (i, 0)),
                  pl.BlockSpec((1, 1, D), lambda i, p: (layer, 0, 5)),
                  pl.BlockSpec((tm, MOE_TOP_K), lambda i, p: (i, 0))],
        out_specs=pl.BlockSpec((tm, D), lambda i, p: (i, 0)),
        scratch_shapes=[pltpu.VMEM((2, MOE_TOP_K, tm, D), F32), pltpu.SemaphoreType.DMA((2,))])
    return pl.pallas_call(
        functools.partial(_combine_kernel, tm=tm),
        grid_spec=grid_spec,
        out_shape=jax.ShapeDtypeStruct((S, D), F32),
        compiler_params=_cparams(1, 32),
        name="moe_combine",
    )(pos, y_sorted, x, mod, top_w)


def _work_items(starts, ends, counts, n_rows, tm):
    n_experts = counts.shape[0]
    n_items = n_rows // tm + n_experts - 1
    first_tile = starts // tm
    n_e = jnp.where(counts > 0, (ends + tm - 1) // tm - first_tile, 0)
    item_end = jnp.cumsum(n_e)
    item_start = item_end - n_e
    total = item_end[-1]
    idx = jnp.arange(n_items, dtype=jnp.int32)
    idx_c = jnp.minimum(idx, total - 1)
    item_e = jnp.minimum(jnp.searchsorted(item_end, idx_c, side='right'), n_experts - 1).astype(jnp.int32)
    item_tile = (first_tile[item_e] + idx_c - item_start[item_e]).astype(jnp.int32)
    return (item_tile, item_e, (idx < total).astype(jnp.int32),
            starts[item_e].astype(jnp.int32), ends[item_e].astype(jnp.int32))


def _route(logits, n_experts):
    S = logits.shape[0]
    top_logit, top_idx = lax.top_k(logits, MOE_TOP_K)
    top_w = jax.nn.softmax(top_logit, axis=-1)
    n_assign = S * MOE_TOP_K
    flat_e = top_idx.reshape(n_assign).astype(jnp.int32)
    onehot = (flat_e[:, None] == jnp.arange(n_experts, dtype=jnp.int32)[None, :]).astype(jnp.int32)
    csum = jnp.cumsum(onehot, axis=0)
    rank = jnp.sum((csum - 1) * onehot, axis=1)
    counts = csum[-1]
    ends = jnp.cumsum(counts)
    starts = ends - counts
    pos = (starts[flat_e] + rank).astype(jnp.int32)
    keys = jnp.sort(flat_e * n_assign + jnp.arange(n_assign, dtype=jnp.int32))
    tok_sorted = (keys % n_assign) // MOE_TOP_K
    return tok_sorted, pos, top_w, (starts, ends, counts)


def kernel(x, c, ada_w, ada_b, mix_norm, ffn_norm, conv_pw1_w, conv_pw1_b, conv_dw_w, conv_dw_b, conv_ln_g, conv_ln_b, conv_pw2_w, conv_pw2_b, fox_qkv_w, fox_o_w, fox_fgate_w, fox_fgate_b, fox_q_norm, fox_k_norm, ffn_gate_w, ffn_up_w, ffn_down_w, moe_router_w, moe_router_b, moe_gate_w, moe_up_w, moe_down_w):
    B, S, D = x.shape
    assert B == 1, "kernels are written for a single sequence"
    L = ada_w.shape[0]
    H = fox_fgate_w.shape[-1]
    E = moe_router_w.shape[-1]

    def as_rows(v):
        return v.reshape(v.shape[0], 1, v.shape[1])

    mod = _adaln(c.reshape(D, 1), ada_w, ada_b)
    mix_g, ffn_g = as_rows(mix_norm), as_rows(ffn_norm)
    zero_bias = jnp.zeros((1, 1, D), F32)
    xs = x.reshape(S, D)

    for i in range(L):
        j = i // 2
        h = _norm_mod(xs, mix_g, mod, i, 0)
        if i % 2 == 0:
            u = _pw1_glu(h, conv_pw1_w, as_rows(conv_pw1_b), j)
            v = _conv_ln(u, conv_dw_w, as_rows(conv_dw_b), as_rows(conv_ln_g), as_rows(conv_ln_b), j)
            xs = _mm_res(v, conv_pw2_w, as_rows(conv_pw2_b), xs, mod, j, i, 2, DENSE_TM, 512, "conv_pw2")
        else:
            q_gain = fox_q_norm[j] * (HEAD_DIM ** -0.5 * LOG2E)
            gain = jnp.concatenate([jnp.tile(q_gain, H), jnp.tile(fox_k_norm[j], H),
                                    jnp.ones((D,), F32)])[None, :]
            qkv = _qkv(h, fox_qkv_w, gain, j)
            cum = _fgate_cum(h, jnp.swapaxes(fox_fgate_w, 1, 2), fox_fgate_b, j)
            bound = 1.02 * HEAD_DIM * jnp.max(jnp.abs(q_gain)) * jnp.max(jnp.abs(fox_k_norm[j]))
            o = _attention(qkv, (cum * LOG2E).reshape(H, 1, S), bound)
            xs = _mm_res(o, fox_o_w, zero_bias, xs, mod, j, i, 2, DENSE_TM, 512, "fox_o")
        if i % 2 == 0:
            h = _norm_mod(xs, ffn_g, mod, i, 3)
            hid = _gate_up(h, ffn_gate_w, ffn_up_w, j)
            xs = _mm_res(hid, ffn_down_w, zero_bias, xs, mod, j, i, 5, 1024, 256, "ffn_down")
        else:
            rw_pad = jnp.zeros((D, LANES), F32).at[:, :E].set(moe_router_w[j])
            rb_pad = jnp.zeros((1, LANES), F32).at[0, :E].set(moe_router_b[j])
            logits = _router(xs, ffn_g, mod, i, rw_pad, rb_pad)[:, :E]
            tok_sorted, pos, top_w, ranges = _route(logits, E)
            n_rows = tok_sorted.shape[0]
            xg = _gather_norm(tok_sorted, xs, ffn_g, mod, i)
            hid = _moe_gate_up(_work_items(*ranges, n_rows, MOE_GU_TM), xg, moe_gate_w, moe_up_w, j)
            y_sorted = _moe_down(_work_items(*ranges, n_rows, MOE_DOWN_TM), hid, moe_down_w, j)
            xs = _combine(pos, y_sorted, xs, mod, i, top_w)
    return xs.reshape(B, S, D)
```

```python
import functools

import jax
import jax.numpy as jnp
from jax import lax
from jax.experimental import pallas as pl
from jax.experimental.pallas import tpu as pltpu

F32 = jnp.float32
BF16 = jnp.bfloat16

NORM_EPS = 1e-6
HEAD_DIM = 128
CONV_WIDTH = 31
CONV_HALO = 32
MOE_TOP_K = 2
LOG2E = 1.4426950408889634
MASK_VALUE = -1e30
ATTN_FAST_MAX_BOUND = 40.0
LANES = 128
SUBLANES = 8
BIG_VMEM_MB = 57
DENSE_TM = 2048
MOE_SUB = 256
MOE_GU_TM = 512
MOE_DOWN_TM = 512


def _cparams(n_axes, vmem_mb):
    return pltpu.CompilerParams(
        dimension_semantics=("arbitrary",) * n_axes,
        vmem_limit_bytes=vmem_mb << 20)


def _norm_modulate(x, g, sc, sh):
    ms = jnp.mean(x * x, axis=-1, keepdims=True)
    y = x * lax.rsqrt(ms + NORM_EPS) * g
    return y * (1.0 + sc) + sh


def _silu(x):
    return x * jax.nn.sigmoid(x)


def _adaln_kernel(c_ref, w_ref, b_ref, o_ref):
    c = c_ref[...]
    o_ref[0] = jnp.sum(w_ref[0] * _silu(c), axis=0, keepdims=True) + b_ref[0]


def _adaln(c_col, ada_w, ada_b):
    L, D, N = ada_w.shape
    tn = 1024
    return pl.pallas_call(
        _adaln_kernel,
        grid=(L, N // tn),
        in_specs=[pl.BlockSpec((D, 1), lambda l, j: (0, 0)),
                  pl.BlockSpec((1, D, tn), lambda l, j: (l, 0, j)),
                  pl.BlockSpec((1, 1, tn), lambda l, j: (l, 0, j))],
        out_specs=pl.BlockSpec((1, 1, tn), lambda l, j: (l, 0, j)),
        out_shape=jax.ShapeDtypeStruct((L, 1, N), F32),
        compiler_params=_cparams(2, 40),
        name="adaln",
    )(c_col, ada_w, ada_b.reshape(L, 1, N))


def _mod_spec(layer, which, D):
    return pl.BlockSpec((1, 1, D), lambda *_: (layer, 0, which))


def _vec_spec(layer, D):
    return pl.BlockSpec((1, 1, D), lambda *_: (layer, 0, 0))


def _norm_mod_kernel(x_ref, g_ref, sc_ref, sh_ref, o_ref):
    o_ref[...] = _norm_modulate(x_ref[...], g_ref[0], sc_ref[0], sh_ref[0]).astype(o_ref.dtype)


def _norm_mod(x, norm_g, mod, layer, which_shift):
    S, D = x.shape
    tm = 512
    return pl.pallas_call(
        _norm_mod_kernel,
        grid=(S // tm,),
        in_specs=[pl.BlockSpec((tm, D), lambda i: (i, 0)),
                  _vec_spec(layer, D),
                  _mod_spec(layer, which_shift + 1, D),
                  _mod_spec(layer, which_shift, D)],
        out_specs=pl.BlockSpec((tm, D), lambda i: (i, 0)),
        out_shape=jax.ShapeDtypeStruct((S, D), BF16),
        compiler_params=_cparams(1, 32),
        name="norm_mod",
    )(x, norm_g, mod, mod)


def _dot(a, w):
    return jnp.dot(a, w, preferred_element_type=F32)


def _pw1_glu_kernel(a_ref, wv_ref, wg_ref, bv_ref, bg_ref, o_ref):
    a = a_ref[...]
    val = _dot(a, wv_ref[...].astype(BF16)) + bv_ref[0]
    gate = _dot(a, wg_ref[...].astype(BF16)) + bg_ref[0]
    o_ref[...] = val * jax.nn.sigmoid(gate)


def _pw1_glu(h, pw1_w, pw1_b, j):
    S, D = h.shape
    tm, tn = DENSE_TM, 512
    nt = D // tn
    return pl.pallas_call(
        _pw1_glu_kernel,
        grid=(S // tm, nt),
        in_specs=[pl.BlockSpec((tm, D), lambda i, n: (i, 0)),
                  pl.BlockSpec((None, D, tn), lambda i, n: (j, 0, n)),
                  pl.BlockSpec((None, D, tn), lambda i, n: (j, 0, n + nt)),
                  pl.BlockSpec((1, 1, tn), lambda i, n: (j, 0, n)),
                  pl.BlockSpec((1, 1, tn), lambda i, n: (j, 0, n + nt))],
        out_specs=pl.BlockSpec((tm, tn), lambda i, n: (i, n)),
        out_shape=jax.ShapeDtypeStruct((S, D), F32),
        compiler_params=_cparams(2, BIG_VMEM_MB),
        name="pw1_glu",
    )(h, pw1_w, pw1_w, pw1_b, pw1_b)


def _gate_up_kernel(a_ref, wg_ref, wu_ref, o_ref):
    a = a_ref[...]
    g = _dot(a, wg_ref[...].astype(BF16))
    u = _dot(a, wu_ref[...].astype(BF16))
    o_ref[...] = (_silu(g) * u).astype(o_ref.dtype)


def _gate_up(h, gate_w, up_w, j):
    S, D = h.shape
    F = gate_w.shape[-1]
    tm, tn = DENSE_TM, 512
    return pl.pallas_call(
        _gate_up_kernel,
        grid=(S // tm, F // tn),
        in_specs=[pl.BlockSpec((tm, D), lambda i, n: (i, 0)),
                  pl.BlockSpec((None, D, tn), lambda i, n: (j, 0, n)),
                  pl.BlockSpec((None, D, tn), lambda i, n: (j, 0, n))],
        out_specs=pl.BlockSpec((tm, tn), lambda i, n: (i, n)),
        out_shape=jax.ShapeDtypeStruct((S, F), BF16),
        compiler_params=_cparams(2, BIG_VMEM_MB),
        name="ffn_gate_up",
    )(h, gate_w, up_w)


def _mm_res_kernel(a_ref, w_ref, b_ref, x_ref, g_ref, o_ref):
    y = _dot(a_ref[...], w_ref[...].astype(BF16)) + b_ref[0]
    o_ref[...] = x_ref[...] + g_ref[0] * y


def _mm_res(a, w, bias, x, mod, j, layer, which_gate, tm, tn, name):
    S, K = a.shape
    D = w.shape[-1]
    bj = j if bias.shape[0] > 1 else 0
    return pl.pallas_call(
        _mm_res_kernel,
        grid=(S // tm, D // tn),
        in_specs=[pl.BlockSpec((tm, K), lambda i, n: (i, 0)),
                  pl.BlockSpec((None, K, tn), lambda i, n: (j, 0, n)),
                  pl.BlockSpec((1, 1, tn), lambda i, n: (bj, 0, n)),
                  pl.BlockSpec((tm, tn), lambda i, n: (i, n)),
                  pl.BlockSpec((1, 1, tn), lambda i, n: (layer, 0, which_gate * (D // tn) + n))],
        out_specs=pl.BlockSpec((tm, tn), lambda i, n: (i, n)),
        out_shape=jax.ShapeDtypeStruct((S, D), F32),
        compiler_params=_cparams(2, BIG_VMEM_MB),
        name=name,
    )(a, w, bias, x, mod)


def _conv_ln_kernel(halo_ref, u_ref, w_ref, b_ref, g_ref, beta_ref, o_ref, win_ref, acc_ref, *, tm):
    i = pl.program_id(0)

    @pl.when(i == 0)
    def _():
        win_ref[0:CONV_HALO, :] = jnp.zeros((CONV_HALO, win_ref.shape[1]), F32)

    @pl.when(i > 0)
    def _():
        win_ref[0:CONV_HALO, :] = halo_ref[...]

    win_ref[CONV_HALO:CONV_HALO + tm, :] = u_ref[...]
    first = CONV_HALO - (CONV_WIDTH - 1)
    for c in range(u_ref.shape[1] // LANES):
        cs = slice(c * LANES, (c + 1) * LANES)
        out = None
        for b in range(SUBLANES):
            rows = tm if b == 0 else tm + SUBLANES
            y = None
            for j in range(first, first + CONV_WIDTH):
                if j % SUBLANES != b:
                    continue
                term = win_ref[j - b:j - b + rows, cs] * w_ref[0, j - first:j - first + 1, cs]
                y = term if y is None else y + term
            yb = y if b == 0 else y[b:b + tm]
            out = yb if out is None else out + yb
        acc_ref[:, cs] = out + b_ref[0][:, cs]
    y = acc_ref[...]
    mu = jnp.mean(y, axis=-1, keepdims=True)
    yc = y - mu
    var = jnp.mean(yc * yc, axis=-1, keepdims=True)
    z = yc * lax.rsqrt(var + NORM_EPS) * g_ref[0] + beta_ref[0]
    o_ref[...] = _silu(z).astype(o_ref.dtype)


def _conv_ln(u, dw_w, dw_b, ln_g, ln_b, j):
    S, D = u.shape
    tm = 128
    hb = tm // CONV_HALO
    return pl.pallas_call(
        functools.partial(_conv_ln_kernel, tm=tm),
        grid=(S // tm,),
        in_specs=[pl.BlockSpec((CONV_HALO, D), lambda i: (jnp.maximum(i * hb - 1, 0), 0)),
                  pl.BlockSpec((tm, D), lambda i: (i, 0)),
                  pl.BlockSpec((1, CONV_WIDTH, D), lambda i: (j, 0, 0)),
                  _vec_spec(j, D), _vec_spec(j, D), _vec_spec(j, D)],
        out_specs=pl.BlockSpec((tm, D), lambda i: (i, 0)),
        out_shape=jax.ShapeDtypeStruct((S, D), BF16),
        scratch_shapes=[pltpu.VMEM((CONV_HALO + tm, D), F32), pltpu.VMEM((tm, D), F32)],
        compiler_params=_cparams(1, 32),
        name="conv_ln",
    )(u, u, dw_w, dw_b, ln_g, ln_b)


def _qkv_kernel(a_ref, w_ref, gain_ref, o_ref, *, n_norm_tiles):
    is_norm_tile = pl.program_id(1) < n_norm_tiles
    acc = _dot(a_ref[...], w_ref[...].astype(BF16))
    gain = gain_ref[...]
    for h in range(acc.shape[1] // HEAD_DIM):
        sl = slice(h * HEAD_DIM, (h + 1) * HEAD_DIM)
        t = acc[:, sl]
        ms = jnp.mean(t * t, axis=-1, keepdims=True)
        r = jnp.where(is_norm_tile, lax.rsqrt(ms + NORM_EPS), 1.0)
        o_ref[:, sl] = (t * r * gain[:, sl]).astype(o_ref.dtype)


def _qkv(h, qkv_w, gain, j):
    S, D = h.shape
    N = qkv_w.shape[-1]
    tm, tn = DENSE_TM, 512
    return pl.pallas_call(
        functools.partial(_qkv_kernel, n_norm_tiles=2 * D // tn),
        grid=(S // tm, N // tn),
        in_specs=[pl.BlockSpec((tm, D), lambda i, n: (i, 0)),
                  pl.BlockSpec((None, D, tn), lambda i, n: (j, 0, n)),
                  pl.BlockSpec((1, tn), lambda i, n: (0, n))],
        out_specs=pl.BlockSpec((tm, tn), lambda i, n: (i, n)),
        out_shape=jax.ShapeDtypeStruct((S, N), BF16),
        compiler_params=_cparams(2, BIG_VMEM_MB),
        name="fox_qkv",
    )(h, qkv_w, gain)


def _fgate_kernel(h_ref, fwt_ref, fb_ref, tri_ref, o_ref, carry_ref):
    i = pl.program_id(0)

    @pl.when(i == 0)
    def _():
        carry_ref[...] = jnp.zeros(carry_ref.shape, F32)

    z = lax.dot_general(fwt_ref[...].astype(BF16), h_ref[...], (((1,), (1,)), ((), ())),
                        preferred_element_type=F32) + fb_ref[...][:, 0:1]
    lf = jnp.minimum(z, 0.0) - jnp.log1p(jnp.exp(-jnp.abs(z)))
    hi = lf.astype(BF16)
    r1 = lf - hi.astype(F32)
    mid = r1.astype(BF16)
    lo = (r1 - mid.astype(F32)).astype(BF16)
    tri = tri_ref[...]
    cs = _dot(hi, tri) + _dot(mid, tri) + _dot(lo, tri) + carry_ref[...][:, 0:1]
    o_ref[...] = cs
    carry_ref[...] = jnp.broadcast_to(cs[:, cs.shape[1] - 1:], carry_ref.shape)


def _fgate_cum(h, fgate_wt, fgate_b, j):
    S, D = h.shape
    H = fgate_wt.shape[1]
    tm = 512
    tri = jnp.triu(jnp.ones((tm, tm), F32)).astype(BF16)
    fb = jnp.broadcast_to(fgate_b[j][:, None], (H, LANES))
    return pl.pallas_call(
        _fgate_kernel,
        grid=(S // tm,),
        in_specs=[pl.BlockSpec((tm, D), lambda i: (i, 0)),
                  pl.BlockSpec((None, H, D), lambda i: (j, 0, 0)),
                  pl.BlockSpec((H, LANES), lambda i: (0, 0)),
                  pl.BlockSpec((tm, tm), lambda i: (0, 0))],
        out_specs=pl.BlockSpec((H, tm), lambda i: (0, i)),
        out_shape=jax.ShapeDtypeStruct((H, S), F32),
        scratch_shapes=[pltpu.VMEM((H, LANES), F32)],
        compiler_params=_cparams(1, 32),
        name="fox_fgate",
    )(h, fgate_wt, fb, tri)


def _attn_kernel(flag_ref, q_ref, k_ref, v_ref, ck_ref, b_ref, o_ref,
                 acc_ref, m_ref, vaug_ref, rt_ref, *, tq, heads):
    qi = pl.program_id(1)
    S = k_ref.shape[0]
    hs = [slice(g * HEAD_DIM, (g + 1) * HEAD_DIM) for g in range(heads)]
    acc_ref[...] = jnp.zeros(acc_ref.shape, F32)

    @pl.when(qi == 0)
    def _():
        for g in range(heads):
            vaug_ref[g, :, 0:HEAD_DIM] = v_ref[:, hs[g]]
            vaug_ref[g, :, HEAD_DIM:] = jnp.ones((S, HEAD_DIM), BF16)

    def scores(g, ks, width):
        return lax.dot_general(q_ref[:, hs[g]], k_ref[pl.ds(ks, width), hs[g]],
                               (((1,), (1,)), ((), ())), preferred_element_type=F32)

    def visible(c):
        row = lax.broadcasted_iota(jnp.int32, (tq, LANES), 0)
        col = lax.broadcasted_iota(jnp.int32, (tq, LANES), 1) + c * LANES
        return col <= row

    def fast_step(ki, width, masked):
        ks = pl.multiple_of(ki * tq, tq)
        for g in range(heads):
            s = scores(g, ks, width)
            ckr = ck_ref[g, :, pl.ds(ks, width)]
            rt = rt_ref[g]
            ps = []
            for c in range(width // LANES):
                sl = slice(c * LANES, (c + 1) * LANES)
                t = s[:, sl] + (rt - ckr[:, sl])
                if masked:
                    t = jnp.where(visible(c), t, MASK_VALUE)
                ps.append(jnp.exp2(t).astype(BF16))
            p = jnp.concatenate(ps, axis=1)
            acc_ref[g] += _dot(p, vaug_ref[g, pl.ds(ks, width), :])

    def slow_step(ki, width, masked):
        ks = pl.multiple_of(ki * tq, tq)
        for g in range(heads):
            t = scores(g, ks, width) - ck_ref[g, :, pl.ds(ks, width)]
            if masked:
                t = jnp.concatenate(
                    [jnp.where(visible(c), t[:, c * LANES:(c + 1) * LANES], MASK_VALUE)
                     for c in range(width // LANES)], axis=1)
            m_prev = m_ref[g]
            m_new = jnp.maximum(m_prev, jnp.max(t, axis=-1, keepdims=True))
            alpha = jnp.exp2(m_prev - m_new)
            p = jnp.exp2(t - m_new[:, 0:1])
            pv = _dot(p.astype(BF16), vaug_ref[g, pl.ds(ks, width), :])
            acc_ref[g] = acc_ref[g] * jnp.concatenate([alpha, alpha], axis=1) + pv
            m_ref[g] = m_new

    def sweep(step):
        def body(kk, carry):
            step(2 * kk, 2 * tq, False)
            return carry
        lax.fori_loop(0, qi // 2, body, 0)

        @pl.when(qi % 2 == 1)
        def _():
            step(qi - 1, tq, False)

        step(qi, tq, True)

    @pl.when(flag_ref[0] == 1)
    def _():
        qs = pl.multiple_of(qi * tq, tq)
        for g in range(heads):
            ckq = jnp.broadcast_to(ck_ref[g, :, pl.ds(qs, tq)], (LANES, tq))
            rt_ref[g] = ckq.T - b_ref[...]
        sweep(fast_step)

    @pl.when(flag_ref[0] != 1)
    def _():
        m_ref[...] = jnp.full(m_ref.shape, MASK_VALUE, F32)
        sweep(slow_step)

    for g in range(heads):
        acc = acc_ref[g]
        o_ref[:, hs[g]] = (acc[:, 0:HEAD_DIM] / acc[:, HEAD_DIM:]).astype(o_ref.dtype)


def _attention(qkv, ck2, bound):
    S = qkv.shape[0]
    H = ck2.shape[0]
    tq, heads = 512, 2
    hw = heads * HEAD_DIM
    flag = (bound <= ATTN_FAST_MAX_BOUND).astype(jnp.int32).reshape(1)
    bvec = jnp.broadcast_to(bound.astype(F32), (1, LANES))
    grid_spec = pltpu.PrefetchScalarGridSpec(
        num_scalar_prefetch=1,
        grid=(H // heads, S // tq),
        in_specs=[pl.BlockSpec((tq, hw), lambda h, i, f: (i, h)),
                  pl.BlockSpec((S, hw), lambda h, i, f: (0, H // heads + h)),
                  pl.BlockSpec((S, hw), lambda h, i, f: (0, 2 * (H // heads) + h)),
                  pl.BlockSpec((heads, 1, S), lambda h, i, f: (h, 0, 0)),
                  pl.BlockSpec((1, LANES), lambda h, i, f: (0, 0))],
        out_specs=pl.BlockSpec((tq, hw), lambda h, i, f: (i, h)),
        scratch_shapes=[pltpu.VMEM((heads, tq, 2 * HEAD_DIM), F32),
                        pltpu.VMEM((heads, tq, HEAD_DIM), F32),
                        pltpu.VMEM((heads, S, 2 * HEAD_DIM), BF16),
                        pltpu.VMEM((heads, tq, LANES), F32)])
    return pl.pallas_call(
        functools.partial(_attn_kernel, tq=tq, heads=heads),
        grid_spec=grid_spec,
        out_shape=jax.ShapeDtypeStruct((S, H * HEAD_DIM), BF16),
        compiler_params=_cparams(2, BIG_VMEM_MB),
        name="fox_attn",
    )(flag, qkv, qkv, qkv, ck2, bvec)


def _router_kernel(x_ref, g_ref, sc_ref, sh_ref, rw_ref, rb_ref, o_ref):
    h = _norm_modulate(x_ref[...], g_ref[0], sc_ref[0], sh_ref[0])
    o_ref[...] = jnp.dot(h, rw_ref[...], preferred_element_type=F32,
                         precision=lax.Precision.HIGHEST) + rb_ref[...]


def _router(x, norm_g, mod, layer, rw_pad, rb_pad):
    S, D = x.shape
    tm = 512
    return pl.pallas_call(
        _router_kernel,
        grid=(S // tm,),
        in_specs=[pl.BlockSpec((tm, D), lambda i: (i, 0)),
                  _vec_spec(layer, D), _mod_spec(layer, 4, D), _mod_spec(layer, 3, D),
                  pl.BlockSpec((D, LANES), lambda i: (0, 0)),
                  pl.BlockSpec((1, LANES), lambda i: (0, 0))],
        out_specs=pl.BlockSpec((tm, LANES), lambda i: (i, 0)),
        out_shape=jax.ShapeDtypeStruct((S, LANES), F32),
        compiler_params=_cparams(1, 32),
        name="moe_router",
    )(x, norm_g, mod, mod, rw_pad, rb_pad)


def _row_copy(src_hbm, dst, src_row, dst_row, sem):
    return pltpu.make_async_copy(src_hbm.at[pl.ds(src_row, 1), :], dst.at[pl.ds(dst_row, 1), :], sem)


def _gathered_rows(idx_ref, src_hbm, buf, sem, tm, n_per_row):
    i = pl.program_id(0)

    def start_copies(step, slot):
        def body(r, c):
            for k in range(n_per_row):
                src_row = idx_ref[(step * tm + r) * n_per_row + k]
                _row_copy(src_hbm, buf.at[slot, k], src_row, r, sem.at[slot]).start()
            return c
        lax.fori_loop(0, tm, body, 0, unroll=8)

    @pl.when(i == 0)
    def _():
        start_copies(0, 0)

    @pl.when(i + 1 < pl.num_programs(0))
    def _():
        start_copies(i + 1, (i + 1) % 2)

    slot = i % 2
    for k in range(n_per_row):
        pltpu.make_async_copy(src_hbm.at[pl.ds(0, tm), :], buf.at[slot, k], sem.at[slot]).wait()
    return slot


def _gather_norm_kernel(tok_ref, x_hbm, g_ref, sc_ref, sh_ref, o_ref, buf, sem, *, tm):
    slot = _gathered_rows(tok_ref, x_hbm, buf, sem, tm, 1)
    o_ref[...] = _norm_modulate(buf[slot, 0], g_ref[0], sc_ref[0], sh_ref[0]).astype(o_ref.dtype)


def _gather_norm(tok_sorted, x, norm_g, mod, layer):
    S, D = x.shape
    n_rows = tok_sorted.shape[0]
    tm = 256
    grid_spec = pltpu.PrefetchScalarGridSpec(
        num_scalar_prefetch=1,
        grid=(n_rows // tm,),
        in_specs=[pl.BlockSpec(memory_space=pl.ANY),
                  pl.BlockSpec((1, 1, D), lambda i, t: (layer, 0, 0)),
                  pl.BlockSpec((1, 1, D), lambda i, t: (layer, 0, 4)),
                  pl.BlockSpec((1, 1, D), lambda i, t: (layer, 0, 3))],
        out_specs=pl.BlockSpec((tm, D), lambda i, t: (i, 0)),
        scratch_shapes=[pltpu.VMEM((2, 1, tm, D), F32), pltpu.SemaphoreType.DMA((2,))])
    return pl.pallas_call(
        functools.partial(_gather_norm_kernel, tm=tm),
        grid_spec=grid_spec,
        out_shape=jax.ShapeDtypeStruct((n_rows, D), BF16),
        compiler_params=_cparams(1, 32),
        name="moe_gather",
    )(tok_sorted, x, norm_g, mod, mod)


def _for_expert_sub_blocks(o_ref, tile, lo, hi, compute):
    tm = o_ref.shape[0]
    whole = (lo <= tile * tm) & (hi >= (tile + 1) * tm)

    @pl.when(whole)
    def _():
        o_ref[...] = compute(slice(0, tm)).astype(o_ref.dtype)

    for sb in range(tm // MOE_SUB):
        rs = slice(sb * MOE_SUB, (sb + 1) * MOE_SUB)
        start = tile * tm + sb * MOE_SUB

        @pl.when(jnp.logical_not(whole) & (hi > start) & (lo < start + MOE_SUB))
        def _(rs=rs, start=start):
            val = compute(rs).astype(o_ref.dtype)

            @pl.when(lo <= start)
            def _():
                o_ref[rs, :] = val

            @pl.when(lo > start)
            def _():
                rows = start + lax.broadcasted_iota(jnp.int32, (MOE_SUB, 1), 0)
                keep = (rows >= lo) & (rows < hi)
                o_ref[rs, :] = jnp.where(keep, val, o_ref[rs, :])


def _stream_expert_weights(exp_ref, nxt_ref, lastrun_ref, w_hbms, stages, caches, sems, layer, tn):
    n, i = pl.program_id(0), pl.program_id(1)

    def copy(k, e, col_tile):
        cols = pl.ds(pl.multiple_of(col_tile * tn, tn), tn)
        return pltpu.make_async_copy(w_hbms[k].at[layer, e, :, cols], stages[k], sems.at[k])

    @pl.when((n == 0) & (i == 0))
    def _():
        for k in range(len(w_hbms)):
            copy(k, exp_ref[0], 0).start()

    @pl.when((i == 0) | (exp_ref[i] != exp_ref[jnp.maximum(i - 1, 0)]))
    def _():
        for k in range(len(w_hbms)):
            copy(k, 0, 0).wait()
            caches[k][...] = stages[k][...].astype(BF16)
        nn = jnp.where(lastrun_ref[i] == 1, n + 1, n)

        @pl.when(nn < pl.num_programs(0))
        def _():
            for k in range(len(w_hbms)):
                copy(k, nxt_ref[i], nn).start()


def _moe_gu_kernel(tile_ref, exp_ref, valid_ref, lo_ref, hi_ref, nxt_ref, lastrun_ref,
                   a_ref, wg_hbm, wu_hbm, o_ref, wg_stage, wu_stage, wgb, wub, sems, *, layer):
    i = pl.program_id(1)
    _stream_expert_weights(exp_ref, nxt_ref, lastrun_ref, (wg_hbm, wu_hbm), (wg_stage, wu_stage), (wgb, wub),
                           sems, layer, o_ref.shape[1])

    def compute(rs):
        a = a_ref[rs, :]
        return _silu(_dot(a, wgb[...])) * _dot(a, wub[...])

    @pl.when(valid_ref[i] == 1)
    def _():
        _for_expert_sub_blocks(o_ref, tile_ref[i], lo_ref[i], hi_ref[i], compute)


def _moe_gate_up(items, xs, gate_w, up_w, j):
    n_rows, D = xs.shape
    F = gate_w.shape[-1]
    tm, tn = MOE_GU_TM, 1024
    n_items = items[0].shape[0]
    grid_spec = pltpu.PrefetchScalarGridSpec(
        num_scalar_prefetch=len(items),
        grid=(F // tn, n_items),
        in_specs=[pl.BlockSpec((tm, D), lambda n, i, tile, *_: (tile[i], 0)),
                  pl.BlockSpec(memory_space=pl.ANY), pl.BlockSpec(memory_space=pl.ANY)],
        out_specs=pl.BlockSpec((tm, tn), lambda n, i, tile, *_: (tile[i], n)),
        scratch_shapes=[pltpu.VMEM((D, tn), F32), pltpu.VMEM((D, tn), F32),
                        pltpu.VMEM((D, tn), BF16), pltpu.VMEM((D, tn), BF16),
                        pltpu.SemaphoreType.DMA((2,))])
    return pl.pallas_call(
        functools.partial(_moe_gu_kernel, layer=j),
        grid_spec=grid_spec,
        out_shape=jax.ShapeDtypeStruct((n_rows, F), BF16),
        compiler_params=_cparams(2, BIG_VMEM_MB),
        name="moe_gate_up",
    )(*items, xs, gate_w, up_w)


def _moe_down_kernel(tile_ref, exp_ref, valid_ref, lo_ref, hi_ref, nxt_ref, lastrun_ref,
                     a_ref, w_hbm, o_ref, w_stage, wb, sems, *, layer):
    i = pl.program_id(1)
    _stream_expert_weights(exp_ref, nxt_ref, lastrun_ref, (w_hbm,), (w_stage,), (wb,), sems, layer,
                           o_ref.shape[1])

    @pl.when(valid_ref[i] == 1)
    def _():
        _for_expert_sub_blocks(o_ref, tile_ref[i], lo_ref[i], hi_ref[i],
                               lambda rs: _dot(a_ref[rs, :], wb[...]))


def _moe_down(items, hid, down_w, j):
    n_rows, F = hid.shape
    D = down_w.shape[-1]
    tm, tn = MOE_DOWN_TM, 512
    n_items = items[0].shape[0]
    grid_spec = pltpu.PrefetchScalarGridSpec(
        num_scalar_prefetch=len(items),
        grid=(D // tn, n_items),
        in_specs=[pl.BlockSpec((tm, F), lambda n, i, tile, *_: (tile[i], 0)),
                  pl.BlockSpec(memory_space=pl.ANY)],
        out_specs=pl.BlockSpec((tm, tn), lambda n, i, tile, *_: (tile[i], n)),
        scratch_shapes=[pltpu.VMEM((F, tn), F32), pltpu.VMEM((F, tn), BF16), pltpu.SemaphoreType.DMA((1,))])
    return pl.pallas_call(
        functools.partial(_moe_down_kernel, layer=j),
        grid_spec=grid_spec,
        out_shape=jax.ShapeDtypeStruct((n_rows, D), F32),
        compiler_params=_cparams(2, BIG_VMEM_MB),
        name="moe_down",
    )(*items, hid, down_w)


def _combined_rows(pos_ref, y_hbm, x_ref, g_ref, w_ref, buf, sem, tm):
    slot = _gathered_rows(pos_ref, y_hbm, buf, sem, tm, MOE_TOP_K)
    w = w_ref[...]
    y = buf[slot, 0] * w[:, 0:1] + buf[slot, 1] * w[:, 1:2]
    return x_ref[...] + g_ref[0] * y


def _combine_kernel(pos_ref, y_hbm, x_ref, g_ref, w_ref, o_ref, buf, sem, *, tm):
    o_ref[...] = _combined_rows(pos_ref, y_hbm, x_ref, g_ref, w_ref, buf, sem, tm)


def _combine_norm_kernel(pos_ref, y_hbm, x_ref, g_ref, w_ref, ng_ref, nsc_ref, nsh_ref, o_ref, h_ref, buf, sem,
                         *, tm):
    x_new = _combined_rows(pos_ref, y_hbm, x_ref, g_ref, w_ref, buf, sem, tm)
    o_ref[...] = x_new
    h_ref[...] = _norm_modulate(x_new, ng_ref[0], nsc_ref[0], nsh_ref[0]).astype(h_ref.dtype)


def _combine(pos, y_sorted, x, mod, layer, top_w, next_norm_g=None):
    S, D = x.shape
    tm = 256
    row_spec = pl.BlockSpec((tm, D), lambda i, p: (i, 0))
    in_specs = [pl.BlockSpec(memory_space=pl.ANY),
                row_spec,
                pl.BlockSpec((1, 1, D), lambda i, p: (layer, 0, 5)),
                pl.BlockSpec((tm, MOE_TOP_K), lambda i, p: (i, 0))]
    args = (pos, y_sorted, x, mod, top_w)
    out_specs, out_shape, body = row_spec, jax.ShapeDtypeStruct((S, D), F32), _combine_kernel
    if next_norm_g is not None:
        in_specs += [pl.BlockSpec((1, 1, D), lambda i, p: (layer + 1, 0, 0)),
                     pl.BlockSpec((1, 1, D), lambda i, p: (layer + 1, 0, 1)),
                     pl.BlockSpec((1, 1, D), lambda i, p: (layer + 1, 0, 0))]
        args += (next_norm_g, mod, mod)
        out_specs = [row_spec, row_spec]
        out_shape = [out_shape, jax.ShapeDtypeStruct((S, D), BF16)]
        body = _combine_norm_kernel
    grid_spec = pltpu.PrefetchScalarGridSpec(
        num_scalar_prefetch=1,
        grid=(S // tm,),
        in_specs=in_specs,
        out_specs=out_specs,
        scratch_shapes=[pltpu.VMEM((2, MOE_TOP_K, tm, D), F32), pltpu.SemaphoreType.DMA((2,))])
    return pl.pallas_call(
        functools.partial(body, tm=tm),
        grid_spec=grid_spec,
        out_shape=out_shape,
        compiler_params=_cparams(1, 40),
        name="moe_combine",
    )(*args)


def _work_items(starts, ends, counts, n_rows, tm):
    n_experts = counts.shape[0]
    n_items = n_rows // tm + n_experts - 1
    first_tile = starts // tm
    n_e = jnp.where(counts > 0, (ends + tm - 1) // tm - first_tile, 0)
    item_end = jnp.cumsum(n_e)
    item_start = item_end - n_e
    total = item_end[-1]
    idx = jnp.arange(n_items, dtype=jnp.int32)
    idx_c = jnp.minimum(idx, total - 1)
    item_e = jnp.minimum(jnp.searchsorted(item_end, idx_c, side='right'), n_experts - 1).astype(jnp.int32)
    item_tile = (first_tile[item_e] + idx_c - item_start[item_e]).astype(jnp.int32)
    run_start = jnp.concatenate([jnp.ones((1,), bool), item_e[1:] != item_e[:-1]])
    next_start = lax.cummin(jnp.where(run_start, idx, n_items), reverse=True)
    next_start = jnp.concatenate([next_start[1:], jnp.full((1,), n_items, jnp.int32)])
    last_run = next_start >= n_items
    next_e = jnp.where(last_run, item_e[0], item_e[jnp.minimum(next_start, n_items - 1)])
    return (item_tile, item_e, (idx < total).astype(jnp.int32),
            starts[item_e].astype(jnp.int32), ends[item_e].astype(jnp.int32),
            next_e.astype(jnp.int32), last_run.astype(jnp.int32))


def _route(logits, n_experts):
    S = logits.shape[0]
    top_logit, top_idx = lax.top_k(logits, MOE_TOP_K)
    top_w = jax.nn.softmax(top_logit, axis=-1)
    n_assign = S * MOE_TOP_K
    flat_e = top_idx.reshape(n_assign).astype(jnp.int32)
    onehot = (flat_e[:, None] == jnp.arange(n_experts, dtype=jnp.int32)[None, :]).astype(jnp.int32)
    csum = jnp.cumsum(onehot, axis=0)
    rank = jnp.sum((csum - 1) * onehot, axis=1)
    counts = csum[-1]
    ends = jnp.cumsum(counts)
    starts = ends - counts
    pos = (starts[flat_e] + rank).astype(jnp.int32)
    keys = jnp.sort(flat_e * n_assign + jnp.arange(n_assign, dtype=jnp.int32))
    tok_sorted = (keys % n_assign) // MOE_TOP_K
    return tok_sorted, pos, top_w, (starts, ends, counts)


def kernel(x, c, ada_w, ada_b, mix_norm, ffn_norm, conv_pw1_w, conv_pw1_b, conv_dw_w, conv_dw_b, conv_ln_g, conv_ln_b, conv_pw2_w, conv_pw2_b, fox_qkv_w, fox_o_w, fox_fgate_w, fox_fgate_b, fox_q_norm, fox_k_norm, ffn_gate_w, ffn_up_w, ffn_down_w, moe_router_w, moe_router_b, moe_gate_w, moe_up_w, moe_down_w):
    B, S, D = x.shape
    assert B == 1, "kernels are written for a single sequence"
    L = ada_w.shape[0]
    H = fox_fgate_w.shape[-1]
    E = moe_router_w.shape[-1]

    def as_rows(v):
        return v.reshape(v.shape[0], 1, v.shape[1])

    mod = _adaln(c.reshape(D, 1), ada_w, ada_b)
    mix_g, ffn_g = as_rows(mix_norm), as_rows(ffn_norm)
    zero_bias = jnp.zeros((1, 1, D), F32)
    xs = x.reshape(S, D)

    h_next = None
    for i in range(L):
        j = i // 2
        h = _norm_mod(xs, mix_g, mod, i, 0) if h_next is None else h_next
        h_next = None
        if i % 2 == 0:
            u = _pw1_glu(h, conv_pw1_w, as_rows(conv_pw1_b), j)
            v = _conv_ln(u, conv_dw_w, as_rows(conv_dw_b), as_rows(conv_ln_g), as_rows(conv_ln_b), j)
            xs = _mm_res(v, conv_pw2_w, as_rows(conv_pw2_b), xs, mod, j, i, 2, DENSE_TM, 512, "conv_pw2")
        else:
            q_gain = fox_q_norm[j] * (HEAD_DIM ** -0.5 * LOG2E)
            gain = jnp.concatenate([jnp.tile(q_gain, H), jnp.tile(fox_k_norm[j], H),
                                    jnp.ones((D,), F32)])[None, :]
            qkv = _qkv(h, fox_qkv_w, gain, j)
            cum = _fgate_cum(h, jnp.swapaxes(fox_fgate_w, 1, 2), fox_fgate_b, j)
            bound = 1.02 * HEAD_DIM * jnp.max(jnp.abs(q_gain)) * jnp.max(jnp.abs(fox_k_norm[j]))
            o = _attention(qkv, (cum * LOG2E).reshape(H, 1, S), bound)
            xs = _mm_res(o, fox_o_w, zero_bias, xs, mod, j, i, 2, DENSE_TM, 512, "fox_o")
        if i % 2 == 0:
            h = _norm_mod(xs, ffn_g, mod, i, 3)
            hid = _gate_up(h, ffn_gate_w, ffn_up_w, j)
            xs = _mm_res(hid, ffn_down_w, zero_bias, xs, mod, j, i, 5, 1024, 256, "ffn_down")
        else:
            rw_pad = jnp.zeros((D, LANES), F32).at[:, :E].set(moe_router_w[j])
            rb_pad = jnp.zeros((1, LANES), F32).at[0, :E].set(moe_router_b[j])
            logits = _router(xs, ffn_g, mod, i, rw_pad, rb_pad)[:, :E]
            tok_sorted, pos, top_w, ranges = _route(logits, E)
            n_rows = tok_sorted.shape[0]
            xg = _gather_norm(tok_sorted, xs, ffn_g, mod, i)
            hid = _moe_gate_up(_work_items(*ranges, n_rows, MOE_GU_TM), xg, moe_gate_w, moe_up_w, j)
            y_sorted = _moe_down(_work_items(*ranges, n_rows, MOE_DOWN_TM), hid, moe_down_w, j)
            if i + 1 < L:
                xs, h_next = _combine(pos, y_sorted, xs, mod, i, top_w, mix_g)
            else:
                xs = _combine(pos, y_sorted, xs, mod, i, top_w)
    return xs.reshape(B, S, D)
```

```python
import functools

import jax
import jax.numpy as jnp
from jax import lax
from jax.experimental import pallas as pl
from jax.experimental.pallas import tpu as pltpu

F32 = jnp.float32
BF16 = jnp.bfloat16

NORM_EPS = 1e-6
HEAD_DIM = 128
CONV_WIDTH = 31
CONV_HALO = 32
MOE_TOP_K = 2
LOG2E = 1.4426950408889634
MASK_VALUE = -1e30
ATTN_FAST_MAX_BOUND = 40.0
LANES = 128
SUBLANES = 8
BIG_VMEM_MB = 57
DENSE_TM = 2048
MOE_SUB = 256
MOE_TM = 512


def _cparams(n_axes, vmem_mb):
    return pltpu.CompilerParams(
        dimension_semantics=("arbitrary",) * n_axes,
        vmem_limit_bytes=vmem_mb << 20)


def _norm_modulate(x, g, sc, sh):
    ms = jnp.mean(x * x, axis=-1, keepdims=True)
    y = x * lax.rsqrt(ms + NORM_EPS) * g
    return y * (1.0 + sc) + sh


def _silu(x):
    return x * jax.nn.sigmoid(x)


def _adaln_kernel(c_ref, w_ref, b_ref, o_ref):
    c = c_ref[...]
    o_ref[0] = jnp.sum(w_ref[0] * _silu(c), axis=0, keepdims=True) + b_ref[0]


def _adaln(c_col, ada_w, ada_b):
    L, D, N = ada_w.shape
    tn = 1024
    return pl.pallas_call(
        _adaln_kernel,
        grid=(L, N // tn),
        in_specs=[pl.BlockSpec((D, 1), lambda l, j: (0, 0)),
                  pl.BlockSpec((1, D, tn), lambda l, j: (l, 0, j)),
                  pl.BlockSpec((1, 1, tn), lambda l, j: (l, 0, j))],
        out_specs=pl.BlockSpec((1, 1, tn), lambda l, j: (l, 0, j)),
        out_shape=jax.ShapeDtypeStruct((L, 1, N), F32),
        compiler_params=_cparams(2, 40),
        name="adaln",
    )(c_col, ada_w, ada_b.reshape(L, 1, N))


def _mod_spec(layer, which, D):
    return pl.BlockSpec((1, 1, D), lambda *_: (layer, 0, which))


def _vec_spec(layer, D):
    return pl.BlockSpec((1, 1, D), lambda *_: (layer, 0, 0))


def _norm_mod_kernel(x_ref, g_ref, sc_ref, sh_ref, o_ref):
    o_ref[...] = _norm_modulate(x_ref[...], g_ref[0], sc_ref[0], sh_ref[0]).astype(o_ref.dtype)


def _norm_mod(x, norm_g, mod, layer, which_shift):
    S, D = x.shape
    tm = 512
    return pl.pallas_call(
        _norm_mod_kernel,
        grid=(S // tm,),
        in_specs=[pl.BlockSpec((tm, D), lambda i: (i, 0)),
                  _vec_spec(layer, D),
                  _mod_spec(layer, which_shift + 1, D),
                  _mod_spec(layer, which_shift, D)],
        out_specs=pl.BlockSpec((tm, D), lambda i: (i, 0)),
        out_shape=jax.ShapeDtypeStruct((S, D), BF16),
        compiler_params=_cparams(1, 32),
        name="norm_mod",
    )(x, norm_g, mod, mod)


def _dot(a, w):
    return jnp.dot(a, w, preferred_element_type=F32)


def _pw1_glu_kernel(a_ref, wv_ref, wg_ref, bv_ref, bg_ref, o_ref):
    a = a_ref[...]
    val = _dot(a, wv_ref[...].astype(BF16)) + bv_ref[0]
    gate = _dot(a, wg_ref[...].astype(BF16)) + bg_ref[0]
    o_ref[...] = val * jax.nn.sigmoid(gate)


def _pw1_glu(h, pw1_w, pw1_b, j):
    S, D = h.shape
    tm, tn = DENSE_TM, 512
    nt = D // tn
    return pl.pallas_call(
        _pw1_glu_kernel,
        grid=(S // tm, nt),
        in_specs=[pl.BlockSpec((tm, D), lambda i, n: (i, 0)),
                  pl.BlockSpec((None, D, tn), lambda i, n: (j, 0, n)),
                  pl.BlockSpec((None, D, tn), lambda i, n: (j, 0, n + nt)),
                  pl.BlockSpec((1, 1, tn), lambda i, n: (j, 0, n)),
                  pl.BlockSpec((1, 1, tn), lambda i, n: (j, 0, n + nt))],
        out_specs=pl.BlockSpec((tm, tn), lambda i, n: (i, n)),
        out_shape=jax.ShapeDtypeStruct((S, D), F32),
        compiler_params=_cparams(2, BIG_VMEM_MB),
        name="pw1_glu",
    )(h, pw1_w, pw1_w, pw1_b, pw1_b)


def _gate_up_kernel(a_ref, wg_ref, wu_ref, o_ref):
    a = a_ref[...]
    g = _dot(a, wg_ref[...].astype(BF16))
    u = _dot(a, wu_ref[...].astype(BF16))
    o_ref[...] = (_silu(g) * u).astype(o_ref.dtype)


def _gate_up(h, gate_w, up_w, j):
    S, D = h.shape
    F = gate_w.shape[-1]
    tm, tn = DENSE_TM, 512
    return pl.pallas_call(
        _gate_up_kernel,
        grid=(S // tm, F // tn),
        in_specs=[pl.BlockSpec((tm, D), lambda i, n: (i, 0)),
                  pl.BlockSpec((None, D, tn), lambda i, n: (j, 0, n)),
                  pl.BlockSpec((None, D, tn), lambda i, n: (j, 0, n))],
        out_specs=pl.BlockSpec((tm, tn), lambda i, n: (i, n)),
        out_shape=jax.ShapeDtypeStruct((S, F), BF16),
        compiler_params=_cparams(2, BIG_VMEM_MB),
        name="ffn_gate_up",
    )(h, gate_w, up_w)


def _mm_res_kernel(a_ref, w_ref, b_ref, x_ref, g_ref, o_ref):
    y = _dot(a_ref[...], w_ref[...].astype(BF16)) + b_ref[0]
    o_ref[...] = x_ref[...] + g_ref[0] * y


def _mm_res(a, w, bias, x, mod, j, layer, which_gate, tm, tn, name):
    S, K = a.shape
    D = w.shape[-1]
    bj = j if bias.shape[0] > 1 else 0
    return pl.pallas_call(
        _mm_res_kernel,
        grid=(S // tm, D // tn),
        in_specs=[pl.BlockSpec((tm, K), lambda i, n: (i, 0)),
                  pl.BlockSpec((None, K, tn), lambda i, n: (j, 0, n)),
                  pl.BlockSpec((1, 1, tn), lambda i, n: (bj, 0, n)),
                  pl.BlockSpec((tm, tn), lambda i, n: (i, n)),
                  pl.BlockSpec((1, 1, tn), lambda i, n: (layer, 0, which_gate * (D // tn) + n))],
        out_specs=pl.BlockSpec((tm, tn), lambda i, n: (i, n)),
        out_shape=jax.ShapeDtypeStruct((S, D), F32),
        compiler_params=_cparams(2, BIG_VMEM_MB),
        name=name,
    )(a, w, bias, x, mod)


def _conv_ln_kernel(halo_ref, u_ref, w_ref, b_ref, g_ref, beta_ref, o_ref, win_ref, acc_ref, *, tm):
    i = pl.program_id(0)

    @pl.when(i == 0)
    def _():
        win_ref[0:CONV_HALO, :] = jnp.zeros((CONV_HALO, win_ref.shape[1]), F32)

    @pl.when(i > 0)
    def _():
        win_ref[0:CONV_HALO, :] = halo_ref[...]

    win_ref[CONV_HALO:CONV_HALO + tm, :] = u_ref[...]
    first = CONV_HALO - (CONV_WIDTH - 1)
    for c in range(u_ref.shape[1] // LANES):
        cs = slice(c * LANES, (c + 1) * LANES)
        out = None
        for b in range(SUBLANES):
            rows = tm if b == 0 else tm + SUBLANES
            y = None
            for j in range(first, first + CONV_WIDTH):
                if j % SUBLANES != b:
                    continue
                term = win_ref[j - b:j - b + rows, cs] * w_ref[0, j - first:j - first + 1, cs]
                y = term if y is None else y + term
            yb = y if b == 0 else y[b:b + tm]
            out = yb if out is None else out + yb
        acc_ref[:, cs] = out + b_ref[0][:, cs]
    y = acc_ref[...]
    mu = jnp.mean(y, axis=-1, keepdims=True)
    yc = y - mu
    var = jnp.mean(yc * yc, axis=-1, keepdims=True)
    z = yc * lax.rsqrt(var + NORM_EPS) * g_ref[0] + beta_ref[0]
    o_ref[...] = _silu(z).astype(o_ref.dtype)


def _conv_ln(u, dw_w, dw_b, ln_g, ln_b, j):
    S, D = u.shape
    tm = 128
    hb = tm // CONV_HALO
    return pl.pallas_call(
        functools.partial(_conv_ln_kernel, tm=tm),
        grid=(S // tm,),
        in_specs=[pl.BlockSpec((CONV_HALO, D), lambda i: (jnp.maximum(i * hb - 1, 0), 0)),
                  pl.BlockSpec((tm, D), lambda i: (i, 0)),
                  pl.BlockSpec((1, CONV_WIDTH, D), lambda i: (j, 0, 0)),
                  _vec_spec(j, D), _vec_spec(j, D), _vec_spec(j, D)],
        out_specs=pl.BlockSpec((tm, D), lambda i: (i, 0)),
        out_shape=jax.ShapeDtypeStruct((S, D), BF16),
        scratch_shapes=[pltpu.VMEM((CONV_HALO + tm, D), F32), pltpu.VMEM((tm, D), F32)],
        compiler_params=_cparams(1, 32),
        name="conv_ln",
    )(u, u, dw_w, dw_b, ln_g, ln_b)


def _qkv_kernel(a_ref, w_ref, gain_ref, o_ref, *, n_norm_tiles):
    is_norm_tile = pl.program_id(1) < n_norm_tiles
    acc = _dot(a_ref[...], w_ref[...].astype(BF16))
    gain = gain_ref[...]
    for h in range(acc.shape[1] // HEAD_DIM):
        sl = slice(h * HEAD_DIM, (h + 1) * HEAD_DIM)
        t = acc[:, sl]
        ms = jnp.mean(t * t, axis=-1, keepdims=True)
        r = jnp.where(is_norm_tile, lax.rsqrt(ms + NORM_EPS), 1.0)
        o_ref[:, sl] = (t * r * gain[:, sl]).astype(o_ref.dtype)


def _qkv(h, qkv_w, gain, j):
    S, D = h.shape
    N = qkv_w.shape[-1]
    tm, tn = DENSE_TM, 512
    return pl.pallas_call(
        functools.partial(_qkv_kernel, n_norm_tiles=2 * D // tn),
        grid=(S // tm, N // tn),
        in_specs=[pl.BlockSpec((tm, D), lambda i, n: (i, 0)),
                  pl.BlockSpec((None, D, tn), lambda i, n: (j, 0, n)),
                  pl.BlockSpec((1, tn), lambda i, n: (0, n))],
        out_specs=pl.BlockSpec((tm, tn), lambda i, n: (i, n)),
        out_shape=jax.ShapeDtypeStruct((S, N), BF16),
        compiler_params=_cparams(2, BIG_VMEM_MB),
        name="fox_qkv",
    )(h, qkv_w, gain)


def _fgate_kernel(h_ref, fwt_ref, fb_ref, tri_ref, o_ref, carry_ref):
    i = pl.program_id(0)

    @pl.when(i == 0)
    def _():
        carry_ref[...] = jnp.zeros(carry_ref.shape, F32)

    z = lax.dot_general(fwt_ref[...].astype(BF16), h_ref[...], (((1,), (1,)), ((), ())),
                        preferred_element_type=F32) + fb_ref[...][:, 0:1]
    lf = jnp.minimum(z, 0.0) - jnp.log1p(jnp.exp(-jnp.abs(z)))
    hi = lf.astype(BF16)
    r1 = lf - hi.astype(F32)
    mid = r1.astype(BF16)
    lo = (r1 - mid.astype(F32)).astype(BF16)
    tri = tri_ref[...]
    cs = _dot(hi, tri) + _dot(mid, tri) + _dot(lo, tri) + carry_ref[...][:, 0:1]
    o_ref[...] = cs
    carry_ref[...] = jnp.broadcast_to(cs[:, cs.shape[1] - 1:], carry_ref.shape)


def _fgate_cum(h, fgate_wt, fgate_b, j):
    S, D = h.shape
    H = fgate_wt.shape[1]
    tm = 512
    tri = jnp.triu(jnp.ones((tm, tm), F32)).astype(BF16)
    fb = jnp.broadcast_to(fgate_b[j][:, None], (H, LANES))
    return pl.pallas_call(
        _fgate_kernel,
        grid=(S // tm,),
        in_specs=[pl.BlockSpec((tm, D), lambda i: (i, 0)),
                  pl.BlockSpec((None, H, D), lambda i: (j, 0, 0)),
                  pl.BlockSpec((H, LANES), lambda i: (0, 0)),
                  pl.BlockSpec((tm, tm), lambda i: (0, 0))],
        out_specs=pl.BlockSpec((H, tm), lambda i: (0, i)),
        out_shape=jax.ShapeDtypeStruct((H, S), F32),
        scratch_shapes=[pltpu.VMEM((H, LANES), F32)],
        compiler_params=_cparams(1, 32),
        name="fox_fgate",
    )(h, fgate_wt, fb, tri)


def _attn_kernel(flag_ref, q_ref, k_ref, v_ref, ck_ref, b_ref, o_ref,
                 acc_ref, m_ref, vaug_ref, rt_ref, *, tq, heads):
    qi = pl.program_id(1)
    S = k_ref.shape[0]
    hs = [slice(g * HEAD_DIM, (g + 1) * HEAD_DIM) for g in range(heads)]
    acc_ref[...] = jnp.zeros(acc_ref.shape, F32)

    @pl.when(qi == 0)
    def _():
        for g in range(heads):
            vaug_ref[g, :, 0:HEAD_DIM] = v_ref[:, hs[g]]
            vaug_ref[g, :, HEAD_DIM:] = jnp.ones((S, HEAD_DIM), BF16)

    def scores(g, ks, width):
        return lax.dot_general(q_ref[:, hs[g]], k_ref[pl.ds(ks, width), hs[g]],
                               (((1,), (1,)), ((), ())), preferred_element_type=F32)

    def visible(c):
        row = lax.broadcasted_iota(jnp.int32, (tq, LANES), 0)
        col = lax.broadcasted_iota(jnp.int32, (tq, LANES), 1) + c * LANES
        return col <= row

    def fast_step(ki, width, masked):
        ks = pl.multiple_of(ki * tq, tq)
        for g in range(heads):
            s = scores(g, ks, width)
            ckr = ck_ref[g, :, pl.ds(ks, width)]
            rt = rt_ref[g]
            ps = []
            for c in range(width // LANES):
                sl = slice(c * LANES, (c + 1) * LANES)
                t = s[:, sl] + (rt - ckr[:, sl])
                if masked:
                    t = jnp.where(visible(c), t, MASK_VALUE)
                ps.append(jnp.exp2(t).astype(BF16))
            p = jnp.concatenate(ps, axis=1)
            acc_ref[g] += _dot(p, vaug_ref[g, pl.ds(ks, width), :])

    def slow_step(ki, width, masked):
        ks = pl.multiple_of(ki * tq, tq)
        for g in range(heads):
            t = scores(g, ks, width) - ck_ref[g, :, pl.ds(ks, width)]
            if masked:
                t = jnp.concatenate(
                    [jnp.where(visible(c), t[:, c * LANES:(c + 1) * LANES], MASK_VALUE)
                     for c in range(width // LANES)], axis=1)
            m_prev = m_ref[g]
            m_new = jnp.maximum(m_prev, jnp.max(t, axis=-1, keepdims=True))
            alpha = jnp.exp2(m_prev - m_new)
            p = jnp.exp2(t - m_new[:, 0:1])
            pv = _dot(p.astype(BF16), vaug_ref[g, pl.ds(ks, width), :])
            acc_ref[g] = acc_ref[g] * jnp.concatenate([alpha, alpha], axis=1) + pv
            m_ref[g] = m_new

    def sweep(step):
        def body(kk, carry):
            step(4 * kk, 4 * tq, False)
            return carry
        lax.fori_loop(0, qi // 4, body, 0)
        rem = qi % 4

        @pl.when(rem >= 2)
        def _():
            step(qi - rem, 2 * tq, False)

        @pl.when(rem % 2 == 1)
        def _():
            step(qi - 1, tq, False)

        step(qi, tq, True)

    @pl.when(flag_ref[0] == 1)
    def _():
        qs = pl.multiple_of(qi * tq, tq)
        for g in range(heads):
            ckq = jnp.broadcast_to(ck_ref[g, :, pl.ds(qs, tq)], (LANES, tq))
            rt_ref[g] = ckq.T - b_ref[...]
        sweep(fast_step)

    @pl.when(flag_ref[0] != 1)
    def _():
        m_ref[...] = jnp.full(m_ref.shape, MASK_VALUE, F32)
        sweep(slow_step)

    for g in range(heads):
        acc = acc_ref[g]
        o_ref[:, hs[g]] = (acc[:, 0:HEAD_DIM] / acc[:, HEAD_DIM:]).astype(o_ref.dtype)


def _attention(qkv, ck2, bound):
    S = qkv.shape[0]
    H = ck2.shape[0]
    tq, heads = 512, 2
    hw = heads * HEAD_DIM
    flag = (bound <= ATTN_FAST_MAX_BOUND).astype(jnp.int32).reshape(1)
    bvec = jnp.broadcast_to(bound.astype(F32), (1, LANES))
    grid_spec = pltpu.PrefetchScalarGridSpec(
        num_scalar_prefetch=1,
        grid=(H // heads, S // tq),
        in_specs=[pl.BlockSpec((tq, hw), lambda h, i, f: (i, h)),
                  pl.BlockSpec((S, hw), lambda h, i, f: (0, H // heads + h)),
                  pl.BlockSpec((S, hw), lambda h, i, f: (0, 2 * (H // heads) + h)),
                  pl.BlockSpec((heads, 1, S), lambda h, i, f: (h, 0, 0)),
                  pl.BlockSpec((1, LANES), lambda h, i, f: (0, 0))],
        out_specs=pl.BlockSpec((tq, hw), lambda h, i, f: (i, h)),
        scratch_shapes=[pltpu.VMEM((heads, tq, 2 * HEAD_DIM), F32),
                        pltpu.VMEM((heads, tq, HEAD_DIM), F32),
                        pltpu.VMEM((heads, S, 2 * HEAD_DIM), BF16),
                        pltpu.VMEM((heads, tq, LANES), F32)])
    return pl.pallas_call(
        functools.partial(_attn_kernel, tq=tq, heads=heads),
        grid_spec=grid_spec,
        out_shape=jax.ShapeDtypeStruct((S, H * HEAD_DIM), BF16),
        compiler_params=_cparams(2, BIG_VMEM_MB),
        name="fox_attn",
    )(flag, qkv, qkv, qkv, ck2, bvec)


def _router_kernel(x_ref, g_ref, sc_ref, sh_ref, rw_ref, rb_ref, o_ref):
    h = _norm_modulate(x_ref[...], g_ref[0], sc_ref[0], sh_ref[0])
    w = rw_ref[...]
    h_hi, w_hi = h.astype(BF16), w.astype(BF16)
    h_lo = (h - h_hi.astype(F32)).astype(BF16)
    w_lo = (w - w_hi.astype(F32)).astype(BF16)
    o_ref[...] = _dot(h_hi, w_hi) + (_dot(h_hi, w_lo) + _dot(h_lo, w_hi)) + rb_ref[...]


def _router(x, norm_g, mod, layer, rw_pad, rb_pad):
    S, D = x.shape
    tm = 512
    return pl.pallas_call(
        _router_kernel,
        grid=(S // tm,),
        in_specs=[pl.BlockSpec((tm, D), lambda i: (i, 0)),
                  _vec_spec(layer, D), _mod_spec(layer, 4, D), _mod_spec(layer, 3, D),
                  pl.BlockSpec((D, LANES), lambda i: (0, 0)),
                  pl.BlockSpec((1, LANES), lambda i: (0, 0))],
        out_specs=pl.BlockSpec((tm, LANES), lambda i: (i, 0)),
        out_shape=jax.ShapeDtypeStruct((S, LANES), F32),
        compiler_params=_cparams(1, 32),
        name="moe_router",
    )(x, norm_g, mod, mod, rw_pad, rb_pad)


def _row_copy(src_hbm, dst, src_row, dst_row, sem):
    return pltpu.make_async_copy(src_hbm.at[pl.ds(src_row, 1), :], dst.at[pl.ds(dst_row, 1), :], sem)


def _gathered_rows(idx_ref, src_hbm, buf, sem, tm, n_per_row):
    i = pl.program_id(0)

    def start_copies(step, slot):
        def body(r, c):
            for k in range(n_per_row):
                src_row = idx_ref[(step * tm + r) * n_per_row + k]
                _row_copy(src_hbm, buf.at[slot, k], src_row, r, sem.at[slot]).start()
            return c
        lax.fori_loop(0, tm, body, 0, unroll=8)

    @pl.when(i == 0)
    def _():
        start_copies(0, 0)

    @pl.when(i + 1 < pl.num_programs(0))
    def _():
        start_copies(i + 1, (i + 1) % 2)

    slot = i % 2
    for k in range(n_per_row):
        pltpu.make_async_copy(src_hbm.at[pl.ds(0, tm), :], buf.at[slot, k], sem.at[slot]).wait()
    return slot


def _gather_norm_kernel(tok_ref, x_hbm, g_ref, sc_ref, sh_ref, o_ref, buf, sem, *, tm):
    slot = _gathered_rows(tok_ref, x_hbm, buf, sem, tm, 1)
    o_ref[...] = _norm_modulate(buf[slot, 0], g_ref[0], sc_ref[0], sh_ref[0]).astype(o_ref.dtype)


def _gather_norm(tok_sorted, x, norm_g, mod, layer):
    S, D = x.shape
    n_rows = tok_sorted.shape[0]
    tm = 256
    grid_spec = pltpu.PrefetchScalarGridSpec(
        num_scalar_prefetch=1,
        grid=(n_rows // tm,),
        in_specs=[pl.BlockSpec(memory_space=pl.ANY),
                  pl.BlockSpec((1, 1, D), lambda i, t: (layer, 0, 0)),
                  pl.BlockSpec((1, 1, D), lambda i, t: (layer, 0, 4)),
                  pl.BlockSpec((1, 1, D), lambda i, t: (layer, 0, 3))],
        out_specs=pl.BlockSpec((tm, D), lambda i, t: (i, 0)),
        scratch_shapes=[pltpu.VMEM((2, 1, tm, D), F32), pltpu.SemaphoreType.DMA((2,))])
    return pl.pallas_call(
        functools.partial(_gather_norm_kernel, tm=tm),
        grid_spec=grid_spec,
        out_shape=jax.ShapeDtypeStruct((n_rows, D), BF16),
        compiler_params=_cparams(1, 32),
        name="moe_gather",
    )(tok_sorted, x, norm_g, mod, mod)


def _for_expert_sub_blocks(o_ref, tile, lo, hi, compute):
    tm = o_ref.shape[0]
    whole = (lo <= tile * tm) & (hi >= (tile + 1) * tm)

    @pl.when(whole)
    def _():
        o_ref[...] = compute(slice(0, tm)).astype(o_ref.dtype)

    for sb in range(tm // MOE_SUB):
        rs = slice(sb * MOE_SUB, (sb + 1) * MOE_SUB)
        start = tile * tm + sb * MOE_SUB

        @pl.when(jnp.logical_not(whole) & (lo <= start) & (start < hi))
        def _(rs=rs):
            o_ref[rs, :] = compute(rs).astype(o_ref.dtype)


def _moe_item(kind, o_ref, tile, lo, hi, compute):
    @pl.when(kind == 1)
    def _():
        _for_expert_sub_blocks(o_ref, tile, lo, hi, compute)

    @pl.when(kind == 2)
    def _():
        _for_expert_sub_blocks(o_ref, tile, lo, hi, lambda rs: jnp.zeros((rs.stop - rs.start, o_ref.shape[1]), F32))


def _stream_expert_weights(exp_ref, nxt_ref, lastrun_ref, w_hbms, stages, caches, sems, layer, tn):
    n, i = pl.program_id(0), pl.program_id(1)

    def copy(k, e, col_tile):
        cols = pl.ds(pl.multiple_of(col_tile * tn, tn), tn)
        return pltpu.make_async_copy(w_hbms[k].at[layer, e, :, cols], stages[k], sems.at[k])

    @pl.when((n == 0) & (i == 0))
    def _():
        for k in range(len(w_hbms)):
            copy(k, exp_ref[0], 0).start()

    @pl.when((i == 0) | (exp_ref[i] != exp_ref[jnp.maximum(i - 1, 0)]))
    def _():
        for k in range(len(w_hbms)):
            copy(k, 0, 0).wait()
            caches[k][...] = stages[k][...].astype(BF16)
        nn = jnp.where(lastrun_ref[i] == 1, n + 1, n)

        @pl.when(nn < pl.num_programs(0))
        def _():
            for k in range(len(w_hbms)):
                copy(k, nxt_ref[i], nn).start()


def _moe_gu_kernel(tile_ref, exp_ref, kind_ref, lo_ref, hi_ref, nxt_ref, lastrun_ref,
                   a_ref, wg_hbm, wu_hbm, o_ref, wg_stage, wu_stage, wgb, wub, sems, *, layer):
    i = pl.program_id(1)
    _stream_expert_weights(exp_ref, nxt_ref, lastrun_ref, (wg_hbm, wu_hbm), (wg_stage, wu_stage), (wgb, wub),
                           sems, layer, o_ref.shape[1])

    def compute(rs):
        a = a_ref[rs, :]
        return _silu(_dot(a, wgb[...])) * _dot(a, wub[...])

    _moe_item(kind_ref[i], o_ref, tile_ref[i], lo_ref[i], hi_ref[i], compute)


def _moe_gate_up(items, xs, gate_w, up_w, j):
    n_rows, D = xs.shape
    F = gate_w.shape[-1]
    tm, tn = MOE_TM, 1024
    n_items = items[0].shape[0]
    grid_spec = pltpu.PrefetchScalarGridSpec(
        num_scalar_prefetch=len(items),
        grid=(F // tn, n_items),
        in_specs=[pl.BlockSpec((tm, D), lambda n, i, tile, *_: (tile[i], 0)),
                  pl.BlockSpec(memory_space=pl.ANY), pl.BlockSpec(memory_space=pl.ANY)],
        out_specs=pl.BlockSpec((tm, tn), lambda n, i, tile, *_: (tile[i], n)),
        scratch_shapes=[pltpu.VMEM((D, tn), F32), pltpu.VMEM((D, tn), F32),
                        pltpu.VMEM((D, tn), BF16), pltpu.VMEM((D, tn), BF16),
                        pltpu.SemaphoreType.DMA((2,))])
    return pl.pallas_call(
        functools.partial(_moe_gu_kernel, layer=j),
        grid_spec=grid_spec,
        out_shape=jax.ShapeDtypeStruct((n_rows, F), BF16),
        compiler_params=_cparams(2, BIG_VMEM_MB),
        name="moe_gate_up",
    )(*items, xs, gate_w, up_w)


def _moe_down_kernel(tile_ref, exp_ref, kind_ref, lo_ref, hi_ref, nxt_ref, lastrun_ref,
                     a_ref, w_hbm, o_ref, w_stage, wb, sems, *, layer):
    i = pl.program_id(1)
    _stream_expert_weights(exp_ref, nxt_ref, lastrun_ref, (w_hbm,), (w_stage,), (wb,), sems, layer,
                           o_ref.shape[1])

    _moe_item(kind_ref[i], o_ref, tile_ref[i], lo_ref[i], hi_ref[i], lambda rs: _dot(a_ref[rs, :], wb[...]))


def _moe_down(items, hid, down_w, j):
    n_rows, F = hid.shape
    D = down_w.shape[-1]
    tm, tn = MOE_TM, 512
    n_items = items[0].shape[0]
    grid_spec = pltpu.PrefetchScalarGridSpec(
        num_scalar_prefetch=len(items),
        grid=(D // tn, n_items),
        in_specs=[pl.BlockSpec((tm, F), lambda n, i, tile, *_: (tile[i], 0)),
                  pl.BlockSpec(memory_space=pl.ANY)],
        out_specs=pl.BlockSpec((tm, tn), lambda n, i, tile, *_: (tile[i], n)),
        scratch_shapes=[pltpu.VMEM((F, tn), F32), pltpu.VMEM((F, tn), BF16), pltpu.SemaphoreType.DMA((1,))])
    return pl.pallas_call(
        functools.partial(_moe_down_kernel, layer=j),
        grid_spec=grid_spec,
        out_shape=jax.ShapeDtypeStruct((n_rows, D), F32),
        compiler_params=_cparams(2, BIG_VMEM_MB),
        name="moe_down",
    )(*items, hid, down_w)


def _combined_rows(pos_ref, y_hbm, x_ref, g_ref, w_ref, buf, sem, tm):
    slot = _gathered_rows(pos_ref, y_hbm, buf, sem, tm, MOE_TOP_K)
    w = w_ref[...]
    y = buf[slot, 0] * w[:, 0:1] + buf[slot, 1] * w[:, 1:2]
    return x_ref[...] + g_ref[0] * y


def _combine_kernel(pos_ref, y_hbm, x_ref, g_ref, w_ref, o_ref, buf, sem, *, tm):
    o_ref[...] = _combined_rows(pos_ref, y_hbm, x_ref, g_ref, w_ref, buf, sem, tm)


def _combine_norm_kernel(pos_ref, y_hbm, x_ref, g_ref, w_ref, ng_ref, nsc_ref, nsh_ref, o_ref, h_ref, buf, sem,
                         *, tm):
    x_new = _combined_rows(pos_ref, y_hbm, x_ref, g_ref, w_ref, buf, sem, tm)
    o_ref[...] = x_new
    h_ref[...] = _norm_modulate(x_new, ng_ref[0], nsc_ref[0], nsh_ref[0]).astype(h_ref.dtype)


def _combine(pos, y_sorted, x, mod, layer, top_w, next_norm_g=None):
    S, D = x.shape
    tm = 256
    row_spec = pl.BlockSpec((tm, D), lambda i, p: (i, 0))
    in_specs = [pl.BlockSpec(memory_space=pl.ANY),
                row_spec,
                pl.BlockSpec((1, 1, D), lambda i, p: (layer, 0, 5)),
                pl.BlockSpec((tm, MOE_TOP_K), lambda i, p: (i, 0))]
    args = (pos, y_sorted, x, mod, top_w)
    out_specs, out_shape, body = row_spec, jax.ShapeDtypeStruct((S, D), F32), _combine_kernel
    if next_norm_g is not None:
        in_specs += [pl.BlockSpec((1, 1, D), lambda i, p: (layer + 1, 0, 0)),
                     pl.BlockSpec((1, 1, D), lambda i, p: (layer + 1, 0, 1)),
                     pl.BlockSpec((1, 1, D), lambda i, p: (layer + 1, 0, 0))]
        args += (next_norm_g, mod, mod)
        out_specs = [row_spec, row_spec]
        out_shape = [out_shape, jax.ShapeDtypeStruct((S, D), BF16)]
        body = _combine_norm_kernel
    grid_spec = pltpu.PrefetchScalarGridSpec(
        num_scalar_prefetch=1,
        grid=(S // tm,),
        in_specs=in_specs,
        out_specs=out_specs,
        scratch_shapes=[pltpu.VMEM((2, MOE_TOP_K, tm, D), F32), pltpu.SemaphoreType.DMA((2,))])
    return pl.pallas_call(
        functools.partial(body, tm=tm),
        grid_spec=grid_spec,
        out_shape=out_shape,
        compiler_params=_cparams(1, 40),
        name="moe_combine",
    )(*args)


def _work_items(starts, ends, counts, n_rows, tm):
    n_experts = counts.shape[0]
    starts = jnp.concatenate([starts, ends[-1:]])
    ends = jnp.concatenate([ends, jnp.full((1,), n_rows, ends.dtype)])
    counts = jnp.concatenate([counts, n_rows - ends[-2:-1]])
    n_items = n_rows // tm + n_experts
    first_tile = starts // tm
    n_e = jnp.where(counts > 0, (ends + tm - 1) // tm - first_tile, 0)
    item_end = jnp.cumsum(n_e)
    item_start = item_end - n_e
    total = item_end[-1]
    idx = jnp.arange(n_items, dtype=jnp.int32)
    idx_c = jnp.minimum(idx, total - 1)
    item_r = jnp.minimum(jnp.searchsorted(item_end, idx_c, side='right'), n_experts).astype(jnp.int32)
    item_tile = (first_tile[item_r] + idx_c - item_start[item_r]).astype(jnp.int32)
    kind = jnp.where(idx < total, jnp.where(item_r == n_experts, 2, 1), 0).astype(jnp.int32)
    last_expert = jnp.max(jnp.where(counts[:n_experts] > 0, jnp.arange(n_experts, dtype=jnp.int32), 0))
    item_e = jnp.where(item_r == n_experts, last_expert, item_r).astype(jnp.int32)
    run_start = jnp.concatenate([jnp.ones((1,), bool), item_e[1:] != item_e[:-1]])
    next_start = lax.cummin(jnp.where(run_start, idx, n_items), reverse=True)
    next_start = jnp.concatenate([next_start[1:], jnp.full((1,), n_items, jnp.int32)])
    last_run = next_start >= n_items
    next_e = jnp.where(last_run, item_e[0], item_e[jnp.minimum(next_start, n_items - 1)])
    return (item_tile, item_e, kind, starts[item_r].astype(jnp.int32), ends[item_r].astype(jnp.int32),
            next_e.astype(jnp.int32), last_run.astype(jnp.int32))


def _route(logits, n_experts):
    S = logits.shape[0]
    top_logit, top_idx = lax.top_k(logits, MOE_TOP_K)
    top_w = jax.nn.softmax(top_logit, axis=-1)
    n_assign = S * MOE_TOP_K
    n_rows = n_assign + n_experts * MOE_SUB
    flat_e = top_idx.reshape(n_assign).astype(jnp.int32)
    onehot = (flat_e[:, None] == jnp.arange(n_experts, dtype=jnp.int32)[None, :]).astype(jnp.int32)
    csum = jnp.cumsum(onehot, axis=0)
    rank = jnp.sum((csum - 1) * onehot, axis=1)
    counts = csum[-1]
    dense_starts = jnp.cumsum(counts) - counts
    padded = (counts + MOE_SUB - 1) // MOE_SUB * MOE_SUB
    ends = jnp.cumsum(padded)
    starts = ends - padded
    pos = (starts[flat_e] + rank).astype(jnp.int32)
    keys = jnp.sort(flat_e * n_assign + jnp.arange(n_assign, dtype=jnp.int32))
    tok_dense = (keys % n_assign) // MOE_TOP_K
    row = jnp.arange(n_rows, dtype=jnp.int32)
    row_e = jnp.minimum(jnp.searchsorted(ends, row, side='right'), n_experts - 1)
    offset = row - starts[row_e]
    is_real = (offset >= 0) & (offset < counts[row_e])
    tok_rows = jnp.where(is_real, tok_dense[jnp.clip(dense_starts[row_e] + offset, 0, n_assign - 1)], 0)
    return tok_rows.astype(jnp.int32), pos, top_w, (starts, ends, padded)


def kernel(x, c, ada_w, ada_b, mix_norm, ffn_norm, conv_pw1_w, conv_pw1_b, conv_dw_w, conv_dw_b, conv_ln_g, conv_ln_b, conv_pw2_w, conv_pw2_b, fox_qkv_w, fox_o_w, fox_fgate_w, fox_fgate_b, fox_q_norm, fox_k_norm, ffn_gate_w, ffn_up_w, ffn_down_w, moe_router_w, moe_router_b, moe_gate_w, moe_up_w, moe_down_w):
    B, S, D = x.shape
    assert B == 1, "kernels are written for a single sequence"
    L = ada_w.shape[0]
    H = fox_fgate_w.shape[-1]
    E = moe_router_w.shape[-1]

    def as_rows(v):
        return v.reshape(v.shape[0], 1, v.shape[1])

    mod = _adaln(c.reshape(D, 1), ada_w, ada_b)
    mix_g, ffn_g = as_rows(mix_norm), as_rows(ffn_norm)
    zero_bias = jnp.zeros((1, 1, D), F32)
    xs = x.reshape(S, D)

    h_next = None
    for i in range(L):
        j = i // 2
        h = _norm_mod(xs, mix_g, mod, i, 0) if h_next is None else h_next
        h_next = None
        if i % 2 == 0:
            u = _pw1_glu(h, conv_pw1_w, as_rows(conv_pw1_b), j)
            v = _conv_ln(u, conv_dw_w, as_rows(conv_dw_b), as_rows(conv_ln_g), as_rows(conv_ln_b), j)
            xs = _mm_res(v, conv_pw2_w, as_rows(conv_pw2_b), xs, mod, j, i, 2, DENSE_TM, 512, "conv_pw2")
        else:
            q_gain = fox_q_norm[j] * (HEAD_DIM ** -0.5 * LOG2E)
            gain = jnp.concatenate([jnp.tile(q_gain, H), jnp.tile(fox_k_norm[j], H),
                                    jnp.ones((D,), F32)])[None, :]
            qkv = _qkv(h, fox_qkv_w, gain, j)
            cum = _fgate_cum(h, jnp.swapaxes(fox_fgate_w, 1, 2), fox_fgate_b, j)
            bound = 1.02 * HEAD_DIM * jnp.max(jnp.abs(q_gain)) * jnp.max(jnp.abs(fox_k_norm[j]))
            o = _attention(qkv, (cum * LOG2E).reshape(H, 1, S), bound)
            xs = _mm_res(o, fox_o_w, zero_bias, xs, mod, j, i, 2, DENSE_TM, 512, "fox_o")
        if i % 2 == 0:
            h = _norm_mod(xs, ffn_g, mod, i, 3)
            hid = _gate_up(h, ffn_gate_w, ffn_up_w, j)
            xs = _mm_res(hid, ffn_down_w, zero_bias, xs, mod, j, i, 5, 1024, 256, "ffn_down")
        else:
            rw_pad = jnp.zeros((D, LANES), F32).at[:, :E].set(moe_router_w[j])
            rb_pad = jnp.zeros((1, LANES), F32).at[0, :E].set(moe_router_b[j])
            logits = _router(xs, ffn_g, mod, i, rw_pad, rb_pad)[:, :E]
            tok_rows, pos, top_w, ranges = _route(logits, E)
            items = _work_items(*ranges, tok_rows.shape[0], MOE_TM)
            xg = _gather_norm(tok_rows, xs, ffn_g, mod, i)
            hid = _moe_gate_up(items, xg, moe_gate_w, moe_up_w, j)
            y_sorted = _moe_down(items, hid, moe_down_w, j)
            if i + 1 < L:
                xs, h_next = _combine(pos, y_sorted, xs, mod, i, top_w, mix_g)
            else:
                xs = _combine(pos, y_sorted, xs, mod, i, top_w)
    return xs.reshape(B, S, D)
```

```python
import functools

import jax
import jax.numpy as jnp
from jax import lax
from jax.experimental import pallas as pl
from jax.experimental.pallas import tpu as pltpu

F32 = jnp.float32
BF16 = jnp.bfloat16

NORM_EPS = 1e-6
HEAD_DIM = 128
CONV_WIDTH = 31
CONV_HALO = 32
MOE_TOP_K = 2
LOG2E = 1.4426950408889634
MASK_VALUE = -1e30
ATTN_FAST_MAX_BOUND = 40.0
LANES = 128
SUBLANES = 8
BIG_VMEM_MB = 57
DENSE_TM = 2048
MOE_SUB = 256
MOE_TM = 512


def _cparams(n_axes, vmem_mb):
    return pltpu.CompilerParams(
        dimension_semantics=("arbitrary",) * n_axes,
        vmem_limit_bytes=vmem_mb << 20)


def _norm_modulate(x, g, sc, sh):
    ms = jnp.mean(x * x, axis=-1, keepdims=True)
    y = x * lax.rsqrt(ms + NORM_EPS) * g
    return y * (1.0 + sc) + sh


def _silu(x):
    return x * jax.nn.sigmoid(x)


def _adaln_kernel(c_ref, w_ref, b_ref, o_ref):
    c = c_ref[...]
    o_ref[0] = jnp.sum(w_ref[0] * _silu(c), axis=0, keepdims=True) + b_ref[0]


def _adaln(c_col, ada_w, ada_b):
    L, D, N = ada_w.shape
    tn = 1024
    return pl.pallas_call(
        _adaln_kernel,
        grid=(L, N // tn),
        in_specs=[pl.BlockSpec((D, 1), lambda l, j: (0, 0)),
                  pl.BlockSpec((1, D, tn), lambda l, j: (l, 0, j)),
                  pl.BlockSpec((1, 1, tn), lambda l, j: (l, 0, j))],
        out_specs=pl.BlockSpec((1, 1, tn), lambda l, j: (l, 0, j)),
        out_shape=jax.ShapeDtypeStruct((L, 1, N), F32),
        compiler_params=_cparams(2, 40),
        name="adaln",
    )(c_col, ada_w, ada_b.reshape(L, 1, N))


def _mod_spec(layer, which, D):
    return pl.BlockSpec((1, 1, D), lambda *_: (layer, 0, which))


def _vec_spec(layer, D):
    return pl.BlockSpec((1, 1, D), lambda *_: (layer, 0, 0))


def _norm_mod_kernel(x_ref, g_ref, sc_ref, sh_ref, o_ref):
    o_ref[...] = _norm_modulate(x_ref[...], g_ref[0], sc_ref[0], sh_ref[0]).astype(o_ref.dtype)


def _norm_mod(x, norm_g, mod, layer, which_shift):
    S, D = x.shape
    tm = 512
    return pl.pallas_call(
        _norm_mod_kernel,
        grid=(S // tm,),
        in_specs=[pl.BlockSpec((tm, D), lambda i: (i, 0)),
                  _vec_spec(layer, D),
                  _mod_spec(layer, which_shift + 1, D),
                  _mod_spec(layer, which_shift, D)],
        out_specs=pl.BlockSpec((tm, D), lambda i: (i, 0)),
        out_shape=jax.ShapeDtypeStruct((S, D), BF16),
        compiler_params=_cparams(1, 32),
        name="norm_mod",
    )(x, norm_g, mod, mod)


def _dot(a, w):
    return jnp.dot(a, w, preferred_element_type=F32)


def _pw1_glu_kernel(a_ref, wv_ref, wg_ref, bv_ref, bg_ref, o_ref):
    a = a_ref[...]
    val = _dot(a, wv_ref[...].astype(BF16)) + bv_ref[0]
    gate = _dot(a, wg_ref[...].astype(BF16)) + bg_ref[0]
    o_ref[...] = val * jax.nn.sigmoid(gate)


def _pw1_glu(h, pw1_w, pw1_b, j):
    S, D = h.shape
    tm, tn = DENSE_TM, 512
    nt = D // tn
    return pl.pallas_call(
        _pw1_glu_kernel,
        grid=(S // tm, nt),
        in_specs=[pl.BlockSpec((tm, D), lambda i, n: (i, 0)),
                  pl.BlockSpec((None, D, tn), lambda i, n: (j, 0, n)),
                  pl.BlockSpec((None, D, tn), lambda i, n: (j, 0, n + nt)),
                  pl.BlockSpec((1, 1, tn), lambda i, n: (j, 0, n)),
                  pl.BlockSpec((1, 1, tn), lambda i, n: (j, 0, n + nt))],
        out_specs=pl.BlockSpec((tm, tn), lambda i, n: (i, n)),
        out_shape=jax.ShapeDtypeStruct((S, D), F32),
        compiler_params=_cparams(2, BIG_VMEM_MB),
        name="pw1_glu",
    )(h, pw1_w, pw1_w, pw1_b, pw1_b)


def _gate_up_kernel(a_ref, wg_ref, wu_ref, o_ref):
    a = a_ref[...]
    g = _dot(a, wg_ref[...].astype(BF16))
    u = _dot(a, wu_ref[...].astype(BF16))
    o_ref[...] = (_silu(g) * u).astype(o_ref.dtype)


def _gate_up(h, gate_w, up_w, j):
    S, D = h.shape
    F = gate_w.shape[-1]
    tm, tn = DENSE_TM, 512
    return pl.pallas_call(
        _gate_up_kernel,
        grid=(S // tm, F // tn),
        in_specs=[pl.BlockSpec((tm, D), lambda i, n: (i, 0)),
                  pl.BlockSpec((None, D, tn), lambda i, n: (j, 0, n)),
                  pl.BlockSpec((None, D, tn), lambda i, n: (j, 0, n))],
        out_specs=pl.BlockSpec((tm, tn), lambda i, n: (i, n)),
        out_shape=jax.ShapeDtypeStruct((S, F), BF16),
        compiler_params=_cparams(2, BIG_VMEM_MB),
        name="ffn_gate_up",
    )(h, gate_w, up_w)


def _mm_res_kernel(a_ref, w_ref, b_ref, x_ref, g_ref, o_ref):
    y = _dot(a_ref[...], w_ref[...].astype(BF16)) + b_ref[0]
    o_ref[...] = x_ref[...] + g_ref[0] * y


def _mm_res(a, w, bias, x, mod, j, layer, which_gate, tm, tn, name):
    S, K = a.shape
    D = w.shape[-1]
    bj = j if bias.shape[0] > 1 else 0
    return pl.pallas_call(
        _mm_res_kernel,
        grid=(S // tm, D // tn),
        in_specs=[pl.BlockSpec((tm, K), lambda i, n: (i, 0)),
                  pl.BlockSpec((None, K, tn), lambda i, n: (j, 0, n)),
                  pl.BlockSpec((1, 1, tn), lambda i, n: (bj, 0, n)),
                  pl.BlockSpec((tm, tn), lambda i, n: (i, n)),
                  pl.BlockSpec((1, 1, tn), lambda i, n: (layer, 0, which_gate * (D // tn) + n))],
        out_specs=pl.BlockSpec((tm, tn), lambda i, n: (i, n)),
        out_shape=jax.ShapeDtypeStruct((S, D), F32),
        compiler_params=_cparams(2, BIG_VMEM_MB),
        name=name,
    )(a, w, bias, x, mod)


def _router_logits(h, rw, rb):
    h_hi, w_hi = h.astype(BF16), rw.astype(BF16)
    h_lo = (h - h_hi.astype(F32)).astype(BF16)
    w_lo = (rw - w_hi.astype(F32)).astype(BF16)
    return _dot(h_hi, w_hi) + (_dot(h_hi, w_lo) + _dot(h_lo, w_hi)) + rb


def _mixer_out_kernel(a_ref, w_hbm, b_ref, x_ref, gate_ref, ng_ref, nsc_ref, nsh_ref, *rest, j, router):
    if router:
        rw_ref, rb_ref, o_ref, aux_ref, w_stage, wb, sem = rest
    else:
        o_ref, aux_ref, w_stage, wb, sem = rest

    @pl.when(pl.program_id(0) == 0)
    def _():
        width = w_stage.shape[1]
        for c in range(wb.shape[1] // width):
            cols = slice(c * width, (c + 1) * width)
            cp = pltpu.make_async_copy(w_hbm.at[j, :, cols], w_stage, sem.at[0])
            cp.start()
            cp.wait()
            wb[:, cols] = w_stage[...].astype(BF16)

    y = _dot(a_ref[...], wb[...]) + b_ref[0]
    x_new = x_ref[...] + gate_ref[0] * y
    o_ref[...] = x_new
    h = _norm_modulate(x_new, ng_ref[0], nsc_ref[0], nsh_ref[0])
    if router:
        aux_ref[...] = _router_logits(h, rw_ref[...], rb_ref[...])
    else:
        aux_ref[...] = h.astype(aux_ref.dtype)


def _mixer_out(a, w, bias, x, mod, norm_g, j, layer, name, router=None):
    S, K = a.shape
    D = w.shape[-1]
    tm = 512
    bj = j if bias.shape[0] > 1 else 0
    row = lambda width: pl.BlockSpec((tm, width), lambda i: (i, 0))
    in_specs = [row(K), pl.BlockSpec(memory_space=pl.ANY), _vec_spec(bj, D), row(D), _mod_spec(layer, 2, D),
                _vec_spec(layer, D), _mod_spec(layer, 4, D), _mod_spec(layer, 3, D)]
    args = (a, w, bias, x, mod, norm_g, mod, mod)
    if router is None:
        aux_spec, aux_shape = row(D), jax.ShapeDtypeStruct((S, D), BF16)
    else:
        in_specs += [pl.BlockSpec((D, LANES), lambda i: (0, 0)), pl.BlockSpec((1, LANES), lambda i: (0, 0))]
        args += router
        aux_spec, aux_shape = row(LANES), jax.ShapeDtypeStruct((S, LANES), F32)
    return pl.pallas_call(
        functools.partial(_mixer_out_kernel, j=j, router=router is not None),
        grid=(S // tm,),
        in_specs=in_specs,
        out_specs=[row(D), aux_spec],
        out_shape=[jax.ShapeDtypeStruct((S, D), F32), aux_shape],
        scratch_shapes=[pltpu.VMEM((K, D // 2), F32), pltpu.VMEM((K, D), BF16), pltpu.SemaphoreType.DMA((1,))],
        compiler_params=_cparams(1, BIG_VMEM_MB),
        name=name,
    )(*args)


def _conv_ln_kernel(halo_ref, u_ref, w_ref, b_ref, g_ref, beta_ref, o_ref, win_ref, acc_ref, *, tm):
    i = pl.program_id(0)

    @pl.when(i == 0)
    def _():
        win_ref[0:CONV_HALO, :] = jnp.zeros((CONV_HALO, win_ref.shape[1]), F32)

    @pl.when(i > 0)
    def _():
        win_ref[0:CONV_HALO, :] = halo_ref[...]

    win_ref[CONV_HALO:CONV_HALO + tm, :] = u_ref[...]
    first = CONV_HALO - (CONV_WIDTH - 1)
    for c in range(u_ref.shape[1] // LANES):
        cs = slice(c * LANES, (c + 1) * LANES)
        out = None
        for b in range(SUBLANES):
            rows = tm if b == 0 else tm + SUBLANES
            y = None
            for j in range(first, first + CONV_WIDTH):
                if j % SUBLANES != b:
                    continue
                term = win_ref[j - b:j - b + rows, cs] * w_ref[0, j - first:j - first + 1, cs]
                y = term if y is None else y + term
            yb = y if b == 0 else y[b:b + tm]
            out = yb if out is None else out + yb
        acc_ref[:, cs] = out + b_ref[0][:, cs]
    y = acc_ref[...]
    mu = jnp.mean(y, axis=-1, keepdims=True)
    yc = y - mu
    var = jnp.mean(yc * yc, axis=-1, keepdims=True)
    z = yc * lax.rsqrt(var + NORM_EPS) * g_ref[0] + beta_ref[0]
    o_ref[...] = _silu(z).astype(o_ref.dtype)


def _conv_ln(u, dw_w, dw_b, ln_g, ln_b, j):
    S, D = u.shape
    tm = 128
    hb = tm // CONV_HALO
    return pl.pallas_call(
        functools.partial(_conv_ln_kernel, tm=tm),
        grid=(S // tm,),
        in_specs=[pl.BlockSpec((CONV_HALO, D), lambda i: (jnp.maximum(i * hb - 1, 0), 0)),
                  pl.BlockSpec((tm, D), lambda i: (i, 0)),
                  pl.BlockSpec((1, CONV_WIDTH, D), lambda i: (j, 0, 0)),
                  _vec_spec(j, D), _vec_spec(j, D), _vec_spec(j, D)],
        out_specs=pl.BlockSpec((tm, D), lambda i: (i, 0)),
        out_shape=jax.ShapeDtypeStruct((S, D), BF16),
        scratch_shapes=[pltpu.VMEM((CONV_HALO + tm, D), F32), pltpu.VMEM((tm, D), F32)],
        compiler_params=_cparams(1, 32),
        name="conv_ln",
    )(u, u, dw_w, dw_b, ln_g, ln_b)


def _qkv_kernel(a_ref, w_ref, gain_ref, o_ref, *, n_norm_tiles):
    is_norm_tile = pl.program_id(1) < n_norm_tiles
    acc = _dot(a_ref[...], w_ref[...].astype(BF16))
    gain = gain_ref[...]
    for h in range(acc.shape[1] // HEAD_DIM):
        sl = slice(h * HEAD_DIM, (h + 1) * HEAD_DIM)
        t = acc[:, sl]
        ms = jnp.mean(t * t, axis=-1, keepdims=True)
        r = jnp.where(is_norm_tile, lax.rsqrt(ms + NORM_EPS), 1.0)
        o_ref[:, sl] = (t * r * gain[:, sl]).astype(o_ref.dtype)


def _qkv(h, qkv_w, gain, j):
    S, D = h.shape
    N = qkv_w.shape[-1]
    tm, tn = DENSE_TM, 512
    return pl.pallas_call(
        functools.partial(_qkv_kernel, n_norm_tiles=2 * D // tn),
        grid=(S // tm, N // tn),
        in_specs=[pl.BlockSpec((tm, D), lambda i, n: (i, 0)),
                  pl.BlockSpec((None, D, tn), lambda i, n: (j, 0, n)),
                  pl.BlockSpec((1, tn), lambda i, n: (0, n))],
        out_specs=pl.BlockSpec((tm, tn), lambda i, n: (i, n)),
        out_shape=jax.ShapeDtypeStruct((S, N), BF16),
        compiler_params=_cparams(2, BIG_VMEM_MB),
        name="fox_qkv",
    )(h, qkv_w, gain)


def _fgate_kernel(h_ref, fwt_ref, fb_ref, tri_ref, o_ref, carry_ref):
    i = pl.program_id(0)

    @pl.when(i == 0)
    def _():
        carry_ref[...] = jnp.zeros(carry_ref.shape, F32)

    z = lax.dot_general(fwt_ref[...].astype(BF16), h_ref[...], (((1,), (1,)), ((), ())),
                        preferred_element_type=F32) + fb_ref[...][:, 0:1]
    lf = jnp.minimum(z, 0.0) - jnp.log1p(jnp.exp(-jnp.abs(z)))
    hi = lf.astype(BF16)
    r1 = lf - hi.astype(F32)
    mid = r1.astype(BF16)
    lo = (r1 - mid.astype(F32)).astype(BF16)
    tri = tri_ref[...]
    cs = _dot(hi, tri) + _dot(mid, tri) + _dot(lo, tri) + carry_ref[...][:, 0:1]
    o_ref[...] = cs
    carry_ref[...] = jnp.broadcast_to(cs[:, cs.shape[1] - 1:], carry_ref.shape)


def _fgate_cum(h, fgate_wt, fgate_b, j):
    S, D = h.shape
    H = fgate_wt.shape[1]
    tm = 512
    tri = jnp.triu(jnp.ones((tm, tm), F32)).astype(BF16)
    fb = jnp.broadcast_to(fgate_b[j][:, None], (H, LANES))
    return pl.pallas_call(
        _fgate_kernel,
        grid=(S // tm,),
        in_specs=[pl.BlockSpec((tm, D), lambda i: (i, 0)),
                  pl.BlockSpec((None, H, D), lambda i: (j, 0, 0)),
                  pl.BlockSpec((H, LANES), lambda i: (0, 0)),
                  pl.BlockSpec((tm, tm), lambda i: (0, 0))],
        out_specs=pl.BlockSpec((H, tm), lambda i: (0, i)),
        out_shape=jax.ShapeDtypeStruct((H, S), F32),
        scratch_shapes=[pltpu.VMEM((H, LANES), F32)],
        compiler_params=_cparams(1, 32),
        name="fox_fgate",
    )(h, fgate_wt, fb, tri)


def _attn_kernel(flag_ref, q_ref, k_ref, v_ref, ck_ref, b_ref, o_ref,
                 acc_ref, m_ref, vaug_ref, rt_ref, *, tq, heads):
    qi = pl.program_id(1)
    S = k_ref.shape[0]
    hs = [slice(g * HEAD_DIM, (g + 1) * HEAD_DIM) for g in range(heads)]
    acc_ref[...] = jnp.zeros(acc_ref.shape, F32)

    @pl.when(qi == 0)
    def _():
        for g in range(heads):
            vaug_ref[g, :, 0:HEAD_DIM] = v_ref[:, hs[g]]
            vaug_ref[g, :, HEAD_DIM:] = jnp.ones((S, HEAD_DIM), BF16)

    def scores(g, ks, width):
        return lax.dot_general(q_ref[:, hs[g]], k_ref[pl.ds(ks, width), hs[g]],
                               (((1,), (1,)), ((), ())), preferred_element_type=F32)

    def visible(c):
        row = lax.broadcasted_iota(jnp.int32, (tq, LANES), 0)
        col = lax.broadcasted_iota(jnp.int32, (tq, LANES), 1) + c * LANES
        return col <= row

    def fast_step(ki, width, masked):
        ks = pl.multiple_of(ki * tq, tq)
        for g in range(heads):
            s = scores(g, ks, width)
            ckr = ck_ref[g, :, pl.ds(ks, width)]
            rt = rt_ref[g]
            ps = []
            for c in range(width // LANES):
                sl = slice(c * LANES, (c + 1) * LANES)
                t = s[:, sl] + (rt - ckr[:, sl])
                if masked:
                    t = jnp.where(visible(c), t, MASK_VALUE)
                ps.append(jnp.exp2(t).astype(BF16))
            p = jnp.concatenate(ps, axis=1)
            acc_ref[g] += _dot(p, vaug_ref[g, pl.ds(ks, width), :])

    def slow_step(ki, width, masked):
        ks = pl.multiple_of(ki * tq, tq)
        for g in range(heads):
            t = scores(g, ks, width) - ck_ref[g, :, pl.ds(ks, width)]
            if masked:
                t = jnp.concatenate(
                    [jnp.where(visible(c), t[:, c * LANES:(c + 1) * LANES], MASK_VALUE)
                     for c in range(width // LANES)], axis=1)
            m_prev = m_ref[g]
            m_new = jnp.maximum(m_prev, jnp.max(t, axis=-1, keepdims=True))
            alpha = jnp.exp2(m_prev - m_new)
            p = jnp.exp2(t - m_new[:, 0:1])
            pv = _dot(p.astype(BF16), vaug_ref[g, pl.ds(ks, width), :])
            acc_ref[g] = acc_ref[g] * jnp.concatenate([alpha, alpha], axis=1) + pv
            m_ref[g] = m_new

    def sweep(step):
        def body(kk, carry):
            step(4 * kk, 4 * tq, False)
            return carry
        lax.fori_loop(0, qi // 4, body, 0)
        rem = qi % 4

        @pl.when(rem >= 2)
        def _():
            step(qi - rem, 2 * tq, False)

        @pl.when(rem % 2 == 1)
        def _():
            step(qi - 1, tq, False)

        step(qi, tq, True)

    @pl.when(flag_ref[0] == 1)
    def _():
        qs = pl.multiple_of(qi * tq, tq)
        for g in range(heads):
            ckq = jnp.broadcast_to(ck_ref[g, :, pl.ds(qs, tq)], (LANES, tq))
            rt_ref[g] = ckq.T - b_ref[...]
        sweep(fast_step)

    @pl.when(flag_ref[0] != 1)
    def _():
        m_ref[...] = jnp.full(m_ref.shape, MASK_VALUE, F32)
        sweep(slow_step)

    for g in range(heads):
        acc = acc_ref[g]
        o_ref[:, hs[g]] = (acc[:, 0:HEAD_DIM] / acc[:, HEAD_DIM:]).astype(o_ref.dtype)


def _attention(qkv, ck2, bound):
    S = qkv.shape[0]
    H = ck2.shape[0]
    tq, heads = 512, 2
    hw = heads * HEAD_DIM
    flag = (bound <= ATTN_FAST_MAX_BOUND).astype(jnp.int32).reshape(1)
    bvec = jnp.broadcast_to(bound.astype(F32), (1, LANES))
    grid_spec = pltpu.PrefetchScalarGridSpec(
        num_scalar_prefetch=1,
        grid=(H // heads, S // tq),
        in_specs=[pl.BlockSpec((tq, hw), lambda h, i, f: (i, h)),
                  pl.BlockSpec((S, hw), lambda h, i, f: (0, H // heads + h)),
                  pl.BlockSpec((S, hw), lambda h, i, f: (0, 2 * (H // heads) + h)),
                  pl.BlockSpec((heads, 1, S), lambda h, i, f: (h, 0, 0)),
                  pl.BlockSpec((1, LANES), lambda h, i, f: (0, 0))],
        out_specs=pl.BlockSpec((tq, hw), lambda h, i, f: (i, h)),
        scratch_shapes=[pltpu.VMEM((heads, tq, 2 * HEAD_DIM), F32),
                        pltpu.VMEM((heads, tq, HEAD_DIM), F32),
                        pltpu.VMEM((heads, S, 2 * HEAD_DIM), BF16),
                        pltpu.VMEM((heads, tq, LANES), F32)])
    return pl.pallas_call(
        functools.partial(_attn_kernel, tq=tq, heads=heads),
        grid_spec=grid_spec,
        out_shape=jax.ShapeDtypeStruct((S, H * HEAD_DIM), BF16),
        compiler_params=_cparams(2, BIG_VMEM_MB),
        name="fox_attn",
    )(flag, qkv, qkv, qkv, ck2, bvec)


def _row_copy(src_hbm, dst, src_row, dst_row, sem):
    return pltpu.make_async_copy(src_hbm.at[pl.ds(src_row, 1), :], dst.at[pl.ds(dst_row, 1), :], sem)


def _gathered_rows(idx_ref, src_hbm, buf, sem, tm, n_per_row):
    i = pl.program_id(0)

    def start_copies(step, slot):
        def body(r, c):
            for k in range(n_per_row):
                src_row = idx_ref[(step * tm + r) * n_per_row + k]
                _row_copy(src_hbm, buf.at[slot, k], src_row, r, sem.at[slot]).start()
            return c
        lax.fori_loop(0, tm, body, 0, unroll=8)

    @pl.when(i == 0)
    def _():
        start_copies(0, 0)

    @pl.when(i + 1 < pl.num_programs(0))
    def _():
        start_copies(i + 1, (i + 1) % 2)

    slot = i % 2
    for k in range(n_per_row):
        pltpu.make_async_copy(src_hbm.at[pl.ds(0, tm), :], buf.at[slot, k], sem.at[slot]).wait()
    return slot


def _gather_norm_kernel(tok_ref, x_hbm, g_ref, sc_ref, sh_ref, o_ref, buf, sem, *, tm):
    slot = _gathered_rows(tok_ref, x_hbm, buf, sem, tm, 1)
    o_ref[...] = _norm_modulate(buf[slot, 0], g_ref[0], sc_ref[0], sh_ref[0]).astype(o_ref.dtype)


def _gather_norm(tok_sorted, x, norm_g, mod, layer):
    S, D = x.shape
    n_rows = tok_sorted.shape[0]
    tm = 256
    grid_spec = pltpu.PrefetchScalarGridSpec(
        num_scalar_prefetch=1,
        grid=(n_rows // tm,),
        in_specs=[pl.BlockSpec(memory_space=pl.ANY),
                  pl.BlockSpec((1, 1, D), lambda i, t: (layer, 0, 0)),
                  pl.BlockSpec((1, 1, D), lambda i, t: (layer, 0, 4)),
                  pl.BlockSpec((1, 1, D), lambda i, t: (layer, 0, 3))],
        out_specs=pl.BlockSpec((tm, D), lambda i, t: (i, 0)),
        scratch_shapes=[pltpu.VMEM((2, 1, tm, D), F32), pltpu.SemaphoreType.DMA((2,))])
    return pl.pallas_call(
        functools.partial(_gather_norm_kernel, tm=tm),
        grid_spec=grid_spec,
        out_shape=jax.ShapeDtypeStruct((n_rows, D), BF16),
        compiler_params=_cparams(1, 32),
        name="moe_gather",
    )(tok_sorted, x, norm_g, mod, mod)


def _for_expert_sub_blocks(o_ref, tile, lo, hi, compute):
    tm = o_ref.shape[0]
    whole = (lo <= tile * tm) & (hi >= (tile + 1) * tm)

    @pl.when(whole)
    def _():
        o_ref[...] = compute(slice(0, tm)).astype(o_ref.dtype)

    for sb in range(tm // MOE_SUB):
        rs = slice(sb * MOE_SUB, (sb + 1) * MOE_SUB)
        start = tile * tm + sb * MOE_SUB

        @pl.when(jnp.logical_not(whole) & (lo <= start) & (start < hi))
        def _(rs=rs):
            o_ref[rs, :] = compute(rs).astype(o_ref.dtype)


def _moe_item(kind, o_ref, tile, lo, hi, compute):
    @pl.when(kind == 1)
    def _():
        _for_expert_sub_blocks(o_ref, tile, lo, hi, compute)

    @pl.when(kind == 2)
    def _():
        _for_expert_sub_blocks(o_ref, tile, lo, hi, lambda rs: jnp.zeros((rs.stop - rs.start, o_ref.shape[1]), F32))


def _stream_expert_weights(exp_ref, nxt_ref, lastrun_ref, w_hbms, stages, caches, sems, layer, tn):
    n, i = pl.program_id(0), pl.program_id(1)

    def copy(k, e, col_tile):
        cols = pl.ds(pl.multiple_of(col_tile * tn, tn), tn)
        return pltpu.make_async_copy(w_hbms[k].at[layer, e, :, cols], stages[k], sems.at[k])

    @pl.when((n == 0) & (i == 0))
    def _():
        for k in range(len(w_hbms)):
            copy(k, exp_ref[0], 0).start()

    @pl.when((i == 0) | (exp_ref[i] != exp_ref[jnp.maximum(i - 1, 0)]))
    def _():
        for k in range(len(w_hbms)):
            copy(k, 0, 0).wait()
            caches[k][...] = stages[k][...].astype(BF16)
        nn = jnp.where(lastrun_ref[i] == 1, n + 1, n)

        @pl.when(nn < pl.num_programs(0))
        def _():
            for k in range(len(w_hbms)):
                copy(k, nxt_ref[i], nn).start()


def _moe_gu_kernel(tile_ref, exp_ref, kind_ref, lo_ref, hi_ref, nxt_ref, lastrun_ref, in_tile_ref,
                   a_ref, wg_hbm, wu_hbm, o_ref, wg_stage, wu_stage, wgb, wub, sems, *, layer):
    i = pl.program_id(1)
    _stream_expert_weights(exp_ref, nxt_ref, lastrun_ref, (wg_hbm, wu_hbm), (wg_stage, wu_stage), (wgb, wub),
                           sems, layer, o_ref.shape[1])

    def compute(rs):
        a = a_ref[rs, :]
        return _silu(_dot(a, wgb[...])) * _dot(a, wub[...])

    _moe_item(kind_ref[i], o_ref, tile_ref[i], lo_ref[i], hi_ref[i], compute)


def _moe_gate_up(items, xs, gate_w, up_w, j):
    n_rows, D = xs.shape
    F = gate_w.shape[-1]
    tm, tn = MOE_TM, 1024
    n_items = items[0].shape[0]
    grid_spec = pltpu.PrefetchScalarGridSpec(
        num_scalar_prefetch=len(items),
        grid=(F // tn, n_items),
        in_specs=[pl.BlockSpec((tm, D), lambda n, i, *refs: (refs[-1][i], 0)),
                  pl.BlockSpec(memory_space=pl.ANY), pl.BlockSpec(memory_space=pl.ANY)],
        out_specs=pl.BlockSpec((tm, tn), lambda n, i, tile, *_: (tile[i], n)),
        scratch_shapes=[pltpu.VMEM((D, tn), F32), pltpu.VMEM((D, tn), F32),
                        pltpu.VMEM((D, tn), BF16), pltpu.VMEM((D, tn), BF16),
                        pltpu.SemaphoreType.DMA((2,))])
    return pl.pallas_call(
        functools.partial(_moe_gu_kernel, layer=j),
        grid_spec=grid_spec,
        out_shape=jax.ShapeDtypeStruct((n_rows, F), BF16),
        compiler_params=_cparams(2, BIG_VMEM_MB),
        name="moe_gate_up",
    )(*items, xs, gate_w, up_w)


def _moe_down_kernel(tile_ref, exp_ref, kind_ref, lo_ref, hi_ref, nxt_ref, lastrun_ref, in_tile_ref,
                     a_ref, w_hbm, o_ref, w_stage, wb, sems, *, layer):
    i = pl.program_id(1)
    _stream_expert_weights(exp_ref, nxt_ref, lastrun_ref, (w_hbm,), (w_stage,), (wb,), sems, layer,
                           o_ref.shape[1])

    _moe_item(kind_ref[i], o_ref, tile_ref[i], lo_ref[i], hi_ref[i], lambda rs: _dot(a_ref[rs, :], wb[...]))


def _moe_down(items, hid, down_w, j):
    n_rows, F = hid.shape
    D = down_w.shape[-1]
    tm, tn = MOE_TM, 512
    n_items = items[0].shape[0]
    grid_spec = pltpu.PrefetchScalarGridSpec(
        num_scalar_prefetch=len(items),
        grid=(D // tn, n_items),
        in_specs=[pl.BlockSpec((tm, F), lambda n, i, *refs: (refs[-1][i], 0)),
                  pl.BlockSpec(memory_space=pl.ANY)],
        out_specs=pl.BlockSpec((tm, tn), lambda n, i, tile, *_: (tile[i], n)),
        scratch_shapes=[pltpu.VMEM((F, tn), F32), pltpu.VMEM((F, tn), BF16), pltpu.SemaphoreType.DMA((1,))])
    return pl.pallas_call(
        functools.partial(_moe_down_kernel, layer=j),
        grid_spec=grid_spec,
        out_shape=jax.ShapeDtypeStruct((n_rows, D), F32),
        compiler_params=_cparams(2, BIG_VMEM_MB),
        name="moe_down",
    )(*items, hid, down_w)


def _combined_rows(pos_ref, y_hbm, x_ref, g_ref, w_ref, buf, sem, tm):
    slot = _gathered_rows(pos_ref, y_hbm, buf, sem, tm, MOE_TOP_K)
    w = w_ref[...]
    y = buf[slot, 0] * w[:, 0:1] + buf[slot, 1] * w[:, 1:2]
    return x_ref[...] + g_ref[0] * y


def _combine_kernel(pos_ref, y_hbm, x_ref, g_ref, w_ref, o_ref, buf, sem, *, tm):
    o_ref[...] = _combined_rows(pos_ref, y_hbm, x_ref, g_ref, w_ref, buf, sem, tm)


def _combine_norm_kernel(pos_ref, y_hbm, x_ref, g_ref, w_ref, ng_ref, nsc_ref, nsh_ref, o_ref, h_ref, buf, sem,
                         *, tm):
    x_new = _combined_rows(pos_ref, y_hbm, x_ref, g_ref, w_ref, buf, sem, tm)
    o_ref[...] = x_new
    h_ref[...] = _norm_modulate(x_new, ng_ref[0], nsc_ref[0], nsh_ref[0]).astype(h_ref.dtype)


def _combine(pos, y_sorted, x, mod, layer, top_w, next_norm_g=None):
    S, D = x.shape
    tm = 256
    row_spec = pl.BlockSpec((tm, D), lambda i, p: (i, 0))
    in_specs = [pl.BlockSpec(memory_space=pl.ANY),
                row_spec,
                pl.BlockSpec((1, 1, D), lambda i, p: (layer, 0, 5)),
                pl.BlockSpec((tm, MOE_TOP_K), lambda i, p: (i, 0))]
    args = (pos, y_sorted, x, mod, top_w)
    out_specs, out_shape, body = row_spec, jax.ShapeDtypeStruct((S, D), F32), _combine_kernel
    if next_norm_g is not None:
        in_specs += [pl.BlockSpec((1, 1, D), lambda i, p: (layer + 1, 0, 0)),
                     pl.BlockSpec((1, 1, D), lambda i, p: (layer + 1, 0, 1)),
                     pl.BlockSpec((1, 1, D), lambda i, p: (layer + 1, 0, 0))]
        args += (next_norm_g, mod, mod)
        out_specs = [row_spec, row_spec]
        out_shape = [out_shape, jax.ShapeDtypeStruct((S, D), BF16)]
        body = _combine_norm_kernel
    grid_spec = pltpu.PrefetchScalarGridSpec(
        num_scalar_prefetch=1,
        grid=(S // tm,),
        in_specs=in_specs,
        out_specs=out_specs,
        scratch_shapes=[pltpu.VMEM((2, MOE_TOP_K, tm, D), F32), pltpu.SemaphoreType.DMA((2,))])
    return pl.pallas_call(
        functools.partial(body, tm=tm),
        grid_spec=grid_spec,
        out_shape=out_shape,
        compiler_params=_cparams(1, 40),
        name="moe_combine",
    )(*args)


def _work_items(starts, ends, counts, n_rows, tm):
    n_experts = counts.shape[0]
    starts = jnp.concatenate([starts, ends[-1:]])
    ends = jnp.concatenate([ends, jnp.full((1,), n_rows, ends.dtype)])
    counts = jnp.concatenate([counts, n_rows - ends[-2:-1]])
    n_items = n_rows // tm + n_experts
    first_tile = starts // tm
    n_e = jnp.where(counts > 0, (ends + tm - 1) // tm - first_tile, 0)
    item_end = jnp.cumsum(n_e)
    item_start = item_end - n_e
    total = item_end[-1]
    idx = jnp.arange(n_items, dtype=jnp.int32)
    idx_c = jnp.minimum(idx, total - 1)
    item_r = jnp.minimum(jnp.sum((idx_c[:, None] >= item_end[None, :]).astype(jnp.int32), axis=1), n_experts)
    item_tile = (first_tile[item_r] + idx_c - item_start[item_r]).astype(jnp.int32)
    kind = jnp.where(idx < total, jnp.where(item_r == n_experts, 2, 1), 0).astype(jnp.int32)
    last_expert = jnp.max(jnp.where(counts[:n_experts] > 0, jnp.arange(n_experts, dtype=jnp.int32), 0))
    item_e = jnp.where(item_r == n_experts, last_expert, item_r).astype(jnp.int32)
    in_tile = jnp.where(item_r == n_experts, (ends[n_experts - 1] - 1) // tm, item_tile).astype(jnp.int32)
    run_start = jnp.concatenate([jnp.ones((1,), bool), item_e[1:] != item_e[:-1]])
    next_start = lax.cummin(jnp.where(run_start, idx, n_items), reverse=True)
    next_start = jnp.concatenate([next_start[1:], jnp.full((1,), n_items, jnp.int32)])
    last_run = next_start >= n_items
    next_e = jnp.where(last_run, item_e[0], item_e[jnp.minimum(next_start, n_items - 1)])
    return (item_tile, item_e, kind, starts[item_r].astype(jnp.int32), ends[item_r].astype(jnp.int32),
            next_e.astype(jnp.int32), last_run.astype(jnp.int32), in_tile)


def _route(logits, n_experts):
    S = logits.shape[0]
    top_logit, top_idx = lax.top_k(logits, MOE_TOP_K)
    top_w = jax.nn.softmax(top_logit, axis=-1)
    n_assign = S * MOE_TOP_K
    n_rows = n_assign + n_experts * MOE_SUB
    flat_e = top_idx.reshape(n_assign).astype(jnp.int32)
    onehot = (flat_e[:, None] == jnp.arange(n_experts, dtype=jnp.int32)[None, :]).astype(jnp.int32)
    csum = jnp.cumsum(onehot, axis=0)
    rank = jnp.sum((csum - 1) * onehot, axis=1)
    counts = csum[-1]
    dense_starts = jnp.cumsum(counts) - counts
    padded = (counts + MOE_SUB - 1) // MOE_SUB * MOE_SUB
    ends = jnp.cumsum(padded)
    starts = ends - padded
    pos = (starts[flat_e] + rank).astype(jnp.int32)
    keys = jnp.sort(flat_e * n_assign + jnp.arange(n_assign, dtype=jnp.int32))
    tok_dense = (keys % n_assign) // MOE_TOP_K
    row = jnp.arange(n_rows, dtype=jnp.int32)
    row_e = jnp.minimum(jnp.sum((row[:, None] >= ends[None, :]).astype(jnp.int32), axis=1), n_experts - 1)
    offset = row - starts[row_e]
    is_real = offset < counts[row_e]
    tok_rows = jnp.where(is_real, tok_dense[jnp.clip(dense_starts[row_e] + offset, 0, n_assign - 1)], row % S)
    return tok_rows.astype(jnp.int32), pos, top_w, (starts, ends, padded)


def kernel(x, c, ada_w, ada_b, mix_norm, ffn_norm, conv_pw1_w, conv_pw1_b, conv_dw_w, conv_dw_b, conv_ln_g, conv_ln_b, conv_pw2_w, conv_pw2_b, fox_qkv_w, fox_o_w, fox_fgate_w, fox_fgate_b, fox_q_norm, fox_k_norm, ffn_gate_w, ffn_up_w, ffn_down_w, moe_router_w, moe_router_b, moe_gate_w, moe_up_w, moe_down_w):
    B, S, D = x.shape
    assert B == 1, "kernels are written for a single sequence"
    L = ada_w.shape[0]
    H = fox_fgate_w.shape[-1]
    E = moe_router_w.shape[-1]

    def as_rows(v):
        return v.reshape(v.shape[0], 1, v.shape[1])

    mod = _adaln(c.reshape(D, 1), ada_w, ada_b)
    mix_g, ffn_g = as_rows(mix_norm), as_rows(ffn_norm)
    zero_bias = jnp.zeros((1, 1, D), F32)
    xs = x.reshape(S, D)

    h_next = None
    for i in range(L):
        j = i // 2
        h = _norm_mod(xs, mix_g, mod, i, 0) if h_next is None else h_next
        h_next = None
        if i % 2 == 0:
            u = _pw1_glu(h, conv_pw1_w, as_rows(conv_pw1_b), j)
            v = _conv_ln(u, conv_dw_w, as_rows(conv_dw_b), as_rows(conv_ln_g), as_rows(conv_ln_b), j)
            xs, h = _mixer_out(v, conv_pw2_w, as_rows(conv_pw2_b), xs, mod, ffn_g, j, i, "conv_pw2")
            hid = _gate_up(h, ffn_gate_w, ffn_up_w, j)
            xs = _mm_res(hid, ffn_down_w, zero_bias, xs, mod, j, i, 5, 1024, 256, "ffn_down")
        else:
            q_gain = fox_q_norm[j] * (HEAD_DIM ** -0.5 * LOG2E)
            gain = jnp.concatenate([jnp.tile(q_gain, H), jnp.tile(fox_k_norm[j], H),
                                    jnp.ones((D,), F32)])[None, :]
            qkv = _qkv(h, fox_qkv_w, gain, j)
            cum = _fgate_cum(h, jnp.swapaxes(fox_fgate_w, 1, 2), fox_fgate_b, j)
            bound = 1.02 * HEAD_DIM * jnp.max(jnp.abs(q_gain)) * jnp.max(jnp.abs(fox_k_norm[j]))
            o = _attention(qkv, (cum * LOG2E).reshape(H, 1, S), bound)
            rw_pad = jnp.zeros((D, LANES), F32).at[:, :E].set(moe_router_w[j])
            rb_pad = jnp.zeros((1, LANES), F32).at[0, :E].set(moe_router_b[j])
            xs, logits = _mixer_out(o, fox_o_w, zero_bias, xs, mod, ffn_g, j, i, "fox_o", router=(rw_pad, rb_pad))
            tok_rows, pos, top_w, ranges = _route(logits[:, :E], E)
            items = _work_items(*ranges, tok_rows.shape[0], MOE_TM)
            xg = _gather_norm(tok_rows, xs, ffn_g, mod, i)
            hid = _moe_gate_up(items, xg, moe_gate_w, moe_up_w, j)
            y_sorted = _moe_down(items, hid, moe_down_w, j)
            if i + 1 < L:
                xs, h_next = _combine(pos, y_sorted, xs, mod, i, top_w, mix_g)
            else:
                xs = _combine(pos, y_sorted, xs, mod, i, top_w)
    return xs.reshape(B, S, D)
```

```python
import functools

import jax
import jax.numpy as jnp
from jax import lax
from jax.experimental import pallas as pl
from jax.experimental.pallas import tpu as pltpu

F32 = jnp.float32
BF16 = jnp.bfloat16

NORM_EPS = 1e-6
HEAD_DIM = 128
CONV_WIDTH = 31
CONV_HALO = 32
MOE_TOP_K = 2
LOG2E = 1.4426950408889634
MASK_VALUE = -1e30
ATTN_FAST_MAX_BOUND = 40.0
LANES = 128
SUBLANES = 8
BIG_VMEM_MB = 57
DENSE_TM = 2048
MOE_SUB = 256
MOE_TM = 512


def _cparams(n_axes, vmem_mb):
    return pltpu.CompilerParams(
        dimension_semantics=("arbitrary",) * n_axes,
        vmem_limit_bytes=vmem_mb << 20)


def _norm_modulate(x, g, sc, sh):
    ms = jnp.mean(x * x, axis=-1, keepdims=True)
    y = x * lax.rsqrt(ms + NORM_EPS) * g
    return y * (1.0 + sc) + sh


def _silu(x):
    return x * jax.nn.sigmoid(x)


def _adaln_kernel(c_ref, w_ref, b_ref, o_ref):
    c = c_ref[...]
    o_ref[0] = jnp.sum(w_ref[0] * _silu(c), axis=0, keepdims=True) + b_ref[0]


def _adaln(c_col, ada_w, ada_b):
    L, D, N = ada_w.shape
    tn = 1024
    return pl.pallas_call(
        _adaln_kernel,
        grid=(L, N // tn),
        in_specs=[pl.BlockSpec((D, 1), lambda l, j: (0, 0)),
                  pl.BlockSpec((1, D, tn), lambda l, j: (l, 0, j)),
                  pl.BlockSpec((1, 1, tn), lambda l, j: (l, 0, j))],
        out_specs=pl.BlockSpec((1, 1, tn), lambda l, j: (l, 0, j)),
        out_shape=jax.ShapeDtypeStruct((L, 1, N), F32),
        compiler_params=_cparams(2, 40),
        name="adaln",
    )(c_col, ada_w, ada_b.reshape(L, 1, N))


def _mod_spec(layer, which, D):
    return pl.BlockSpec((1, 1, D), lambda *_: (layer, 0, which))


def _vec_spec(layer, D):
    return pl.BlockSpec((1, 1, D), lambda *_: (layer, 0, 0))


def _norm_mod_kernel(x_ref, g_ref, sc_ref, sh_ref, o_ref):
    o_ref[...] = _norm_modulate(x_ref[...], g_ref[0], sc_ref[0], sh_ref[0]).astype(o_ref.dtype)


def _norm_mod(x, norm_g, mod, layer, which_shift):
    S, D = x.shape
    tm = 512
    return pl.pallas_call(
        _norm_mod_kernel,
        grid=(S // tm,),
        in_specs=[pl.BlockSpec((tm, D), lambda i: (i, 0)),
                  _vec_spec(layer, D),
                  _mod_spec(layer, which_shift + 1, D),
                  _mod_spec(layer, which_shift, D)],
        out_specs=pl.BlockSpec((tm, D), lambda i: (i, 0)),
        out_shape=jax.ShapeDtypeStruct((S, D), BF16),
        compiler_params=_cparams(1, 32),
        name="norm_mod",
    )(x, norm_g, mod, mod)


def _dot(a, w):
    return jnp.dot(a, w, preferred_element_type=F32)


def _pw1_glu_kernel(a_ref, wv_ref, wg_ref, bv_ref, bg_ref, o_ref):
    a = a_ref[...]
    val = _dot(a, wv_ref[...].astype(BF16)) + bv_ref[0]
    gate = _dot(a, wg_ref[...].astype(BF16)) + bg_ref[0]
    o_ref[...] = val * jax.nn.sigmoid(gate)


def _pw1_glu(h, pw1_w, pw1_b, j):
    S, D = h.shape
    tm, tn = DENSE_TM, 512
    nt = D // tn
    return pl.pallas_call(
        _pw1_glu_kernel,
        grid=(S // tm, nt),
        in_specs=[pl.BlockSpec((tm, D), lambda i, n: (i, 0)),
                  pl.BlockSpec((None, D, tn), lambda i, n: (j, 0, n)),
                  pl.BlockSpec((None, D, tn), lambda i, n: (j, 0, n + nt)),
                  pl.BlockSpec((1, 1, tn), lambda i, n: (j, 0, n)),
                  pl.BlockSpec((1, 1, tn), lambda i, n: (j, 0, n + nt))],
        out_specs=pl.BlockSpec((tm, tn), lambda i, n: (i, n)),
        out_shape=jax.ShapeDtypeStruct((S, D), F32),
        compiler_params=_cparams(2, BIG_VMEM_MB),
        name="pw1_glu",
    )(h, pw1_w, pw1_w, pw1_b, pw1_b)


def _gate_up_kernel(a_ref, wg_ref, wu_ref, o_ref):
    a = a_ref[...]
    g = _dot(a, wg_ref[...].astype(BF16))
    u = _dot(a, wu_ref[...].astype(BF16))
    o_ref[...] = (_silu(g) * u).astype(o_ref.dtype)


def _gate_up(h, gate_w, up_w, j):
    S, D = h.shape
    F = gate_w.shape[-1]
    tm, tn = DENSE_TM, 512
    return pl.pallas_call(
        _gate_up_kernel,
        grid=(S // tm, F // tn),
        in_specs=[pl.BlockSpec((tm, D), lambda i, n: (i, 0)),
                  pl.BlockSpec((None, D, tn), lambda i, n: (j, 0, n)),
                  pl.BlockSpec((None, D, tn), lambda i, n: (j, 0, n))],
        out_specs=pl.BlockSpec((tm, tn), lambda i, n: (i, n)),
        out_shape=jax.ShapeDtypeStruct((S, F), BF16),
        compiler_params=_cparams(2, BIG_VMEM_MB),
        name="ffn_gate_up",
    )(h, gate_w, up_w)


def _mm_res_kernel(a_ref, w_ref, b_ref, x_ref, g_ref, o_ref):
    y = _dot(a_ref[...], w_ref[...].astype(BF16)) + b_ref[0]
    o_ref[...] = x_ref[...] + g_ref[0] * y


def _mm_res(a, w, bias, x, mod, j, layer, which_gate, tm, tn, name):
    S, K = a.shape
    D = w.shape[-1]
    bj = j if bias.shape[0] > 1 else 0
    return pl.pallas_call(
        _mm_res_kernel,
        grid=(S // tm, D // tn),
        in_specs=[pl.BlockSpec((tm, K), lambda i, n: (i, 0)),
                  pl.BlockSpec((None, K, tn), lambda i, n: (j, 0, n)),
                  pl.BlockSpec((1, 1, tn), lambda i, n: (bj, 0, n)),
                  pl.BlockSpec((tm, tn), lambda i, n: (i, n)),
                  pl.BlockSpec((1, 1, tn), lambda i, n: (layer, 0, which_gate * (D // tn) + n))],
        out_specs=pl.BlockSpec((tm, tn), lambda i, n: (i, n)),
        out_shape=jax.ShapeDtypeStruct((S, D), F32),
        compiler_params=_cparams(2, BIG_VMEM_MB),
        name=name,
    )(a, w, bias, x, mod)


def _router_logits(h, rw, rb):
    h_hi, w_hi = h.astype(BF16), rw.astype(BF16)
    h_lo = (h - h_hi.astype(F32)).astype(BF16)
    w_lo = (rw - w_hi.astype(F32)).astype(BF16)
    return _dot(h_hi, w_hi) + (_dot(h_hi, w_lo) + _dot(h_lo, w_hi)) + rb


def _mixer_out_kernel(a_ref, w_hbm, b_ref, x_ref, gate_ref, ng_ref, nsc_ref, nsh_ref, *rest, j, router):
    if router:
        rw_ref, rb_ref, o_ref, aux_ref, w_stage, wb, sem = rest
    else:
        o_ref, aux_ref, w_stage, wb, sem = rest

    @pl.when(pl.program_id(0) == 0)
    def _():
        width = w_stage.shape[1]
        for c in range(wb.shape[1] // width):
            cols = slice(c * width, (c + 1) * width)
            cp = pltpu.make_async_copy(w_hbm.at[j, :, cols], w_stage, sem.at[0])
            cp.start()
            cp.wait()
            wb[:, cols] = w_stage[...].astype(BF16)

    y = _dot(a_ref[...], wb[...]) + b_ref[0]
    x_new = x_ref[...] + gate_ref[0] * y
    o_ref[...] = x_new
    h = _norm_modulate(x_new, ng_ref[0], nsc_ref[0], nsh_ref[0])
    if router:
        aux_ref[...] = _router_logits(h, rw_ref[...], rb_ref[...])
    else:
        aux_ref[...] = h.astype(aux_ref.dtype)


def _mixer_out(a, w, bias, x, mod, norm_g, j, layer, name, router=None):
    S, K = a.shape
    D = w.shape[-1]
    tm = 512
    bj = j if bias.shape[0] > 1 else 0
    row = lambda width: pl.BlockSpec((tm, width), lambda i: (i, 0))
    in_specs = [row(K), pl.BlockSpec(memory_space=pl.ANY), _vec_spec(bj, D), row(D), _mod_spec(layer, 2, D),
                _vec_spec(layer, D), _mod_spec(layer, 4, D), _mod_spec(layer, 3, D)]
    args = (a, w, bias, x, mod, norm_g, mod, mod)
    if router is None:
        aux_spec, aux_shape = row(D), jax.ShapeDtypeStruct((S, D), BF16)
    else:
        in_specs += [pl.BlockSpec((D, LANES), lambda i: (0, 0)), pl.BlockSpec((1, LANES), lambda i: (0, 0))]
        args += router
        aux_spec, aux_shape = row(LANES), jax.ShapeDtypeStruct((S, LANES), F32)
    return pl.pallas_call(
        functools.partial(_mixer_out_kernel, j=j, router=router is not None),
        grid=(S // tm,),
        in_specs=in_specs,
        out_specs=[row(D), aux_spec],
        out_shape=[jax.ShapeDtypeStruct((S, D), F32), aux_shape],
        scratch_shapes=[pltpu.VMEM((K, D // 2), F32), pltpu.VMEM((K, D), BF16), pltpu.SemaphoreType.DMA((1,))],
        compiler_params=_cparams(1, BIG_VMEM_MB),
        name=name,
    )(*args)


def _conv_ln_kernel(halo_ref, u_ref, w_ref, b_ref, g_ref, beta_ref, o_ref, win_ref, acc_ref, *, tm):
    i = pl.program_id(0)

    @pl.when(i == 0)
    def _():
        win_ref[0:CONV_HALO, :] = jnp.zeros((CONV_HALO, win_ref.shape[1]), F32)

    @pl.when(i > 0)
    def _():
        win_ref[0:CONV_HALO, :] = halo_ref[...]

    win_ref[CONV_HALO:CONV_HALO + tm, :] = u_ref[...]
    first = CONV_HALO - (CONV_WIDTH - 1)
    for c in range(u_ref.shape[1] // LANES):
        cs = slice(c * LANES, (c + 1) * LANES)
        out = None
        for b in range(SUBLANES):
            rows = tm if b == 0 else tm + SUBLANES
            y = None
            for j in range(first, first + CONV_WIDTH):
                if j % SUBLANES != b:
                    continue
                term = win_ref[j - b:j - b + rows, cs] * w_ref[0, j - first:j - first + 1, cs]
                y = term if y is None else y + term
            yb = y if b == 0 else y[b:b + tm]
            out = yb if out is None else out + yb
        acc_ref[:, cs] = out + b_ref[0][:, cs]
    y = acc_ref[...]
    mu = jnp.mean(y, axis=-1, keepdims=True)
    yc = y - mu
    var = jnp.mean(yc * yc, axis=-1, keepdims=True)
    z = yc * lax.rsqrt(var + NORM_EPS) * g_ref[0] + beta_ref[0]
    o_ref[...] = _silu(z).astype(o_ref.dtype)


def _conv_ln(u, dw_w, dw_b, ln_g, ln_b, j):
    S, D = u.shape
    tm = 128
    hb = tm // CONV_HALO
    return pl.pallas_call(
        functools.partial(_conv_ln_kernel, tm=tm),
        grid=(S // tm,),
        in_specs=[pl.BlockSpec((CONV_HALO, D), lambda i: (jnp.maximum(i * hb - 1, 0), 0)),
                  pl.BlockSpec((tm, D), lambda i: (i, 0)),
                  pl.BlockSpec((1, CONV_WIDTH, D), lambda i: (j, 0, 0)),
                  _vec_spec(j, D), _vec_spec(j, D), _vec_spec(j, D)],
        out_specs=pl.BlockSpec((tm, D), lambda i: (i, 0)),
        out_shape=jax.ShapeDtypeStruct((S, D), BF16),
        scratch_shapes=[pltpu.VMEM((CONV_HALO + tm, D), F32), pltpu.VMEM((tm, D), F32)],
        compiler_params=_cparams(1, 32),
        name="conv_ln",
    )(u, u, dw_w, dw_b, ln_g, ln_b)


def _qkv_kernel(a_ref, w_ref, gain_ref, o_ref, *, n_norm_tiles):
    is_norm_tile = pl.program_id(1) < n_norm_tiles
    acc = _dot(a_ref[...], w_ref[...].astype(BF16))
    gain = gain_ref[...]
    for h in range(acc.shape[1] // HEAD_DIM):
        sl = slice(h * HEAD_DIM, (h + 1) * HEAD_DIM)
        t = acc[:, sl]
        ms = jnp.mean(t * t, axis=-1, keepdims=True)
        r = jnp.where(is_norm_tile, lax.rsqrt(ms + NORM_EPS), 1.0)
        o_ref[:, sl] = (t * r * gain[:, sl]).astype(o_ref.dtype)


def _qkv(h, qkv_w, gain, j):
    S, D = h.shape
    N = qkv_w.shape[-1]
    tm, tn = DENSE_TM, 1024
    return pl.pallas_call(
        functools.partial(_qkv_kernel, n_norm_tiles=2 * D // tn),
        grid=(S // tm, N // tn),
        in_specs=[pl.BlockSpec((tm, D), lambda i, n: (i, 0)),
                  pl.BlockSpec((None, D, tn), lambda i, n: (j, 0, n)),
                  pl.BlockSpec((1, tn), lambda i, n: (0, n))],
        out_specs=pl.BlockSpec((tm, tn), lambda i, n: (i, n)),
        out_shape=jax.ShapeDtypeStruct((S, N), BF16),
        compiler_params=_cparams(2, BIG_VMEM_MB),
        name="fox_qkv",
    )(h, qkv_w, gain)


def _fgate_kernel(h_ref, fwt_ref, fb_ref, tri_ref, o_ref, carry_ref):
    i = pl.program_id(0)

    @pl.when(i == 0)
    def _():
        carry_ref[...] = jnp.zeros(carry_ref.shape, F32)

    z = lax.dot_general(fwt_ref[...].astype(BF16), h_ref[...], (((1,), (1,)), ((), ())),
                        preferred_element_type=F32) + fb_ref[...][:, 0:1]
    lf = jnp.minimum(z, 0.0) - jnp.log1p(jnp.exp(-jnp.abs(z)))
    hi = lf.astype(BF16)
    r1 = lf - hi.astype(F32)
    mid = r1.astype(BF16)
    lo = (r1 - mid.astype(F32)).astype(BF16)
    tri = tri_ref[...]
    cs = _dot(hi, tri) + _dot(mid, tri) + _dot(lo, tri) + carry_ref[...][:, 0:1]
    o_ref[...] = cs
    carry_ref[...] = jnp.broadcast_to(cs[:, cs.shape[1] - 1:], carry_ref.shape)


def _fgate_cum(h, fgate_wt, fgate_b, j):
    S, D = h.shape
    H = fgate_wt.shape[1]
    tm = 512
    tri = jnp.triu(jnp.ones((tm, tm), F32)).astype(BF16)
    fb = jnp.broadcast_to(fgate_b[j][:, None], (H, LANES))
    return pl.pallas_call(
        _fgate_kernel,
        grid=(S // tm,),
        in_specs=[pl.BlockSpec((tm, D), lambda i: (i, 0)),
                  pl.BlockSpec((None, H, D), lambda i: (j, 0, 0)),
                  pl.BlockSpec((H, LANES), lambda i: (0, 0)),
                  pl.BlockSpec((tm, tm), lambda i: (0, 0))],
        out_specs=pl.BlockSpec((H, tm), lambda i: (0, i)),
        out_shape=jax.ShapeDtypeStruct((H, S), F32),
        scratch_shapes=[pltpu.VMEM((H, LANES), F32)],
        compiler_params=_cparams(1, 32),
        name="fox_fgate",
    )(h, fgate_wt, fb, tri)


def _attn_kernel(flag_ref, q_ref, k_ref, v_ref, ck_ref, b_ref, o_ref,
                 acc_ref, m_ref, vaug_ref, rt_ref, *, tq, heads):
    qi = pl.program_id(1)
    S = k_ref.shape[0]
    hs = [slice(g * HEAD_DIM, (g + 1) * HEAD_DIM) for g in range(heads)]
    acc_ref[...] = jnp.zeros(acc_ref.shape, F32)

    @pl.when(qi == 0)
    def _():
        for g in range(heads):
            vaug_ref[g, :, 0:HEAD_DIM] = v_ref[:, hs[g]]
            vaug_ref[g, :, HEAD_DIM:] = jnp.ones((S, HEAD_DIM), BF16)

    def scores(g, ks, width):
        return lax.dot_general(q_ref[:, hs[g]], k_ref[pl.ds(ks, width), hs[g]],
                               (((1,), (1,)), ((), ())), preferred_element_type=F32)

    def visible(c):
        row = lax.broadcasted_iota(jnp.int32, (tq, LANES), 0)
        col = lax.broadcasted_iota(jnp.int32, (tq, LANES), 1) + c * LANES
        return col <= row

    def fast_step(ki, width, diag_at=None):
        ks = pl.multiple_of(ki * tq, tq)
        for g in range(heads):
            s = scores(g, ks, width)
            ckr = ck_ref[g, :, pl.ds(ks, width)]
            rt = rt_ref[g]
            ps = []
            for c in range(width // LANES):
                sl = slice(c * LANES, (c + 1) * LANES)
                t = s[:, sl] + (rt - ckr[:, sl])
                if diag_at is not None and c >= diag_at:
                    t = jnp.where(visible(c - diag_at), t, MASK_VALUE)
                ps.append(jnp.exp2(t).astype(BF16))
            p = jnp.concatenate(ps, axis=1)
            acc_ref[g] += _dot(p, vaug_ref[g, pl.ds(ks, width), :])

    def slow_step(ki, width, diag_at=None):
        ks = pl.multiple_of(ki * tq, tq)
        for g in range(heads):
            t = scores(g, ks, width) - ck_ref[g, :, pl.ds(ks, width)]
            if diag_at is not None:
                t = jnp.concatenate(
                    [t[:, c * LANES:(c + 1) * LANES] if c < diag_at else
                     jnp.where(visible(c - diag_at), t[:, c * LANES:(c + 1) * LANES], MASK_VALUE)
                     for c in range(width // LANES)], axis=1)
            m_prev = m_ref[g]
            m_new = jnp.maximum(m_prev, jnp.max(t, axis=-1, keepdims=True))
            alpha = jnp.exp2(m_prev - m_new)
            p = jnp.exp2(t - m_new[:, 0:1])
            pv = _dot(p.astype(BF16), vaug_ref[g, pl.ds(ks, width), :])
            acc_ref[g] = acc_ref[g] * jnp.concatenate([alpha, alpha], axis=1) + pv
            m_ref[g] = m_new

    def sweep(step):
        def body(kk, carry):
            step(4 * kk, 4 * tq)
            return carry
        lax.fori_loop(0, qi // 4, body, 0)
        rem = qi % 4

        @pl.when(rem >= 2)
        def _():
            step(qi - rem, 2 * tq)

        @pl.when(rem % 2 == 1)
        def _():
            step(qi - 1, 2 * tq, diag_at=tq // LANES)

        @pl.when(rem % 2 == 0)
        def _():
            step(qi, tq, diag_at=0)

    @pl.when(flag_ref[0] == 1)
    def _():
        qs = pl.multiple_of(qi * tq, tq)
        for g in range(heads):
            ckq = jnp.broadcast_to(ck_ref[g, :, pl.ds(qs, tq)], (LANES, tq))
            rt_ref[g] = ckq.T - b_ref[...]
        sweep(fast_step)

    @pl.when(flag_ref[0] != 1)
    def _():
        m_ref[...] = jnp.full(m_ref.shape, MASK_VALUE, F32)
        sweep(slow_step)

    for g in range(heads):
        acc = acc_ref[g]
        o_ref[:, hs[g]] = (acc[:, 0:HEAD_DIM] / acc[:, HEAD_DIM:]).astype(o_ref.dtype)


def _attention(qkv, ck2, bound):
    S = qkv.shape[0]
    H = ck2.shape[0]
    tq, heads = 512, 2
    hw = heads * HEAD_DIM
    flag = (bound <= ATTN_FAST_MAX_BOUND).astype(jnp.int32).reshape(1)
    bvec = jnp.broadcast_to(bound.astype(F32), (1, LANES))
    grid_spec = pltpu.PrefetchScalarGridSpec(
        num_scalar_prefetch=1,
        grid=(H // heads, S // tq),
        in_specs=[pl.BlockSpec((tq, hw), lambda h, i, f: (i, h)),
                  pl.BlockSpec((S, hw), lambda h, i, f: (0, H // heads + h)),
                  pl.BlockSpec((S, hw), lambda h, i, f: (0, 2 * (H // heads) + h)),
                  pl.BlockSpec((heads, 1, S), lambda h, i, f: (h, 0, 0)),
                  pl.BlockSpec((1, LANES), lambda h, i, f: (0, 0))],
        out_specs=pl.BlockSpec((tq, hw), lambda h, i, f: (i, h)),
        scratch_shapes=[pltpu.VMEM((heads, tq, 2 * HEAD_DIM), F32),
                        pltpu.VMEM((heads, tq, HEAD_DIM), F32),
                        pltpu.VMEM((heads, S, 2 * HEAD_DIM), BF16),
                        pltpu.VMEM((heads, tq, LANES), F32)])
    return pl.pallas_call(
        functools.partial(_attn_kernel, tq=tq, heads=heads),
        grid_spec=grid_spec,
        out_shape=jax.ShapeDtypeStruct((S, H * HEAD_DIM), BF16),
        compiler_params=_cparams(2, BIG_VMEM_MB),
        name="fox_attn",
    )(flag, qkv, qkv, qkv, ck2, bvec)


def _row_copy(src_hbm, dst, src_row, dst_row, sem):
    return pltpu.make_async_copy(src_hbm.at[pl.ds(src_row, 1), :], dst.at[pl.ds(dst_row, 1), :], sem)


def _gathered_rows(idx_ref, src_hbm, buf, sem, tm, n_per_row):
    i = pl.program_id(0)

    def start_copies(step, slot):
        def body(r, c):
            for k in range(n_per_row):
                src_row = idx_ref[(step * tm + r) * n_per_row + k]
                _row_copy(src_hbm, buf.at[slot, k], src_row, r, sem.at[slot]).start()
            return c
        lax.fori_loop(0, tm, body, 0, unroll=8)

    @pl.when(i == 0)
    def _():
        start_copies(0, 0)

    @pl.when(i + 1 < pl.num_programs(0))
    def _():
        start_copies(i + 1, (i + 1) % 2)

    slot = i % 2
    for k in range(n_per_row):
        pltpu.make_async_copy(src_hbm.at[pl.ds(0, tm), :], buf.at[slot, k], sem.at[slot]).wait()
    return slot


def _gather_norm_kernel(tok_ref, x_hbm, g_ref, sc_ref, sh_ref, o_ref, buf, sem, *, tm):
    slot = _gathered_rows(tok_ref, x_hbm, buf, sem, tm, 1)
    o_ref[...] = _norm_modulate(buf[slot, 0], g_ref[0], sc_ref[0], sh_ref[0]).astype(o_ref.dtype)


def _gather_norm(tok_sorted, x, norm_g, mod, layer):
    S, D = x.shape
    n_rows = tok_sorted.shape[0]
    tm = 256
    grid_spec = pltpu.PrefetchScalarGridSpec(
        num_scalar_prefetch=1,
        grid=(n_rows // tm,),
        in_specs=[pl.BlockSpec(memory_space=pl.ANY),
                  pl.BlockSpec((1, 1, D), lambda i, t: (layer, 0, 0)),
                  pl.BlockSpec((1, 1, D), lambda i, t: (layer, 0, 4)),
                  pl.BlockSpec((1, 1, D), lambda i, t: (layer, 0, 3))],
        out_specs=pl.BlockSpec((tm, D), lambda i, t: (i, 0)),
        scratch_shapes=[pltpu.VMEM((2, 1, tm, D), F32), pltpu.SemaphoreType.DMA((2,))])
    return pl.pallas_call(
        functools.partial(_gather_norm_kernel, tm=tm),
        grid_spec=grid_spec,
        out_shape=jax.ShapeDtypeStruct((n_rows, D), BF16),
        compiler_params=_cparams(1, 32),
        name="moe_gather",
    )(tok_sorted, x, norm_g, mod, mod)


def _grouped_matmul_item(item_refs, a_ref, o_ref, w_hbms, stages, caches, sems, layer, product):
    tile_ref, exp_ref, kind_ref, lo_ref, hi_ref, nxt_ref, lastrun_ref = item_refs
    n, i = pl.program_id(0), pl.program_id(1)
    tm, tn = o_ref.shape
    tile, lo, hi, kind = tile_ref[i], lo_ref[i], hi_ref[i], kind_ref[i]
    run_start = (i == 0) | (exp_ref[i] != exp_ref[jnp.maximum(i - 1, 0)])
    sub_blocks = [(slice(sb * MOE_SUB, (sb + 1) * MOE_SUB), tile * tm + sb * MOE_SUB) for sb in range(tm // MOE_SUB)]

    def copy(k, e, col_tile):
        cols = pl.ds(pl.multiple_of(col_tile * tn, tn), tn)
        return pltpu.make_async_copy(w_hbms[k].at[layer, e, :, cols], stages[k], sems.at[k])

    @pl.when((n == 0) & (i == 0))
    def _():
        for k in range(len(w_hbms)):
            copy(k, exp_ref[0], 0).start()

    @pl.when(run_start)
    def _():
        for k in range(len(w_hbms)):
            copy(k, 0, 0).wait()

    def rows(rs, fresh):
        if fresh:
            ws = [stage[...].astype(BF16) for stage in stages]
            for cache, w in zip(caches, ws):
                cache[...] = w
        else:
            ws = [cache[...] for cache in caches]
        return product(a_ref[rs, :], ws).astype(o_ref.dtype)

    whole = (lo <= tile * tm) & (hi >= (tile + 1) * tm)
    for fresh in (True, False):
        mine = (kind == 1) & (run_start if fresh else jnp.logical_not(run_start))

        @pl.when(mine & whole)
        def _(fresh=fresh):
            o_ref[...] = rows(slice(0, tm), fresh)

        for rs, start in sub_blocks:
            @pl.when(mine & jnp.logical_not(whole) & (lo <= start) & (start < hi))
            def _(rs=rs, fresh=fresh):
                o_ref[rs, :] = rows(rs, fresh)

    for rs, start in sub_blocks:
        @pl.when((kind == 2) & (lo <= start) & (start < hi))
        def _(rs=rs):
            o_ref[rs, :] = jnp.zeros((MOE_SUB, tn), o_ref.dtype)

    @pl.when(run_start)
    def _():
        nn = jnp.where(lastrun_ref[i] == 1, n + 1, n)

        @pl.when(nn < pl.num_programs(0))
        def _():
            for k in range(len(w_hbms)):
                copy(k, nxt_ref[i], nn).start()


def _moe_gu_kernel(tile_ref, exp_ref, kind_ref, lo_ref, hi_ref, nxt_ref, lastrun_ref, in_tile_ref,
                   a_ref, wg_hbm, wu_hbm, o_ref, wg_stage, wu_stage, wgb, wub, sems, *, layer):
    _grouped_matmul_item((tile_ref, exp_ref, kind_ref, lo_ref, hi_ref, nxt_ref, lastrun_ref), a_ref, o_ref,
                         (wg_hbm, wu_hbm), (wg_stage, wu_stage), (wgb, wub), sems, layer,
                         lambda a, ws: _silu(_dot(a, ws[0])) * _dot(a, ws[1]))


def _moe_gate_up(items, xs, gate_w, up_w, j):
    n_rows, D = xs.shape
    F = gate_w.shape[-1]
    tm, tn = MOE_TM, 1024
    n_items = items[0].shape[0]
    grid_spec = pltpu.PrefetchScalarGridSpec(
        num_scalar_prefetch=len(items),
        grid=(F // tn, n_items),
        in_specs=[pl.BlockSpec((tm, D), lambda n, i, *refs: (refs[-1][i], 0)),
                  pl.BlockSpec(memory_space=pl.ANY), pl.BlockSpec(memory_space=pl.ANY)],
        out_specs=pl.BlockSpec((tm, tn), lambda n, i, tile, *_: (tile[i], n)),
        scratch_shapes=[pltpu.VMEM((D, tn), F32), pltpu.VMEM((D, tn), F32),
                        pltpu.VMEM((D, tn), BF16), pltpu.VMEM((D, tn), BF16),
                        pltpu.SemaphoreType.DMA((2,))])
    return pl.pallas_call(
        functools.partial(_moe_gu_kernel, layer=j),
        grid_spec=grid_spec,
        out_shape=jax.ShapeDtypeStruct((n_rows, F), BF16),
        compiler_params=_cparams(2, BIG_VMEM_MB),
        name="moe_gate_up",
    )(*items, xs, gate_w, up_w)


def _moe_down_kernel(tile_ref, exp_ref, kind_ref, lo_ref, hi_ref, nxt_ref, lastrun_ref, in_tile_ref,
                     a_ref, w_hbm, o_ref, w_stage, wb, sems, *, layer):
    _grouped_matmul_item((tile_ref, exp_ref, kind_ref, lo_ref, hi_ref, nxt_ref, lastrun_ref), a_ref, o_ref,
                         (w_hbm,), (w_stage,), (wb,), sems, layer, lambda a, ws: _dot(a, ws[0]))


def _moe_down(items, hid, down_w, j):
    n_rows, F = hid.shape
    D = down_w.shape[-1]
    tm, tn = MOE_TM, 512
    n_items = items[0].shape[0]
    grid_spec = pltpu.PrefetchScalarGridSpec(
        num_scalar_prefetch=len(items),
        grid=(D // tn, n_items),
        in_specs=[pl.BlockSpec((tm, F), lambda n, i, *refs: (refs[-1][i], 0)),
                  pl.BlockSpec(memory_space=pl.ANY)],
        out_specs=pl.BlockSpec((tm, tn), lambda n, i, tile, *_: (tile[i], n)),
        scratch_shapes=[pltpu.VMEM((F, tn), F32), pltpu.VMEM((F, tn), BF16), pltpu.SemaphoreType.DMA((1,))])
    return pl.pallas_call(
        functools.partial(_moe_down_kernel, layer=j),
        grid_spec=grid_spec,
        out_shape=jax.ShapeDtypeStruct((n_rows, D), F32),
        compiler_params=_cparams(2, BIG_VMEM_MB),
        name="moe_down",
    )(*items, hid, down_w)


def _combined_rows(pos_ref, y_hbm, x_ref, g_ref, w_ref, buf, sem, tm):
    slot = _gathered_rows(pos_ref, y_hbm, buf, sem, tm, MOE_TOP_K)
    w = w_ref[...]
    y = buf[slot, 0] * w[:, 0:1] + buf[slot, 1] * w[:, 1:2]
    return x_ref[...] + g_ref[0] * y


def _combine_kernel(pos_ref, y_hbm, x_ref, g_ref, w_ref, o_ref, buf, sem, *, tm):
    o_ref[...] = _combined_rows(pos_ref, y_hbm, x_ref, g_ref, w_ref, buf, sem, tm)


def _combine_norm_kernel(pos_ref, y_hbm, x_ref, g_ref, w_ref, ng_ref, nsc_ref, nsh_ref, o_ref, h_ref, buf, sem,
                         *, tm):
    x_new = _combined_rows(pos_ref, y_hbm, x_ref, g_ref, w_ref, buf, sem, tm)
    o_ref[...] = x_new
    h_ref[...] = _norm_modulate(x_new, ng_ref[0], nsc_ref[0], nsh_ref[0]).astype(h_ref.dtype)


def _combine(pos, y_sorted, x, mod, layer, top_w, next_norm_g=None):
    S, D = x.shape
    tm = 256
    row_spec = pl.BlockSpec((tm, D), lambda i, p: (i, 0))
    in_specs = [pl.BlockSpec(memory_space=pl.ANY),
                row_spec,
                pl.BlockSpec((1, 1, D), lambda i, p: (layer, 0, 5)),
                pl.BlockSpec((tm, MOE_TOP_K), lambda i, p: (i, 0))]
    args = (pos, y_sorted, x, mod, top_w)
    out_specs, out_shape, body = row_spec, jax.ShapeDtypeStruct((S, D), F32), _combine_kernel
    if next_norm_g is not None:
        in_specs += [pl.BlockSpec((1, 1, D), lambda i, p: (layer + 1, 0, 0)),
                     pl.BlockSpec((1, 1, D), lambda i, p: (layer + 1, 0, 1)),
                     pl.BlockSpec((1, 1, D), lambda i, p: (layer + 1, 0, 0))]
        args += (next_norm_g, mod, mod)
        out_specs = [row_spec, row_spec]
        out_shape = [out_shape, jax.ShapeDtypeStruct((S, D), BF16)]
        body = _combine_norm_kernel
    grid_spec = pltpu.PrefetchScalarGridSpec(
        num_scalar_prefetch=1,
        grid=(S // tm,),
        in_specs=in_specs,
        out_specs=out_specs,
        scratch_shapes=[pltpu.VMEM((2, MOE_TOP_K, tm, D), F32), pltpu.SemaphoreType.DMA((2,))])
    return pl.pallas_call(
        functools.partial(body, tm=tm),
        grid_spec=grid_spec,
        out_shape=out_shape,
        compiler_params=_cparams(1, 40),
        name="moe_combine",
    )(*args)


def _work_items(starts, ends, counts, n_rows, tm):
    n_experts = counts.shape[0]
    starts = jnp.concatenate([starts, ends[-1:]])
    ends = jnp.concatenate([ends, jnp.full((1,), n_rows, ends.dtype)])
    counts = jnp.concatenate([counts, n_rows - ends[-2:-1]])
    n_items = n_rows // tm + n_experts
    first_tile = starts // tm
    n_e = jnp.where(counts > 0, (ends + tm - 1) // tm - first_tile, 0)
    item_end = jnp.cumsum(n_e)
    item_start = item_end - n_e
    total = item_end[-1]
    idx = jnp.arange(n_items, dtype=jnp.int32)
    idx_c = jnp.minimum(idx, total - 1)
    item_r = jnp.minimum(jnp.sum((idx_c[:, None] >= item_end[None, :]).astype(jnp.int32), axis=1), n_experts)
    item_tile = (first_tile[item_r] + idx_c - item_start[item_r]).astype(jnp.int32)
    kind = jnp.where(idx < total, jnp.where(item_r == n_experts, 2, 1), 0).astype(jnp.int32)
    last_expert = jnp.max(jnp.where(counts[:n_experts] > 0, jnp.arange(n_experts, dtype=jnp.int32), 0))
    item_e = jnp.where(item_r == n_experts, last_expert, item_r).astype(jnp.int32)
    in_tile = jnp.where(item_r == n_experts, (ends[n_experts - 1] - 1) // tm, item_tile).astype(jnp.int32)
    run_start = jnp.concatenate([jnp.ones((1,), bool), item_e[1:] != item_e[:-1]])
    next_start = lax.cummin(jnp.where(run_start, idx, n_items), reverse=True)
    next_start = jnp.concatenate([next_start[1:], jnp.full((1,), n_items, jnp.int32)])
    last_run = next_start >= n_items
    next_e = jnp.where(last_run, item_e[0], item_e[jnp.minimum(next_start, n_items - 1)])
    return (item_tile, item_e, kind, starts[item_r].astype(jnp.int32), ends[item_r].astype(jnp.int32),
            next_e.astype(jnp.int32), last_run.astype(jnp.int32), in_tile)


def _route(logits, n_experts):
    S = logits.shape[0]
    top_logit, top_idx = lax.top_k(logits, MOE_TOP_K)
    top_w = jax.nn.softmax(top_logit, axis=-1)
    n_assign = S * MOE_TOP_K
    n_rows = n_assign + n_experts * MOE_SUB
    flat_e = top_idx.reshape(n_assign).astype(jnp.int32)
    flat_idx = jnp.arange(n_assign, dtype=jnp.int32)
    counts = jnp.sum((flat_e[:, None] == jnp.arange(n_experts, dtype=jnp.int32)[None, :]).astype(jnp.int32), axis=0)
    dense_starts = jnp.cumsum(counts) - counts
    padded = (counts + MOE_SUB - 1) // MOE_SUB * MOE_SUB
    ends = jnp.cumsum(padded)
    starts = ends - padded
    order = jnp.sort(flat_e * n_assign + flat_idx) % n_assign
    _, dense_row = lax.sort((order, flat_idx), num_keys=1)
    pos = (starts[flat_e] + dense_row - dense_starts[flat_e]).astype(jnp.int32)
    tok_dense = order // MOE_TOP_K
    row = jnp.arange(n_rows, dtype=jnp.int32)
    row_e = jnp.minimum(jnp.sum((row[:, None] >= ends[None, :]).astype(jnp.int32), axis=1), n_experts - 1)
    offset = row - starts[row_e]
    is_real = offset < counts[row_e]
    tok_rows = jnp.where(is_real, tok_dense[jnp.clip(dense_starts[row_e] + offset, 0, n_assign - 1)], row % S)
    return tok_rows.astype(jnp.int32), pos, top_w, (starts, ends, padded)


def kernel(x, c, ada_w, ada_b, mix_norm, ffn_norm, conv_pw1_w, conv_pw1_b, conv_dw_w, conv_dw_b, conv_ln_g, conv_ln_b, conv_pw2_w, conv_pw2_b, fox_qkv_w, fox_o_w, fox_fgate_w, fox_fgate_b, fox_q_norm, fox_k_norm, ffn_gate_w, ffn_up_w, ffn_down_w, moe_router_w, moe_router_b, moe_gate_w, moe_up_w, moe_down_w):
    B, S, D = x.shape
    assert B == 1, "kernels are written for a single sequence"
    L = ada_w.shape[0]
    H = fox_fgate_w.shape[-1]
    E = moe_router_w.shape[-1]

    def as_rows(v):
        return v.reshape(v.shape[0], 1, v.shape[1])

    mod = _adaln(c.reshape(D, 1), ada_w, ada_b)
    mix_g, ffn_g = as_rows(mix_norm), as_rows(ffn_norm)
    zero_bias = jnp.zeros((1, 1, D), F32)
    xs = x.reshape(S, D)

    h_next = None
    for i in range(L):
        j = i // 2
        h = _norm_mod(xs, mix_g, mod, i, 0) if h_next is None else h_next
        h_next = None
        if i % 2 == 0:
            u = _pw1_glu(h, conv_pw1_w, as_rows(conv_pw1_b), j)
            v = _conv_ln(u, conv_dw_w, as_rows(conv_dw_b), as_rows(conv_ln_g), as_rows(conv_ln_b), j)
            xs, h = _mixer_out(v, conv_pw2_w, as_rows(conv_pw2_b), xs, mod, ffn_g, j, i, "conv_pw2")
            hid = _gate_up(h, ffn_gate_w, ffn_up_w, j)
            xs = _mm_res(hid, ffn_down_w, zero_bias, xs, mod, j, i, 5, 1024, 256, "ffn_down")
        else:
            q_gain = fox_q_norm[j] * (HEAD_DIM ** -0.5 * LOG2E)
            gain = jnp.concatenate([jnp.tile(q_gain, H), jnp.tile(fox_k_norm[j], H),
                                    jnp.ones((D,), F32)])[None, :]
            qkv = _qkv(h, fox_qkv_w, gain, j)
            cum = _fgate_cum(h, jnp.swapaxes(fox_fgate_w, 1, 2), fox_fgate_b, j)
            bound = 1.02 * HEAD_DIM * jnp.max(jnp.abs(q_gain)) * jnp.max(jnp.abs(fox_k_norm[j]))
            o = _attention(qkv, (cum * LOG2E).reshape(H, 1, S), bound)
            rw_pad = jnp.zeros((D, LANES), F32).at[:, :E].set(moe_router_w[j])
            rb_pad = jnp.zeros((1, LANES), F32).at[0, :E].set(moe_router_b[j])
            xs, logits = _mixer_out(o, fox_o_w, zero_bias, xs, mod, ffn_g, j, i, "fox_o", router=(rw_pad, rb_pad))
            tok_rows, pos, top_w, ranges = _route(logits[:, :E], E)
            items = _work_items(*ranges, tok_rows.shape[0], MOE_TM)
            xg = _gather_norm(tok_rows, xs, ffn_g, mod, i)
            hid = _moe_gate_up(items, xg, moe_gate_w, moe_up_w, j)
            y_sorted = _moe_down(items, hid, moe_down_w, j)
            if i + 1 < L:
                xs, h_next = _combine(pos, y_sorted, xs, mod, i, top_w, mix_g)
            else:
                xs = _combine(pos, y_sorted, xs, mod, i, top_w)
    return xs.reshape(B, S, D)
```

```python
import functools

import jax
import jax.numpy as jnp
from jax import lax
from jax.experimental import pallas as pl
from jax.experimental.pallas import tpu as pltpu

F32 = jnp.float32
BF16 = jnp.bfloat16

NORM_EPS = 1e-6
HEAD_DIM = 128
CONV_WIDTH = 31
CONV_HALO = 32
MOE_TOP_K = 2
LOG2E = 1.4426950408889634
MASK_VALUE = -1e30
ATTN_FAST_MAX_BOUND = 40.0
LANES = 128
SUBLANES = 8
BIG_VMEM_MB = 57
DENSE_TM = 2048
MOE_SUB = 256
MOE_TM = 512


def _cparams(n_axes, vmem_mb):
    return pltpu.CompilerParams(
        dimension_semantics=("arbitrary",) * n_axes,
        vmem_limit_bytes=vmem_mb << 20)


def _norm_modulate(x, g, sc, sh):
    ms = jnp.mean(x * x, axis=-1, keepdims=True)
    y = x * lax.rsqrt(ms + NORM_EPS) * g
    return y * (1.0 + sc) + sh


def _silu(x):
    return x * jax.nn.sigmoid(x)


def _adaln_kernel(c_ref, w_ref, b_ref, o_ref):
    c = c_ref[...]
    o_ref[0] = jnp.sum(w_ref[0] * _silu(c), axis=0, keepdims=True) + b_ref[0]


def _adaln(c_col, ada_w, ada_b):
    L, D, N = ada_w.shape
    tn = 1024
    return pl.pallas_call(
        _adaln_kernel,
        grid=(L, N // tn),
        in_specs=[pl.BlockSpec((D, 1), lambda l, j: (0, 0)),
                  pl.BlockSpec((1, D, tn), lambda l, j: (l, 0, j)),
                  pl.BlockSpec((1, 1, tn), lambda l, j: (l, 0, j))],
        out_specs=pl.BlockSpec((1, 1, tn), lambda l, j: (l, 0, j)),
        out_shape=jax.ShapeDtypeStruct((L, 1, N), F32),
        compiler_params=_cparams(2, 40),
        name="adaln",
    )(c_col, ada_w, ada_b.reshape(L, 1, N))


def _mod_spec(layer, which, D):
    return pl.BlockSpec((1, 1, D), lambda *_: (layer, 0, which))


def _vec_spec(layer, D):
    return pl.BlockSpec((1, 1, D), lambda *_: (layer, 0, 0))


def _norm_mod_kernel(x_ref, g_ref, sc_ref, sh_ref, o_ref):
    o_ref[...] = _norm_modulate(x_ref[...], g_ref[0], sc_ref[0], sh_ref[0]).astype(o_ref.dtype)


def _norm_mod(x, norm_g, mod, layer, which_shift):
    S, D = x.shape
    tm = 512
    return pl.pallas_call(
        _norm_mod_kernel,
        grid=(S // tm,),
        in_specs=[pl.BlockSpec((tm, D), lambda i: (i, 0)),
                  _vec_spec(layer, D),
                  _mod_spec(layer, which_shift + 1, D),
                  _mod_spec(layer, which_shift, D)],
        out_specs=pl.BlockSpec((tm, D), lambda i: (i, 0)),
        out_shape=jax.ShapeDtypeStruct((S, D), BF16),
        compiler_params=_cparams(1, 32),
        name="norm_mod",
    )(x, norm_g, mod, mod)


def _dot(a, w):
    return jnp.dot(a, w, preferred_element_type=F32)


def _pw1_glu_kernel(a_ref, wv_ref, wg_ref, bv_ref, bg_ref, o_ref):
    a = a_ref[...]
    val = _dot(a, wv_ref[...].astype(BF16)) + bv_ref[0]
    gate = _dot(a, wg_ref[...].astype(BF16)) + bg_ref[0]
    o_ref[...] = val * jax.nn.sigmoid(gate)


def _pw1_glu(h, pw1_w, pw1_b, j):
    S, D = h.shape
    tm, tn = DENSE_TM, 512
    nt = D // tn
    return pl.pallas_call(
        _pw1_glu_kernel,
        grid=(S // tm, nt),
        in_specs=[pl.BlockSpec((tm, D), lambda i, n: (i, 0)),
                  pl.BlockSpec((None, D, tn), lambda i, n: (j, 0, n)),
                  pl.BlockSpec((None, D, tn), lambda i, n: (j, 0, n + nt)),
                  pl.BlockSpec((1, 1, tn), lambda i, n: (j, 0, n)),
                  pl.BlockSpec((1, 1, tn), lambda i, n: (j, 0, n + nt))],
        out_specs=pl.BlockSpec((tm, tn), lambda i, n: (i, n)),
        out_shape=jax.ShapeDtypeStruct((S, D), F32),
        compiler_params=_cparams(2, BIG_VMEM_MB),
        name="pw1_glu",
    )(h, pw1_w, pw1_w, pw1_b, pw1_b)


def _gate_up_kernel(a_ref, wg_ref, wu_ref, o_ref):
    a = a_ref[...]
    g = _dot(a, wg_ref[...].astype(BF16))
    u = _dot(a, wu_ref[...].astype(BF16))
    o_ref[...] = (_silu(g) * u).astype(o_ref.dtype)


def _gate_up(h, gate_w, up_w, j):
    S, D = h.shape
    F = gate_w.shape[-1]
    tm, tn = DENSE_TM, 512
    return pl.pallas_call(
        _gate_up_kernel,
        grid=(S // tm, F // tn),
        in_specs=[pl.BlockSpec((tm, D), lambda i, n: (i, 0)),
                  pl.BlockSpec((None, D, tn), lambda i, n: (j, 0, n)),
                  pl.BlockSpec((None, D, tn), lambda i, n: (j, 0, n))],
        out_specs=pl.BlockSpec((tm, tn), lambda i, n: (i, n)),
        out_shape=jax.ShapeDtypeStruct((S, F), BF16),
        compiler_params=_cparams(2, BIG_VMEM_MB),
        name="ffn_gate_up",
    )(h, gate_w, up_w)


def _mm_res_kernel(a_ref, w_ref, b_ref, x_ref, g_ref, o_ref):
    y = _dot(a_ref[...], w_ref[...].astype(BF16)) + b_ref[0]
    o_ref[...] = x_ref[...] + g_ref[0] * y


def _mm_res(a, w, bias, x, mod, j, layer, which_gate, tm, tn, name):
    S, K = a.shape
    D = w.shape[-1]
    bj = j if bias.shape[0] > 1 else 0
    return pl.pallas_call(
        _mm_res_kernel,
        grid=(S // tm, D // tn),
        in_specs=[pl.BlockSpec((tm, K), lambda i, n: (i, 0)),
                  pl.BlockSpec((None, K, tn), lambda i, n: (j, 0, n)),
                  pl.BlockSpec((1, 1, tn), lambda i, n: (bj, 0, n)),
                  pl.BlockSpec((tm, tn), lambda i, n: (i, n)),
                  pl.BlockSpec((1, 1, tn), lambda i, n: (layer, 0, which_gate * (D // tn) + n))],
        out_specs=pl.BlockSpec((tm, tn), lambda i, n: (i, n)),
        out_shape=jax.ShapeDtypeStruct((S, D), F32),
        compiler_params=_cparams(2, BIG_VMEM_MB),
        name=name,
    )(a, w, bias, x, mod)


def _router_logits(h, rw, rb):
    h_hi, w_hi = h.astype(BF16), rw.astype(BF16)
    h_lo = (h - h_hi.astype(F32)).astype(BF16)
    w_lo = (rw - w_hi.astype(F32)).astype(BF16)
    return _dot(h_hi, w_hi) + (_dot(h_hi, w_lo) + _dot(h_lo, w_hi)) + rb


def _mixer_out_kernel(a_ref, w_hbm, b_ref, x_ref, gate_ref, ng_ref, nsc_ref, nsh_ref, *rest, j, router):
    if router:
        rw_ref, rb_ref, o_ref, aux_ref, w_stage, wb, sem = rest
    else:
        o_ref, aux_ref, w_stage, wb, sem = rest

    @pl.when(pl.program_id(0) == 0)
    def _():
        width = w_stage.shape[1]
        for c in range(wb.shape[1] // width):
            cols = slice(c * width, (c + 1) * width)
            cp = pltpu.make_async_copy(w_hbm.at[j, :, cols], w_stage, sem.at[0])
            cp.start()
            cp.wait()
            wb[:, cols] = w_stage[...].astype(BF16)

    y = _dot(a_ref[...], wb[...]) + b_ref[0]
    x_new = x_ref[...] + gate_ref[0] * y
    o_ref[...] = x_new
    h = _norm_modulate(x_new, ng_ref[0], nsc_ref[0], nsh_ref[0])
    if router:
        aux_ref[...] = _router_logits(h, rw_ref[...], rb_ref[...])
    else:
        aux_ref[...] = h.astype(aux_ref.dtype)


def _mixer_out(a, w, bias, x, mod, norm_g, j, layer, name, router=None):
    S, K = a.shape
    D = w.shape[-1]
    tm = 512
    bj = j if bias.shape[0] > 1 else 0
    row = lambda width: pl.BlockSpec((tm, width), lambda i: (i, 0))
    in_specs = [row(K), pl.BlockSpec(memory_space=pl.ANY), _vec_spec(bj, D), row(D), _mod_spec(layer, 2, D),
                _vec_spec(layer, D), _mod_spec(layer, 4, D), _mod_spec(layer, 3, D)]
    args = (a, w, bias, x, mod, norm_g, mod, mod)
    if router is None:
        aux_spec, aux_shape = row(D), jax.ShapeDtypeStruct((S, D), BF16)
    else:
        in_specs += [pl.BlockSpec((D, LANES), lambda i: (0, 0)), pl.BlockSpec((1, LANES), lambda i: (0, 0))]
        args += router
        aux_spec, aux_shape = row(LANES), jax.ShapeDtypeStruct((S, LANES), F32)
    return pl.pallas_call(
        functools.partial(_mixer_out_kernel, j=j, router=router is not None),
        grid=(S // tm,),
        in_specs=in_specs,
        out_specs=[row(D), aux_spec],
        out_shape=[jax.ShapeDtypeStruct((S, D), F32), aux_shape],
        scratch_shapes=[pltpu.VMEM((K, D // 2), F32), pltpu.VMEM((K, D), BF16), pltpu.SemaphoreType.DMA((1,))],
        compiler_params=_cparams(1, BIG_VMEM_MB),
        name=name,
    )(*args)


def _conv_ln_kernel(halo_ref, u_ref, w_ref, b_ref, g_ref, beta_ref, o_ref, win_ref, acc_ref, *, tm):
    i = pl.program_id(0)

    @pl.when(i == 0)
    def _():
        win_ref[0:CONV_HALO, :] = jnp.zeros((CONV_HALO, win_ref.shape[1]), F32)

    @pl.when(i > 0)
    def _():
        win_ref[0:CONV_HALO, :] = halo_ref[...]

    win_ref[CONV_HALO:CONV_HALO + tm, :] = u_ref[...]
    first = CONV_HALO - (CONV_WIDTH - 1)
    for c in range(u_ref.shape[1] // LANES):
        cs = slice(c * LANES, (c + 1) * LANES)
        out = None
        for b in range(SUBLANES):
            rows = tm if b == 0 else tm + SUBLANES
            y = None
            for j in range(first, first + CONV_WIDTH):
                if j % SUBLANES != b:
                    continue
                term = win_ref[j - b:j - b + rows, cs] * w_ref[0, j - first:j - first + 1, cs]
                y = term if y is None else y + term
            yb = y if b == 0 else y[b:b + tm]
            out = yb if out is None else out + yb
        acc_ref[:, cs] = out + b_ref[0][:, cs]
    y = acc_ref[...]
    mu = jnp.mean(y, axis=-1, keepdims=True)
    yc = y - mu
    var = jnp.mean(yc * yc, axis=-1, keepdims=True)
    z = yc * lax.rsqrt(var + NORM_EPS) * g_ref[0] + beta_ref[0]
    o_ref[...] = _silu(z).astype(o_ref.dtype)


def _conv_ln(u, dw_w, dw_b, ln_g, ln_b, j):
    S, D = u.shape
    tm = 128
    hb = tm // CONV_HALO
    return pl.pallas_call(
        functools.partial(_conv_ln_kernel, tm=tm),
        grid=(S // tm,),
        in_specs=[pl.BlockSpec((CONV_HALO, D), lambda i: (jnp.maximum(i * hb - 1, 0), 0)),
                  pl.BlockSpec((tm, D), lambda i: (i, 0)),
                  pl.BlockSpec((1, CONV_WIDTH, D), lambda i: (j, 0, 0)),
                  _vec_spec(j, D), _vec_spec(j, D), _vec_spec(j, D)],
        out_specs=pl.BlockSpec((tm, D), lambda i: (i, 0)),
        out_shape=jax.ShapeDtypeStruct((S, D), BF16),
        scratch_shapes=[pltpu.VMEM((CONV_HALO + tm, D), F32), pltpu.VMEM((tm, D), F32)],
        compiler_params=_cparams(1, 32),
        name="conv_ln",
    )(u, u, dw_w, dw_b, ln_g, ln_b)


def _qkv_kernel(a_ref, w_ref, gain_ref, o_ref, *, n_norm_tiles):
    is_norm_tile = pl.program_id(1) < n_norm_tiles
    acc = _dot(a_ref[...], w_ref[...].astype(BF16))
    gain = gain_ref[...]
    for h in range(acc.shape[1] // HEAD_DIM):
        sl = slice(h * HEAD_DIM, (h + 1) * HEAD_DIM)
        t = acc[:, sl]
        ms = jnp.mean(t * t, axis=-1, keepdims=True)
        r = jnp.where(is_norm_tile, lax.rsqrt(ms + NORM_EPS), 1.0)
        o_ref[:, sl] = (t * r * gain[:, sl]).astype(o_ref.dtype)


def _qkv(h, qkv_w, gain, j):
    S, D = h.shape
    N = qkv_w.shape[-1]
    tm, tn = DENSE_TM, 1024
    return pl.pallas_call(
        functools.partial(_qkv_kernel, n_norm_tiles=2 * D // tn),
        grid=(S // tm, N // tn),
        in_specs=[pl.BlockSpec((tm, D), lambda i, n: (i, 0)),
                  pl.BlockSpec((None, D, tn), lambda i, n: (j, 0, n)),
                  pl.BlockSpec((1, tn), lambda i, n: (0, n))],
        out_specs=pl.BlockSpec((tm, tn), lambda i, n: (i, n)),
        out_shape=jax.ShapeDtypeStruct((S, N), BF16),
        compiler_params=_cparams(2, BIG_VMEM_MB),
        name="fox_qkv",
    )(h, qkv_w, gain)


def _fgate_kernel(h_ref, fwt_ref, fb_ref, tri_ref, o_ref, carry_ref):
    i = pl.program_id(0)

    @pl.when(i == 0)
    def _():
        carry_ref[...] = jnp.zeros(carry_ref.shape, F32)

    z = lax.dot_general(fwt_ref[...].astype(BF16), h_ref[...], (((1,), (1,)), ((), ())),
                        preferred_element_type=F32) + fb_ref[...][:, 0:1]
    lf = jnp.minimum(z, 0.0) - jnp.log1p(jnp.exp(-jnp.abs(z)))
    hi = lf.astype(BF16)
    r1 = lf - hi.astype(F32)
    mid = r1.astype(BF16)
    lo = (r1 - mid.astype(F32)).astype(BF16)
    tri = tri_ref[...]
    cs = _dot(hi, tri) + _dot(mid, tri) + _dot(lo, tri) + carry_ref[...][:, 0:1]
    o_ref[...] = cs
    carry_ref[...] = jnp.broadcast_to(cs[:, cs.shape[1] - 1:], carry_ref.shape)


def _fgate_cum(h, fgate_wt, fgate_b, j):
    S, D = h.shape
    H = fgate_wt.shape[1]
    tm = 512
    tri = jnp.triu(jnp.ones((tm, tm), F32)).astype(BF16)
    fb = jnp.broadcast_to(fgate_b[j][:, None], (H, LANES))
    return pl.pallas_call(
        _fgate_kernel,
        grid=(S // tm,),
        in_specs=[pl.BlockSpec((tm, D), lambda i: (i, 0)),
                  pl.BlockSpec((None, H, D), lambda i: (j, 0, 0)),
                  pl.BlockSpec((H, LANES), lambda i: (0, 0)),
                  pl.BlockSpec((tm, tm), lambda i: (0, 0))],
        out_specs=pl.BlockSpec((H, tm), lambda i: (0, i)),
        out_shape=jax.ShapeDtypeStruct((H, S), F32),
        scratch_shapes=[pltpu.VMEM((H, LANES), F32)],
        compiler_params=_cparams(1, 32),
        name="fox_fgate",
    )(h, fgate_wt, fb, tri)


def _attn_kernel(flag_ref, q_ref, k_ref, v_ref, ck_ref, b_ref, o_ref,
                 acc_ref, m_ref, vaug_ref, rt_ref, *, tq, heads):
    qi = pl.program_id(1)
    S = k_ref.shape[0]
    hs = [slice(g * HEAD_DIM, (g + 1) * HEAD_DIM) for g in range(heads)]
    acc_ref[...] = jnp.zeros(acc_ref.shape, F32)

    @pl.when(qi == 0)
    def _():
        for g in range(heads):
            vaug_ref[g, :, 0:HEAD_DIM] = v_ref[:, hs[g]]
            vaug_ref[g, :, HEAD_DIM:] = jnp.ones((S, HEAD_DIM), BF16)

    def scores(g, ks, width):
        return lax.dot_general(q_ref[:, hs[g]], k_ref[pl.ds(ks, width), hs[g]],
                               (((1,), (1,)), ((), ())), preferred_element_type=F32)

    def visible(c):
        row = lax.broadcasted_iota(jnp.int32, (tq, LANES), 0)
        col = lax.broadcasted_iota(jnp.int32, (tq, LANES), 1) + c * LANES
        return col <= row

    def fast_step(ki, width, diag_at=None):
        ks = pl.multiple_of(ki * tq, tq)
        for g in range(heads):
            s = scores(g, ks, width)
            ckr = ck_ref[g, :, pl.ds(ks, width)]
            rt = rt_ref[g]
            ps = []
            for c in range(width // LANES):
                sl = slice(c * LANES, (c + 1) * LANES)
                t = s[:, sl] + (rt - ckr[:, sl])
                if diag_at is not None and c >= diag_at:
                    t = jnp.where(visible(c - diag_at), t, MASK_VALUE)
                ps.append(jnp.exp2(t).astype(BF16))
            p = jnp.concatenate(ps, axis=1)
            acc_ref[g] += _dot(p, vaug_ref[g, pl.ds(ks, width), :])

    def slow_step(ki, width, diag_at=None):
        ks = pl.multiple_of(ki * tq, tq)
        for g in range(heads):
            t = scores(g, ks, width) - ck_ref[g, :, pl.ds(ks, width)]
            if diag_at is not None:
                t = jnp.concatenate(
                    [t[:, c * LANES:(c + 1) * LANES] if c < diag_at else
                     jnp.where(visible(c - diag_at), t[:, c * LANES:(c + 1) * LANES], MASK_VALUE)
                     for c in range(width // LANES)], axis=1)
            m_prev = m_ref[g]
            m_new = jnp.maximum(m_prev, jnp.max(t, axis=-1, keepdims=True))
            alpha = jnp.exp2(m_prev - m_new)
            p = jnp.exp2(t - m_new[:, 0:1])
            pv = _dot(p.astype(BF16), vaug_ref[g, pl.ds(ks, width), :])
            acc_ref[g] = acc_ref[g] * jnp.concatenate([alpha, alpha], axis=1) + pv
            m_ref[g] = m_new

    def sweep(step):
        def body(kk, carry):
            step(4 * kk, 4 * tq)
            return carry
        lax.fori_loop(0, qi // 4, body, 0)
        rem = qi % 4

        @pl.when(rem >= 2)
        def _():
            step(qi - rem, 2 * tq)

        @pl.when(rem % 2 == 1)
        def _():
            step(qi - 1, 2 * tq, diag_at=tq // LANES)

        @pl.when(rem % 2 == 0)
        def _():
            step(qi, tq, diag_at=0)

    @pl.when(flag_ref[0] == 1)
    def _():
        qs = pl.multiple_of(qi * tq, tq)
        for g in range(heads):
            ckq = jnp.broadcast_to(ck_ref[g, :, pl.ds(qs, tq)], (LANES, tq))
            rt_ref[g] = ckq.T - b_ref[...]
        sweep(fast_step)

    @pl.when(flag_ref[0] != 1)
    def _():
        m_ref[...] = jnp.full(m_ref.shape, MASK_VALUE, F32)
        sweep(slow_step)

    for g in range(heads):
        acc = acc_ref[g]
        o_ref[:, hs[g]] = (acc[:, 0:HEAD_DIM] / acc[:, HEAD_DIM:]).astype(o_ref.dtype)


def _attention(qkv, ck2, bound):
    S = qkv.shape[0]
    H = ck2.shape[0]
    tq, heads = 512, 2
    hw = heads * HEAD_DIM
    flag = (bound <= ATTN_FAST_MAX_BOUND).astype(jnp.int32).reshape(1)
    bvec = jnp.broadcast_to(bound.astype(F32), (1, LANES))
    grid_spec = pltpu.PrefetchScalarGridSpec(
        num_scalar_prefetch=1,
        grid=(H // heads, S // tq),
        in_specs=[pl.BlockSpec((tq, hw), lambda h, i, f: (i, h)),
                  pl.BlockSpec((S, hw), lambda h, i, f: (0, H // heads + h)),
                  pl.BlockSpec((S, hw), lambda h, i, f: (0, 2 * (H // heads) + h)),
                  pl.BlockSpec((heads, 1, S), lambda h, i, f: (h, 0, 0)),
                  pl.BlockSpec((1, LANES), lambda h, i, f: (0, 0))],
        out_specs=pl.BlockSpec((tq, hw), lambda h, i, f: (i, h)),
        scratch_shapes=[pltpu.VMEM((heads, tq, 2 * HEAD_DIM), F32),
                        pltpu.VMEM((heads, tq, HEAD_DIM), F32),
                        pltpu.VMEM((heads, S, 2 * HEAD_DIM), BF16),
                        pltpu.VMEM((heads, tq, LANES), F32)])
    return pl.pallas_call(
        functools.partial(_attn_kernel, tq=tq, heads=heads),
        grid_spec=grid_spec,
        out_shape=jax.ShapeDtypeStruct((S, H * HEAD_DIM), BF16),
        compiler_params=_cparams(2, BIG_VMEM_MB),
        name="fox_attn",
    )(flag, qkv, qkv, qkv, ck2, bvec)


def _row_copy(src_hbm, dst, src_row, dst_row, sem):
    return pltpu.make_async_copy(src_hbm.at[pl.ds(src_row, 1), :], dst.at[pl.ds(dst_row, 1), :], sem)


def _gathered_rows(idx_ref, src_hbm, buf, sem, tm, n_per_row):
    i = pl.program_id(0)

    def start_copies(step, slot):
        def body(r, c):
            for k in range(n_per_row):
                src_row = idx_ref[(step * tm + r) * n_per_row + k]
                _row_copy(src_hbm, buf.at[slot, k], src_row, r, sem.at[slot]).start()
            return c
        lax.fori_loop(0, tm, body, 0, unroll=8)

    @pl.when(i == 0)
    def _():
        start_copies(0, 0)

    @pl.when(i + 1 < pl.num_programs(0))
    def _():
        start_copies(i + 1, (i + 1) % 2)

    slot = i % 2
    for k in range(n_per_row):
        pltpu.make_async_copy(src_hbm.at[pl.ds(0, tm), :], buf.at[slot, k], sem.at[slot]).wait()
    return slot


def _gather_norm_kernel(tok_ref, x_hbm, g_ref, sc_ref, sh_ref, o_ref, buf, sem, *, tm):
    slot = _gathered_rows(tok_ref, x_hbm, buf, sem, tm, 1)
    o_ref[...] = _norm_modulate(buf[slot, 0], g_ref[0], sc_ref[0], sh_ref[0]).astype(o_ref.dtype)


def _gather_norm(tok_sorted, x, norm_g, mod, layer):
    S, D = x.shape
    n_rows = tok_sorted.shape[0]
    tm = 256
    grid_spec = pltpu.PrefetchScalarGridSpec(
        num_scalar_prefetch=1,
        grid=(n_rows // tm,),
        in_specs=[pl.BlockSpec(memory_space=pl.ANY),
                  pl.BlockSpec((1, 1, D), lambda i, t: (layer, 0, 0)),
                  pl.BlockSpec((1, 1, D), lambda i, t: (layer, 0, 4)),
                  pl.BlockSpec((1, 1, D), lambda i, t: (layer, 0, 3))],
        out_specs=pl.BlockSpec((tm, D), lambda i, t: (i, 0)),
        scratch_shapes=[pltpu.VMEM((2, 1, tm, D), F32), pltpu.SemaphoreType.DMA((2,))])
    return pl.pallas_call(
        functools.partial(_gather_norm_kernel, tm=tm),
        grid_spec=grid_spec,
        out_shape=jax.ShapeDtypeStruct((n_rows, D), BF16),
        compiler_params=_cparams(1, 32),
        name="moe_gather",
    )(tok_sorted, x, norm_g, mod, mod)


def _grouped_matmul_item(item_refs, a_ref, o_ref, w_hbms, stages, caches, sems, layer, product):
    tile_ref, exp_ref, kind_ref, lo_ref, hi_ref, nxt_ref, lastrun_ref = item_refs
    n, i = pl.program_id(0), pl.program_id(1)
    tm, tn = o_ref.shape
    tile, lo, hi, kind = tile_ref[i], lo_ref[i], hi_ref[i], kind_ref[i]
    run_start = (i == 0) | (exp_ref[i] != exp_ref[jnp.maximum(i - 1, 0)])
    sub_blocks = [(slice(sb * MOE_SUB, (sb + 1) * MOE_SUB), tile * tm + sb * MOE_SUB) for sb in range(tm // MOE_SUB)]

    def copy(k, e, col_tile):
        cols = pl.ds(pl.multiple_of(col_tile * tn, tn), tn)
        return pltpu.make_async_copy(w_hbms[k].at[layer, e, :, cols], stages[k], sems.at[k])

    @pl.when((n == 0) & (i == 0))
    def _():
        for k in range(len(w_hbms)):
            copy(k, exp_ref[0], 0).start()

    @pl.when(run_start)
    def _():
        for k in range(len(w_hbms)):
            copy(k, 0, 0).wait()
            caches[k][...] = stages[k][...].astype(BF16)
        nn = jnp.where(lastrun_ref[i] == 1, n + 1, n)

        @pl.when(nn < pl.num_programs(0))
        def _():
            for k in range(len(w_hbms)):
                copy(k, nxt_ref[i], nn).start()

    def rows(rs):
        return product(a_ref[rs, :], [cache[...] for cache in caches]).astype(o_ref.dtype)

    whole = (lo <= tile * tm) & (hi >= (tile + 1) * tm)

    @pl.when((kind == 1) & whole)
    def _():
        o_ref[...] = rows(slice(0, tm))

    for rs, start in sub_blocks:
        @pl.when((kind == 1) & jnp.logical_not(whole) & (lo <= start) & (start < hi))
        def _(rs=rs):
            o_ref[rs, :] = rows(rs)

        @pl.when((kind == 2) & (lo <= start) & (start < hi))
        def _(rs=rs):
            o_ref[rs, :] = jnp.zeros((MOE_SUB, tn), o_ref.dtype)


def _moe_gu_kernel(tile_ref, exp_ref, kind_ref, lo_ref, hi_ref, nxt_ref, lastrun_ref, in_tile_ref,
                   a_ref, wg_hbm, wu_hbm, o_ref, wg_stage, wu_stage, wgb, wub, sems, *, layer):
    _grouped_matmul_item((tile_ref, exp_ref, kind_ref, lo_ref, hi_ref, nxt_ref, lastrun_ref), a_ref, o_ref,
                         (wg_hbm, wu_hbm), (wg_stage, wu_stage), (wgb, wub), sems, layer,
                         lambda a, ws: _silu(_dot(a, ws[0])) * _dot(a, ws[1]))


def _moe_gate_up(items, xs, gate_w, up_w, j):
    n_rows, D = xs.shape
    F = gate_w.shape[-1]
    tm, tn = MOE_TM, 1024
    n_items = items[0].shape[0]
    grid_spec = pltpu.PrefetchScalarGridSpec(
        num_scalar_prefetch=len(items),
        grid=(F // tn, n_items),
        in_specs=[pl.BlockSpec((tm, D), lambda n, i, *refs: (refs[-1][i], 0)),
                  pl.BlockSpec(memory_space=pl.ANY), pl.BlockSpec(memory_space=pl.ANY)],
        out_specs=pl.BlockSpec((tm, tn), lambda n, i, tile, *_: (tile[i], n)),
        scratch_shapes=[pltpu.VMEM((D, tn), F32), pltpu.VMEM((D, tn), F32),
                        pltpu.VMEM((D, tn), BF16), pltpu.VMEM((D, tn), BF16),
                        pltpu.SemaphoreType.DMA((2,))])
    return pl.pallas_call(
        functools.partial(_moe_gu_kernel, layer=j),
        grid_spec=grid_spec,
        out_shape=jax.ShapeDtypeStruct((n_rows, F), BF16),
        compiler_params=_cparams(2, BIG_VMEM_MB),
        name="moe_gate_up",
    )(*items, xs, gate_w, up_w)


def _moe_down_kernel(tile_ref, exp_ref, kind_ref, lo_ref, hi_ref, nxt_ref, lastrun_ref, in_tile_ref,
                     a_ref, w_hbm, o_ref, w_stage, wb, sems, *, layer):
    _grouped_matmul_item((tile_ref, exp_ref, kind_ref, lo_ref, hi_ref, nxt_ref, lastrun_ref), a_ref, o_ref,
                         (w_hbm,), (w_stage,), (wb,), sems, layer, lambda a, ws: _dot(a, ws[0]))


def _moe_down(items, hid, down_w, j):
    n_rows, F = hid.shape
    D = down_w.shape[-1]
    tm, tn = MOE_TM, 512
    n_items = items[0].shape[0]
    grid_spec = pltpu.PrefetchScalarGridSpec(
        num_scalar_prefetch=len(items),
        grid=(D // tn, n_items),
        in_specs=[pl.BlockSpec((tm, F), lambda n, i, *refs: (refs[-1][i], 0)),
                  pl.BlockSpec(memory_space=pl.ANY)],
        out_specs=pl.BlockSpec((tm, tn), lambda n, i, tile, *_: (tile[i], n)),
        scratch_shapes=[pltpu.VMEM((F, tn), F32), pltpu.VMEM((F, tn), BF16), pltpu.SemaphoreType.DMA((1,))])
    return pl.pallas_call(
        functools.partial(_moe_down_kernel, layer=j),
        grid_spec=grid_spec,
        out_shape=jax.ShapeDtypeStruct((n_rows, D), F32),
        compiler_params=_cparams(2, BIG_VMEM_MB),
        name="moe_down",
    )(*items, hid, down_w)


def _combined_rows(pos_ref, y_hbm, x_ref, g_ref, w_ref, buf, sem, tm):
    slot = _gathered_rows(pos_ref, y_hbm, buf, sem, tm, MOE_TOP_K)
    w = w_ref[...]
    y = buf[slot, 0] * w[:, 0:1] + buf[slot, 1] * w[:, 1:2]
    return x_ref[...] + g_ref[0] * y


def _combine_kernel(pos_ref, y_hbm, x_ref, g_ref, w_ref, o_ref, buf, sem, *, tm):
    o_ref[...] = _combined_rows(pos_ref, y_hbm, x_ref, g_ref, w_ref, buf, sem, tm)


def _combine_norm_kernel(pos_ref, y_hbm, x_ref, g_ref, w_ref, ng_ref, nsc_ref, nsh_ref, o_ref, h_ref, buf, sem,
                         *, tm):
    x_new = _combined_rows(pos_ref, y_hbm, x_ref, g_ref, w_ref, buf, sem, tm)
    o_ref[...] = x_new
    h_ref[...] = _norm_modulate(x_new, ng_ref[0], nsc_ref[0], nsh_ref[0]).astype(h_ref.dtype)


def _combine(pos, y_sorted, x, mod, layer, top_w, next_norm_g=None):
    S, D = x.shape
    tm = 256
    row_spec = pl.BlockSpec((tm, D), lambda i, p: (i, 0))
    in_specs = [pl.BlockSpec(memory_space=pl.ANY),
                row_spec,
                pl.BlockSpec((1, 1, D), lambda i, p: (layer, 0, 5)),
                pl.BlockSpec((tm, MOE_TOP_K), lambda i, p: (i, 0))]
    args = (pos, y_sorted, x, mod, top_w)
    out_specs, out_shape, body = row_spec, jax.ShapeDtypeStruct((S, D), F32), _combine_kernel
    if next_norm_g is not None:
        in_specs += [pl.BlockSpec((1, 1, D), lambda i, p: (layer + 1, 0, 0)),
                     pl.BlockSpec((1, 1, D), lambda i, p: (layer + 1, 0, 1)),
                     pl.BlockSpec((1, 1, D), lambda i, p: (layer + 1, 0, 0))]
        args += (next_norm_g, mod, mod)
        out_specs = [row_spec, row_spec]
        out_shape = [out_shape, jax.ShapeDtypeStruct((S, D), BF16)]
        body = _combine_norm_kernel
    grid_spec = pltpu.PrefetchScalarGridSpec(
        num_scalar_prefetch=1,
        grid=(S // tm,),
        in_specs=in_specs,
        out_specs=out_specs,
        scratch_shapes=[pltpu.VMEM((2, MOE_TOP_K, tm, D), F32), pltpu.SemaphoreType.DMA((2,))])
    return pl.pallas_call(
        functools.partial(body, tm=tm),
        grid_spec=grid_spec,
        out_shape=out_shape,
        compiler_params=_cparams(1, 40),
        name="moe_combine",
    )(*args)


def _work_items(starts, ends, counts, n_rows, tm):
    n_experts = counts.shape[0]
    starts = jnp.concatenate([starts, ends[-1:]])
    ends = jnp.concatenate([ends, jnp.full((1,), n_rows, ends.dtype)])
    counts = jnp.concatenate([counts, n_rows - ends[-2:-1]])
    n_items = n_rows // tm + n_experts
    first_tile = starts // tm
    n_e = jnp.where(counts > 0, (ends + tm - 1) // tm - first_tile, 0)
    item_end = jnp.cumsum(n_e)
    item_start = item_end - n_e
    total = item_end[-1]
    idx = jnp.arange(n_items, dtype=jnp.int32)
    idx_c = jnp.minimum(idx, total - 1)
    item_r = jnp.minimum(jnp.sum((idx_c[:, None] >= item_end[None, :]).astype(jnp.int32), axis=1), n_experts)
    item_tile = (first_tile[item_r] + idx_c - item_start[item_r]).astype(jnp.int32)
    kind = jnp.where(idx < total, jnp.where(item_r == n_experts, 2, 1), 0).astype(jnp.int32)
    last_expert = jnp.max(jnp.where(counts[:n_experts] > 0, jnp.arange(n_experts, dtype=jnp.int32), 0))
    item_e = jnp.where(item_r == n_experts, last_expert, item_r).astype(jnp.int32)
    in_tile = jnp.where(item_r == n_experts, (ends[n_experts - 1] - 1) // tm, item_tile).astype(jnp.int32)
    run_start = jnp.concatenate([jnp.ones((1,), bool), item_e[1:] != item_e[:-1]])
    next_start = lax.cummin(jnp.where(run_start, idx, n_items), reverse=True)
    next_start = jnp.concatenate([next_start[1:], jnp.full((1,), n_items, jnp.int32)])
    last_run = next_start >= n_items
    next_e = jnp.where(last_run, item_e[0], item_e[jnp.minimum(next_start, n_items - 1)])
    return (item_tile, item_e, kind, starts[item_r].astype(jnp.int32), ends[item_r].astype(jnp.int32),
            next_e.astype(jnp.int32), last_run.astype(jnp.int32), in_tile)


def _route(logits, n_experts):
    S = logits.shape[0]
    top_logit, top_idx = lax.top_k(logits, MOE_TOP_K)
    top_w = jax.nn.softmax(top_logit, axis=-1)
    n_assign = S * MOE_TOP_K
    n_rows = n_assign + n_experts * MOE_SUB
    flat_e = top_idx.reshape(n_assign).astype(jnp.int32)
    flat_idx = jnp.arange(n_assign, dtype=jnp.int32)
    counts = jnp.sum((flat_e[:, None] == jnp.arange(n_experts, dtype=jnp.int32)[None, :]).astype(jnp.int32), axis=0)
    dense_starts = jnp.cumsum(counts) - counts
    padded = (counts + MOE_SUB - 1) // MOE_SUB * MOE_SUB
    ends = jnp.cumsum(padded)
    starts = ends - padded
    order = jnp.sort(flat_e * n_assign + flat_idx) % n_assign
    _, dense_row = lax.sort((order, flat_idx), num_keys=1)
    pos = (starts[flat_e] + dense_row - dense_starts[flat_e]).astype(jnp.int32)
    tok_dense = order // MOE_TOP_K
    row = jnp.arange(n_rows, dtype=jnp.int32)
    row_e = jnp.minimum(jnp.sum((row[:, None] >= ends[None, :]).astype(jnp.int32), axis=1), n_experts - 1)
    offset = row - starts[row_e]
    is_real = offset < counts[row_e]
    tok_rows = jnp.where(is_real, tok_dense[jnp.clip(dense_starts[row_e] + offset, 0, n_assign - 1)], row % S)
    return tok_rows.astype(jnp.int32), pos, top_w, (starts, ends, padded)


def kernel(x, c, ada_w, ada_b, mix_norm, ffn_norm, conv_pw1_w, conv_pw1_b, conv_dw_w, conv_dw_b, conv_ln_g, conv_ln_b, conv_pw2_w, conv_pw2_b, fox_qkv_w, fox_o_w, fox_fgate_w, fox_fgate_b, fox_q_norm, fox_k_norm, ffn_gate_w, ffn_up_w, ffn_down_w, moe_router_w, moe_router_b, moe_gate_w, moe_up_w, moe_down_w):
    B, S, D = x.shape
    assert B == 1, "kernels are written for a single sequence"
    L = ada_w.shape[0]
    H = fox_fgate_w.shape[-1]
    E = moe_router_w.shape[-1]

    def as_rows(v):
        return v.reshape(v.shape[0], 1, v.shape[1])

    mod = _adaln(c.reshape(D, 1), ada_w, ada_b)
    mix_g, ffn_g = as_rows(mix_norm), as_rows(ffn_norm)
    zero_bias = jnp.zeros((1, 1, D), F32)
    xs = x.reshape(S, D)

    h_next = None
    for i in range(L):
        j = i // 2
        h = _norm_mod(xs, mix_g, mod, i, 0) if h_next is None else h_next
        h_next = None
        if i % 2 == 0:
            u = _pw1_glu(h, conv_pw1_w, as_rows(conv_pw1_b), j)
            v = _conv_ln(u, conv_dw_w, as_rows(conv_dw_b), as_rows(conv_ln_g), as_rows(conv_ln_b), j)
            xs, h = _mixer_out(v, conv_pw2_w, as_rows(conv_pw2_b), xs, mod, ffn_g, j, i, "conv_pw2")
            hid = _gate_up(h, ffn_gate_w, ffn_up_w, j)
            xs = _mm_res(hid, ffn_down_w, zero_bias, xs, mod, j, i, 5, 1024, 256, "ffn_down")
        else:
            q_gain = fox_q_norm[j] * (HEAD_DIM ** -0.5 * LOG2E)
            gain = jnp.concatenate([jnp.tile(q_gain, H), jnp.tile(fox_k_norm[j], H),
                                    jnp.ones((D,), F32)])[None, :]
            qkv = _qkv(h, fox_qkv_w, gain, j)
            cum = _fgate_cum(h, jnp.swapaxes(fox_fgate_w, 1, 2), fox_fgate_b, j)
            bound = 1.02 * HEAD_DIM * jnp.max(jnp.abs(q_gain)) * jnp.max(jnp.abs(fox_k_norm[j]))
            o = _attention(qkv, (cum * LOG2E).reshape(H, 1, S), bound)
            rw_pad = jnp.zeros((D, LANES), F32).at[:, :E].set(moe_router_w[j])
            rb_pad = jnp.zeros((1, LANES), F32).at[0, :E].set(moe_router_b[j])
            xs, logits = _mixer_out(o, fox_o_w, zero_bias, xs, mod, ffn_g, j, i, "fox_o", router=(rw_pad, rb_pad))
            tok_rows, pos, top_w, ranges = _route(logits[:, :E], E)
            items = _work_items(*ranges, tok_rows.shape[0], MOE_TM)
            xg = _gather_norm(tok_rows, xs, ffn_g, mod, i)
            hid = _moe_gate_up(items, xg, moe_gate_w, moe_up_w, j)
            y_sorted = _moe_down(items, hid, moe_down_w, j)
            if i + 1 < L:
                xs, h_next = _combine(pos, y_sorted, xs, mod, i, top_w, mix_g)
            else:
                xs = _combine(pos, y_sorted, xs, mod, i, top_w)
    return xs.reshape(B, S, D)
```

```python
import functools

import jax
import jax.numpy as jnp
from jax import lax
from jax.experimental import pallas as pl
from jax.experimental.pallas import tpu as pltpu

F32 = jnp.float32
BF16 = jnp.bfloat16

NORM_EPS = 1e-6
HEAD_DIM = 128
CONV_WIDTH = 31
CONV_HALO = 32
MOE_TOP_K = 2
LOG2E = 1.4426950408889634
MASK_VALUE = -1e30
ATTN_FAST_MAX_BOUND = 40.0
LANES = 128
SUBLANES = 8
BIG_VMEM_MB = 57
DENSE_TM = 2048
MOE_SUB = 256
MOE_TM = 512


def _cparams(n_axes, vmem_mb):
    return pltpu.CompilerParams(
        dimension_semantics=("arbitrary",) * n_axes,
        vmem_limit_bytes=vmem_mb << 20)


def _norm_modulate(x, g, sc, sh):
    ms = jnp.mean(x * x, axis=-1, keepdims=True)
    y = x * lax.rsqrt(ms + NORM_EPS) * g
    return y * (1.0 + sc) + sh


def _silu(x):
    return x * jax.nn.sigmoid(x)


def _adaln_kernel(c_ref, w_ref, b_ref, o_ref):
    c = c_ref[...]
    o_ref[0] = jnp.sum(w_ref[0] * _silu(c), axis=0, keepdims=True) + b_ref[0]


def _adaln(c_col, ada_w, ada_b):
    L, D, N = ada_w.shape
    tn = 1024
    return pl.pallas_call(
        _adaln_kernel,
        grid=(L, N // tn),
        in_specs=[pl.BlockSpec((D, 1), lambda l, j: (0, 0)),
                  pl.BlockSpec((1, D, tn), lambda l, j: (l, 0, j)),
                  pl.BlockSpec((1, 1, tn), lambda l, j: (l, 0, j))],
        out_specs=pl.BlockSpec((1, 1, tn), lambda l, j: (l, 0, j)),
        out_shape=jax.ShapeDtypeStruct((L, 1, N), F32),
        compiler_params=_cparams(2, 40),
        name="adaln",
    )(c_col, ada_w, ada_b.reshape(L, 1, N))


def _mod_spec(layer, which, D):
    return pl.BlockSpec((1, 1, D), lambda *_: (layer, 0, which))


def _vec_spec(layer, D):
    return pl.BlockSpec((1, 1, D), lambda *_: (layer, 0, 0))


def _norm_mod_kernel(x_ref, g_ref, sc_ref, sh_ref, o_ref):
    o_ref[...] = _norm_modulate(x_ref[...], g_ref[0], sc_ref[0], sh_ref[0]).astype(o_ref.dtype)


def _norm_mod(x, norm_g, mod, layer, which_shift):
    S, D = x.shape
    tm = 512
    return pl.pallas_call(
        _norm_mod_kernel,
        grid=(S // tm,),
        in_specs=[pl.BlockSpec((tm, D), lambda i: (i, 0)),
                  _vec_spec(layer, D),
                  _mod_spec(layer, which_shift + 1, D),
                  _mod_spec(layer, which_shift, D)],
        out_specs=pl.BlockSpec((tm, D), lambda i: (i, 0)),
        out_shape=jax.ShapeDtypeStruct((S, D), BF16),
        compiler_params=_cparams(1, 32),
        name="norm_mod",
    )(x, norm_g, mod, mod)


def _dot(a, w):
    return jnp.dot(a, w, preferred_element_type=F32)


def _pw1_glu_kernel(a_ref, wv_ref, wg_ref, bv_ref, bg_ref, o_ref):
    a = a_ref[...]
    val = _dot(a, wv_ref[...].astype(BF16)) + bv_ref[0]
    gate = _dot(a, wg_ref[...].astype(BF16)) + bg_ref[0]
    o_ref[...] = val * jax.nn.sigmoid(gate)


def _pw1_glu(h, pw1_w, pw1_b, j):
    S, D = h.shape
    tm, tn = DENSE_TM, 512
    nt = D // tn
    return pl.pallas_call(
        _pw1_glu_kernel,
        grid=(S // tm, nt),
        in_specs=[pl.BlockSpec((tm, D), lambda i, n: (i, 0)),
                  pl.BlockSpec((None, D, tn), lambda i, n: (j, 0, n)),
                  pl.BlockSpec((None, D, tn), lambda i, n: (j, 0, n + nt)),
                  pl.BlockSpec((1, 1, tn), lambda i, n: (j, 0, n)),
                  pl.BlockSpec((1, 1, tn), lambda i, n: (j, 0, n + nt))],
        out_specs=pl.BlockSpec((tm, tn), lambda i, n: (i, n)),
        out_shape=jax.ShapeDtypeStruct((S, D), F32),
        compiler_params=_cparams(2, BIG_VMEM_MB),
        name="pw1_glu",
    )(h, pw1_w, pw1_w, pw1_b, pw1_b)


def _gate_up_kernel(a_ref, wg_ref, wu_ref, o_ref):
    a = a_ref[...]
    g = _dot(a, wg_ref[...].astype(BF16))
    u = _dot(a, wu_ref[...].astype(BF16))
    o_ref[...] = (_silu(g) * u).astype(o_ref.dtype)


def _gate_up(h, gate_w, up_w, j):
    S, D = h.shape
    F = gate_w.shape[-1]
    tm, tn = DENSE_TM, 512
    return pl.pallas_call(
        _gate_up_kernel,
        grid=(S // tm, F // tn),
        in_specs=[pl.BlockSpec((tm, D), lambda i, n: (i, 0)),
                  pl.BlockSpec((None, D, tn), lambda i, n: (j, 0, n)),
                  pl.BlockSpec((None, D, tn), lambda i, n: (j, 0, n))],
        out_specs=pl.BlockSpec((tm, tn), lambda i, n: (i, n)),
        out_shape=jax.ShapeDtypeStruct((S, F), BF16),
        compiler_params=_cparams(2, BIG_VMEM_MB),
        name="ffn_gate_up",
    )(h, gate_w, up_w)


def _mm_res_kernel(a_ref, w_ref, b_ref, x_ref, g_ref, o_ref):
    y = _dot(a_ref[...], w_ref[...].astype(BF16)) + b_ref[0]
    o_ref[...] = x_ref[...] + g_ref[0] * y


def _mm_res(a, w, bias, x, mod, j, layer, which_gate, tm, tn, name):
    S, K = a.shape
    D = w.shape[-1]
    bj = j if bias.shape[0] > 1 else 0
    return pl.pallas_call(
        _mm_res_kernel,
        grid=(S // tm, D // tn),
        in_specs=[pl.BlockSpec((tm, K), lambda i, n: (i, 0)),
                  pl.BlockSpec((None, K, tn), lambda i, n: (j, 0, n)),
                  pl.BlockSpec((1, 1, tn), lambda i, n: (bj, 0, n)),
                  pl.BlockSpec((tm, tn), lambda i, n: (i, n)),
                  pl.BlockSpec((1, 1, tn), lambda i, n: (layer, 0, which_gate * (D // tn) + n))],
        out_specs=pl.BlockSpec((tm, tn), lambda i, n: (i, n)),
        out_shape=jax.ShapeDtypeStruct((S, D), F32),
        compiler_params=_cparams(2, BIG_VMEM_MB),
        name=name,
    )(a, w, bias, x, mod)


def _router_logits(h, rw, rb):
    h_hi, w_hi = h.astype(BF16), rw.astype(BF16)
    h_lo = (h - h_hi.astype(F32)).astype(BF16)
    w_lo = (rw - w_hi.astype(F32)).astype(BF16)
    return _dot(h_hi, w_hi) + (_dot(h_hi, w_lo) + _dot(h_lo, w_hi)) + rb


def _mixer_out_kernel(a_ref, w_hbm, b_ref, x_ref, gate_ref, ng_ref, nsc_ref, nsh_ref, *rest, j, router):
    if router:
        rw_ref, rb_ref, o_ref, aux_ref, w_stage, wb, sem = rest
    else:
        o_ref, aux_ref, w_stage, wb, sem = rest

    @pl.when(pl.program_id(0) == 0)
    def _():
        width = w_stage.shape[1]
        for c in range(wb.shape[1] // width):
            cols = slice(c * width, (c + 1) * width)
            cp = pltpu.make_async_copy(w_hbm.at[j, :, cols], w_stage, sem.at[0])
            cp.start()
            cp.wait()
            wb[:, cols] = w_stage[...].astype(BF16)

    y = _dot(a_ref[...], wb[...]) + b_ref[0]
    x_new = x_ref[...] + gate_ref[0] * y
    o_ref[...] = x_new
    h = _norm_modulate(x_new, ng_ref[0], nsc_ref[0], nsh_ref[0])
    if router:
        aux_ref[...] = _router_logits(h, rw_ref[...], rb_ref[...])
    else:
        aux_ref[...] = h.astype(aux_ref.dtype)


def _mixer_out(a, w, bias, x, mod, norm_g, j, layer, name, router=None):
    S, K = a.shape
    D = w.shape[-1]
    tm = 512
    bj = j if bias.shape[0] > 1 else 0
    row = lambda width: pl.BlockSpec((tm, width), lambda i: (i, 0))
    in_specs = [row(K), pl.BlockSpec(memory_space=pl.ANY), _vec_spec(bj, D), row(D), _mod_spec(layer, 2, D),
                _vec_spec(layer, D), _mod_spec(layer, 4, D), _mod_spec(layer, 3, D)]
    args = (a, w, bias, x, mod, norm_g, mod, mod)
    if router is None:
        aux_spec, aux_shape = row(D), jax.ShapeDtypeStruct((S, D), BF16)
    else:
        in_specs += [pl.BlockSpec((D, LANES), lambda i: (0, 0)), pl.BlockSpec((1, LANES), lambda i: (0, 0))]
        args += router
        aux_spec, aux_shape = row(LANES), jax.ShapeDtypeStruct((S, LANES), F32)
    return pl.pallas_call(
        functools.partial(_mixer_out_kernel, j=j, router=router is not None),
        grid=(S // tm,),
        in_specs=in_specs,
        out_specs=[row(D), aux_spec],
        out_shape=[jax.ShapeDtypeStruct((S, D), F32), aux_shape],
        scratch_shapes=[pltpu.VMEM((K, D // 2), F32), pltpu.VMEM((K, D), BF16), pltpu.SemaphoreType.DMA((1,))],
        compiler_params=_cparams(1, BIG_VMEM_MB),
        name=name,
    )(*args)


def _conv_ln_kernel(halo_ref, u_ref, w_ref, b_ref, g_ref, beta_ref, o_ref, win_ref, acc_ref, *, tm):
    i = pl.program_id(0)

    @pl.when(i == 0)
    def _():
        win_ref[0:CONV_HALO, :] = jnp.zeros((CONV_HALO, win_ref.shape[1]), F32)

    @pl.when(i > 0)
    def _():
        win_ref[0:CONV_HALO, :] = halo_ref[...]

    win_ref[CONV_HALO:CONV_HALO + tm, :] = u_ref[...]
    first = CONV_HALO - (CONV_WIDTH - 1)
    for c in range(u_ref.shape[1] // LANES):
        cs = slice(c * LANES, (c + 1) * LANES)
        out = None
        for b in range(SUBLANES):
            rows = tm if b == 0 else tm + SUBLANES
            y = None
            for j in range(first, first + CONV_WIDTH):
                if j % SUBLANES != b:
                    continue
                term = win_ref[j - b:j - b + rows, cs] * w_ref[0, j - first:j - first + 1, cs]
                y = term if y is None else y + term
            yb = y if b == 0 else y[b:b + tm]
            out = yb if out is None else out + yb
        acc_ref[:, cs] = out + b_ref[0][:, cs]
    y = acc_ref[...]
    mu = jnp.mean(y, axis=-1, keepdims=True)
    yc = y - mu
    var = jnp.mean(yc * yc, axis=-1, keepdims=True)
    z = yc * lax.rsqrt(var + NORM_EPS) * g_ref[0] + beta_ref[0]
    o_ref[...] = _silu(z).astype(o_ref.dtype)


def _conv_ln(u, dw_w, dw_b, ln_g, ln_b, j):
    S, D = u.shape
    tm = 128
    hb = tm // CONV_HALO
    return pl.pallas_call(
        functools.partial(_conv_ln_kernel, tm=tm),
        grid=(S // tm,),
        in_specs=[pl.BlockSpec((CONV_HALO, D), lambda i: (jnp.maximum(i * hb - 1, 0), 0)),
                  pl.BlockSpec((tm, D), lambda i: (i, 0)),
                  pl.BlockSpec((1, CONV_WIDTH, D), lambda i: (j, 0, 0)),
                  _vec_spec(j, D), _vec_spec(j, D), _vec_spec(j, D)],
        out_specs=pl.BlockSpec((tm, D), lambda i: (i, 0)),
        out_shape=jax.ShapeDtypeStruct((S, D), BF16),
        scratch_shapes=[pltpu.VMEM((CONV_HALO + tm, D), F32), pltpu.VMEM((tm, D), F32)],
        compiler_params=_cparams(1, 32),
        name="conv_ln",
    )(u, u, dw_w, dw_b, ln_g, ln_b)


def _qkv_kernel(a_ref, w_ref, gain_ref, o_ref, *, n_norm_tiles):
    is_norm_tile = pl.program_id(1) < n_norm_tiles
    acc = _dot(a_ref[...], w_ref[...].astype(BF16))
    gain = gain_ref[...]
    for h in range(acc.shape[1] // HEAD_DIM):
        sl = slice(h * HEAD_DIM, (h + 1) * HEAD_DIM)
        t = acc[:, sl]
        ms = jnp.mean(t * t, axis=-1, keepdims=True)
        r = jnp.where(is_norm_tile, lax.rsqrt(ms + NORM_EPS), 1.0)
        o_ref[:, sl] = (t * r * gain[:, sl]).astype(o_ref.dtype)


def _qkv(h, qkv_w, gain, j):
    S, D = h.shape
    N = qkv_w.shape[-1]
    tm, tn = DENSE_TM, 1024
    return pl.pallas_call(
        functools.partial(_qkv_kernel, n_norm_tiles=2 * D // tn),
        grid=(S // tm, N // tn),
        in_specs=[pl.BlockSpec((tm, D), lambda i, n: (i, 0)),
                  pl.BlockSpec((None, D, tn), lambda i, n: (j, 0, n)),
                  pl.BlockSpec((1, tn), lambda i, n: (0, n))],
        out_specs=pl.BlockSpec((tm, tn), lambda i, n: (i, n)),
        out_shape=jax.ShapeDtypeStruct((S, N), BF16),
        compiler_params=_cparams(2, BIG_VMEM_MB),
        name="fox_qkv",
    )(h, qkv_w, gain)


def _fgate_kernel(h_ref, fwt_ref, fb_ref, tri_ref, o_ref, carry_ref):
    i = pl.program_id(0)

    @pl.when(i == 0)
    def _():
        carry_ref[...] = jnp.zeros(carry_ref.shape, F32)

    z = lax.dot_general(fwt_ref[...].astype(BF16), h_ref[...], (((1,), (1,)), ((), ())),
                        preferred_element_type=F32) + fb_ref[...][:, 0:1]
    lf = jnp.minimum(z, 0.0) - jnp.log1p(jnp.exp(-jnp.abs(z)))
    hi = lf.astype(BF16)
    r1 = lf - hi.astype(F32)
    mid = r1.astype(BF16)
    lo = (r1 - mid.astype(F32)).astype(BF16)
    tri = tri_ref[...]
    cs = _dot(hi, tri) + _dot(mid, tri) + _dot(lo, tri) + carry_ref[...][:, 0:1]
    o_ref[...] = cs
    carry_ref[...] = jnp.broadcast_to(cs[:, cs.shape[1] - 1:], carry_ref.shape)


def _fgate_cum(h, fgate_wt, fgate_b, j):
    S, D = h.shape
    H = fgate_wt.shape[1]
    tm = 512
    tri = jnp.triu(jnp.ones((tm, tm), F32)).astype(BF16)
    fb = jnp.broadcast_to(fgate_b[j][:, None], (H, LANES))
    return pl.pallas_call(
        _fgate_kernel,
        grid=(S // tm,),
        in_specs=[pl.BlockSpec((tm, D), lambda i: (i, 0)),
                  pl.BlockSpec((None, H, D), lambda i: (j, 0, 0)),
                  pl.BlockSpec((H, LANES), lambda i: (0, 0)),
                  pl.BlockSpec((tm, tm), lambda i: (0, 0))],
        out_specs=pl.BlockSpec((H, tm), lambda i: (0, i)),
        out_shape=jax.ShapeDtypeStruct((H, S), F32),
        scratch_shapes=[pltpu.VMEM((H, LANES), F32)],
        compiler_params=_cparams(1, 32),
        name="fox_fgate",
    )(h, fgate_wt, fb, tri)


def _attn_kernel(flag_ref, q_ref, k_ref, v_ref, ck_ref, b_ref, o_ref,
                 acc_ref, m_ref, vaug_ref, rt_ref, *, tq, heads):
    qi = pl.program_id(1)
    S = k_ref.shape[0]
    hs = [slice(g * HEAD_DIM, (g + 1) * HEAD_DIM) for g in range(heads)]
    acc_ref[...] = jnp.zeros(acc_ref.shape, F32)

    @pl.when(qi == 0)
    def _():
        for g in range(heads):
            vaug_ref[g, :, 0:HEAD_DIM] = v_ref[:, hs[g]]
            vaug_ref[g, :, HEAD_DIM:] = jnp.ones((S, HEAD_DIM), BF16)

    def scores(g, ks, width):
        return lax.dot_general(q_ref[:, hs[g]], k_ref[pl.ds(ks, width), hs[g]],
                               (((1,), (1,)), ((), ())), preferred_element_type=F32)

    def visible(c):
        row = lax.broadcasted_iota(jnp.int32, (tq, LANES), 0)
        col = lax.broadcasted_iota(jnp.int32, (tq, LANES), 1) + c * LANES
        return col <= row

    def fast_step(ki, width, diag_at=None):
        ks = pl.multiple_of(ki * tq, tq)
        for g in range(heads):
            s = scores(g, ks, width)
            ckr = ck_ref[g, :, pl.ds(ks, width)]
            rt = rt_ref[g]
            ps = []
            for c in range(width // LANES):
                sl = slice(c * LANES, (c + 1) * LANES)
                t = s[:, sl] + (rt - ckr[:, sl])
                if diag_at is not None and c >= diag_at:
                    t = jnp.where(visible(c - diag_at), t, MASK_VALUE)
                ps.append(jnp.exp2(t).astype(BF16))
            p = jnp.concatenate(ps, axis=1)
            acc_ref[g] += _dot(p, vaug_ref[g, pl.ds(ks, width), :])

    def slow_step(ki, width, diag_at=None):
        ks = pl.multiple_of(ki * tq, tq)
        for g in range(heads):
            t = scores(g, ks, width) - ck_ref[g, :, pl.ds(ks, width)]
            if diag_at is not None:
                t = jnp.concatenate(
                    [t[:, c * LANES:(c + 1) * LANES] if c < diag_at else
                     jnp.where(visible(c - diag_at), t[:, c * LANES:(c + 1) * LANES], MASK_VALUE)
                     for c in range(width // LANES)], axis=1)
            m_prev = m_ref[g]
            m_new = jnp.maximum(m_prev, jnp.max(t, axis=-1, keepdims=True))
            alpha = jnp.exp2(m_prev - m_new)
            p = jnp.exp2(t - m_new[:, 0:1])
            pv = _dot(p.astype(BF16), vaug_ref[g, pl.ds(ks, width), :])
            acc_ref[g] = acc_ref[g] * jnp.concatenate([alpha, alpha], axis=1) + pv
            m_ref[g] = m_new

    def sweep(step):
        def body(kk, carry):
            step(4 * kk, 4 * tq)
            return carry
        lax.fori_loop(0, qi // 4, body, 0)
        rem = qi % 4

        @pl.when(rem >= 2)
        def _():
            step(qi - rem, 2 * tq)

        @pl.when(rem % 2 == 1)
        def _():
            step(qi - 1, 2 * tq, diag_at=tq // LANES)

        @pl.when(rem % 2 == 0)
        def _():
            step(qi, tq, diag_at=0)

    @pl.when(flag_ref[0] == 1)
    def _():
        qs = pl.multiple_of(qi * tq, tq)
        for g in range(heads):
            ckq = jnp.broadcast_to(ck_ref[g, :, pl.ds(qs, tq)], (LANES, tq))
            rt_ref[g] = ckq.T - b_ref[...]
        sweep(fast_step)

    @pl.when(flag_ref[0] != 1)
    def _():
        m_ref[...] = jnp.full(m_ref.shape, MASK_VALUE, F32)
        sweep(slow_step)

    for g in range(heads):
        acc = acc_ref[g]
        o_ref[:, hs[g]] = (acc[:, 0:HEAD_DIM] / acc[:, HEAD_DIM:]).astype(o_ref.dtype)


def _attention(qkv, ck2, bound):
    S = qkv.shape[0]
    H = ck2.shape[0]
    tq, heads = 512, 2
    hw = heads * HEAD_DIM
    flag = (bound <= ATTN_FAST_MAX_BOUND).astype(jnp.int32).reshape(1)
    bvec = jnp.broadcast_to(bound.astype(F32), (1, LANES))
    grid_spec = pltpu.PrefetchScalarGridSpec(
        num_scalar_prefetch=1,
        grid=(H // heads, S // tq),
        in_specs=[pl.BlockSpec((tq, hw), lambda h, i, f: (i, h)),
                  pl.BlockSpec((S, hw), lambda h, i, f: (0, H // heads + h)),
                  pl.BlockSpec((S, hw), lambda h, i, f: (0, 2 * (H // heads) + h)),
                  pl.BlockSpec((heads, 1, S), lambda h, i, f: (h, 0, 0)),
                  pl.BlockSpec((1, LANES), lambda h, i, f: (0, 0))],
        out_specs=pl.BlockSpec((tq, hw), lambda h, i, f: (i, h)),
        scratch_shapes=[pltpu.VMEM((heads, tq, 2 * HEAD_DIM), F32),
                        pltpu.VMEM((heads, tq, HEAD_DIM), F32),
                        pltpu.VMEM((heads, S, 2 * HEAD_DIM), BF16),
                        pltpu.VMEM((heads, tq, LANES), F32)])
    return pl.pallas_call(
        functools.partial(_attn_kernel, tq=tq, heads=heads),
        grid_spec=grid_spec,
        out_shape=jax.ShapeDtypeStruct((S, H * HEAD_DIM), BF16),
        compiler_params=_cparams(2, BIG_VMEM_MB),
        name="fox_attn",
    )(flag, qkv, qkv, qkv, ck2, bvec)


def _row_copy(src_hbm, dst, src_row, dst_row, sem):
    return pltpu.make_async_copy(src_hbm.at[pl.ds(src_row, 1), :], dst.at[pl.ds(dst_row, 1), :], sem)


def _gathered_rows(idx_ref, src_hbm, buf, sem, tm, n_per_row):
    i = pl.program_id(0)

    def start_copies(step, slot):
        def body(half, c):
            for odd in range(2):
                r = 2 * half + odd
                for k in range(n_per_row):
                    src_row = idx_ref[(step * tm + r) * n_per_row + k]
                    _row_copy(src_hbm, buf.at[slot, k], src_row, r, sem.at[slot]).start(priority=(odd + k) % 2)
            return c
        lax.fori_loop(0, tm // 2, body, 0, unroll=4)

    @pl.when(i == 0)
    def _():
        start_copies(0, 0)

    @pl.when(i + 1 < pl.num_programs(0))
    def _():
        start_copies(i + 1, (i + 1) % 2)

    slot = i % 2
    for k in range(n_per_row):
        pltpu.make_async_copy(src_hbm.at[pl.ds(0, tm), :], buf.at[slot, k], sem.at[slot]).wait()
    return slot


def _gather_norm_kernel(tok_ref, x_hbm, g_ref, sc_ref, sh_ref, o_ref, buf, sem, *, tm):
    slot = _gathered_rows(tok_ref, x_hbm, buf, sem, tm, 1)
    o_ref[...] = _norm_modulate(buf[slot, 0], g_ref[0], sc_ref[0], sh_ref[0]).astype(o_ref.dtype)


def _gather_norm(tok_sorted, x, norm_g, mod, layer):
    S, D = x.shape
    n_rows = tok_sorted.shape[0]
    tm = 256
    grid_spec = pltpu.PrefetchScalarGridSpec(
        num_scalar_prefetch=1,
        grid=(n_rows // tm,),
        in_specs=[pl.BlockSpec(memory_space=pl.ANY),
                  pl.BlockSpec((1, 1, D), lambda i, t: (layer, 0, 0)),
                  pl.BlockSpec((1, 1, D), lambda i, t: (layer, 0, 4)),
                  pl.BlockSpec((1, 1, D), lambda i, t: (layer, 0, 3))],
        out_specs=pl.BlockSpec((tm, D), lambda i, t: (i, 0)),
        scratch_shapes=[pltpu.VMEM((2, 1, tm, D), F32), pltpu.SemaphoreType.DMA((2,))])
    return pl.pallas_call(
        functools.partial(_gather_norm_kernel, tm=tm),
        grid_spec=grid_spec,
        out_shape=jax.ShapeDtypeStruct((n_rows, D), BF16),
        compiler_params=_cparams(1, 32),
        name="moe_gather",
    )(tok_sorted, x, norm_g, mod, mod)


def _grouped_matmul_item(item_refs, a_ref, o_ref, w_hbms, stages, caches, sems, layer, product):
    tile_ref, exp_ref, kind_ref, lo_ref, hi_ref, nxt_ref, lastrun_ref = item_refs
    n, i = pl.program_id(0), pl.program_id(1)
    tm, tn = o_ref.shape
    tile, lo, hi, kind = tile_ref[i], lo_ref[i], hi_ref[i], kind_ref[i]
    run_start = (i == 0) | (exp_ref[i] != exp_ref[jnp.maximum(i - 1, 0)])
    sub_blocks = [(slice(sb * MOE_SUB, (sb + 1) * MOE_SUB), tile * tm + sb * MOE_SUB) for sb in range(tm // MOE_SUB)]

    def copy(k, e, col_tile):
        cols = pl.ds(pl.multiple_of(col_tile * tn, tn), tn)
        return pltpu.make_async_copy(w_hbms[k].at[layer, e, :, cols], stages[k], sems.at[k])

    @pl.when((n == 0) & (i == 0))
    def _():
        for k in range(len(w_hbms)):
            copy(k, exp_ref[0], 0).start(priority=1)

    @pl.when(run_start)
    def _():
        nn = jnp.where(lastrun_ref[i] == 1, n + 1, n)
        for k in range(len(w_hbms)):
            copy(k, 0, 0).wait()
            caches[k][...] = stages[k][...].astype(BF16)

            @pl.when(nn < pl.num_programs(0))
            def _(k=k):
                copy(k, nxt_ref[i], nn).start(priority=1)

    def rows(rs):
        return product(a_ref[rs, :], [cache[...] for cache in caches]).astype(o_ref.dtype)

    whole = (lo <= tile * tm) & (hi >= (tile + 1) * tm)

    @pl.when((kind == 1) & whole)
    def _():
        o_ref[...] = rows(slice(0, tm))

    for rs, start in sub_blocks:
        @pl.when((kind == 1) & jnp.logical_not(whole) & (lo <= start) & (start < hi))
        def _(rs=rs):
            o_ref[rs, :] = rows(rs)

        @pl.when((kind == 2) & (lo <= start) & (start < hi))
        def _(rs=rs):
            o_ref[rs, :] = jnp.zeros((MOE_SUB, tn), o_ref.dtype)


def _moe_gu_kernel(tile_ref, exp_ref, kind_ref, lo_ref, hi_ref, nxt_ref, lastrun_ref, in_tile_ref,
                   a_ref, wg_hbm, wu_hbm, o_ref, wg_stage, wu_stage, wgb, wub, sems, *, layer):
    _grouped_matmul_item((tile_ref, exp_ref, kind_ref, lo_ref, hi_ref, nxt_ref, lastrun_ref), a_ref, o_ref,
                         (wg_hbm, wu_hbm), (wg_stage, wu_stage), (wgb, wub), sems, layer,
                         lambda a, ws: _silu(_dot(a, ws[0])) * _dot(a, ws[1]))


def _moe_gate_up(items, xs, gate_w, up_w, j):
    n_rows, D = xs.shape
    F = gate_w.shape[-1]
    tm, tn = MOE_TM, 1024
    n_items = items[0].shape[0]
    grid_spec = pltpu.PrefetchScalarGridSpec(
        num_scalar_prefetch=len(items),
        grid=(F // tn, n_items),
        in_specs=[pl.BlockSpec((tm, D), lambda n, i, *refs: (refs[-1][i], 0)),
                  pl.BlockSpec(memory_space=pl.ANY), pl.BlockSpec(memory_space=pl.ANY)],
        out_specs=pl.BlockSpec((tm, tn), lambda n, i, tile, *_: (tile[i], n)),
        scratch_shapes=[pltpu.VMEM((D, tn), F32), pltpu.VMEM((D, tn), F32),
                        pltpu.VMEM((D, tn), BF16), pltpu.VMEM((D, tn), BF16),
                        pltpu.SemaphoreType.DMA((2,))])
    return pl.pallas_call(
        functools.partial(_moe_gu_kernel, layer=j),
        grid_spec=grid_spec,
        out_shape=jax.ShapeDtypeStruct((n_rows, F), BF16),
        compiler_params=_cparams(2, BIG_VMEM_MB),
        name="moe_gate_up",
    )(*items, xs, gate_w, up_w)


def _moe_down_kernel(tile_ref, exp_ref, kind_ref, lo_ref, hi_ref, nxt_ref, lastrun_ref, in_tile_ref,
                     a_ref, w_hbm, o_ref, w_stage, wb, sems, *, layer):
    _grouped_matmul_item((tile_ref, exp_ref, kind_ref, lo_ref, hi_ref, nxt_ref, lastrun_ref), a_ref, o_ref,
                         (w_hbm,), (w_stage,), (wb,), sems, layer, lambda a, ws: _dot(a, ws[0]))


def _moe_down(items, hid, down_w, j):
    n_rows, F = hid.shape
    D = down_w.shape[-1]
    tm, tn = MOE_TM, 512
    n_items = items[0].shape[0]
    grid_spec = pltpu.PrefetchScalarGridSpec(
        num_scalar_prefetch=len(items),
        grid=(D // tn, n_items),
        in_specs=[pl.BlockSpec((tm, F), lambda n, i, *refs: (refs[-1][i], 0)),
                  pl.BlockSpec(memory_space=pl.ANY)],
        out_specs=pl.BlockSpec((tm, tn), lambda n, i, tile, *_: (tile[i], n)),
        scratch_shapes=[pltpu.VMEM((F, tn), F32), pltpu.VMEM((F, tn), BF16), pltpu.SemaphoreType.DMA((1,))])
    return pl.pallas_call(
        functools.partial(_moe_down_kernel, layer=j),
        grid_spec=grid_spec,
        out_shape=jax.ShapeDtypeStruct((n_rows, D), F32),
        compiler_params=_cparams(2, BIG_VMEM_MB),
        name="moe_down",
    )(*items, hid, down_w)


def _combined_rows(pos_ref, y_hbm, x_ref, g_ref, w_ref, buf, sem, tm):
    slot = _gathered_rows(pos_ref, y_hbm, buf, sem, tm, MOE_TOP_K)
    w = w_ref[...]
    y = buf[slot, 0] * w[:, 0:1] + buf[slot, 1] * w[:, 1:2]
    return x_ref[...] + g_ref[0] * y


def _combine_kernel(pos_ref, y_hbm, x_ref, g_ref, w_ref, o_ref, buf, sem, *, tm):
    o_ref[...] = _combined_rows(pos_ref, y_hbm, x_ref, g_ref, w_ref, buf, sem, tm)


def _combine_norm_kernel(pos_ref, y_hbm, x_ref, g_ref, w_ref, ng_ref, nsc_ref, nsh_ref, o_ref, h_ref, buf, sem,
                         *, tm):
    x_new = _combined_rows(pos_ref, y_hbm, x_ref, g_ref, w_ref, buf, sem, tm)
    o_ref[...] = x_new
    h_ref[...] = _norm_modulate(x_new, ng_ref[0], nsc_ref[0], nsh_ref[0]).astype(h_ref.dtype)


def _combine(pos, y_sorted, x, mod, layer, top_w, next_norm_g=None):
    S, D = x.shape
    tm = 256
    row_spec = pl.BlockSpec((tm, D), lambda i, p: (i, 0))
    in_specs = [pl.BlockSpec(memory_space=pl.ANY),
                row_spec,
                pl.BlockSpec((1, 1, D), lambda i, p: (layer, 0, 5)),
                pl.BlockSpec((tm, MOE_TOP_K), lambda i, p: (i, 0))]
    args = (pos, y_sorted, x, mod, top_w)
    out_specs, out_shape, body = row_spec, jax.ShapeDtypeStruct((S, D), F32), _combine_kernel
    if next_norm_g is not None:
        in_specs += [pl.BlockSpec((1, 1, D), lambda i, p: (layer + 1, 0, 0)),
                     pl.BlockSpec((1, 1, D), lambda i, p: (layer + 1, 0, 1)),
                     pl.BlockSpec((1, 1, D), lambda i, p: (layer + 1, 0, 0))]
        args += (next_norm_g, mod, mod)
        out_specs = [row_spec, row_spec]
        out_shape = [out_shape, jax.ShapeDtypeStruct((S, D), BF16)]
        body = _combine_norm_kernel
    grid_spec = pltpu.PrefetchScalarGridSpec(
        num_scalar_prefetch=1,
        grid=(S // tm,),
        in_specs=in_specs,
        out_specs=out_specs,
        scratch_shapes=[pltpu.VMEM((2, MOE_TOP_K, tm, D), F32), pltpu.SemaphoreType.DMA((2,))])
    return pl.pallas_call(
        functools.partial(body, tm=tm),
        grid_spec=grid_spec,
        out_shape=out_shape,
        compiler_params=_cparams(1, 40),
        name="moe_combine",
    )(*args)


def _work_items(starts, ends, counts, n_rows, tm):
    n_experts = counts.shape[0]
    starts = jnp.concatenate([starts, ends[-1:]])
    ends = jnp.concatenate([ends, jnp.full((1,), n_rows, ends.dtype)])
    counts = jnp.concatenate([counts, n_rows - ends[-2:-1]])
    n_items = n_rows // tm + n_experts
    first_tile = starts // tm
    n_e = jnp.where(counts > 0, (ends + tm - 1) // tm - first_tile, 0)
    item_end = jnp.cumsum(n_e)
    item_start = item_end - n_e
    total = item_end[-1]
    idx = jnp.arange(n_items, dtype=jnp.int32)
    idx_c = jnp.minimum(idx, total - 1)
    item_r = jnp.minimum(jnp.sum((idx_c[:, None] >= item_end[None, :]).astype(jnp.int32), axis=1), n_experts)
    item_tile = (first_tile[item_r] + idx_c - item_start[item_r]).astype(jnp.int32)
    kind = jnp.where(idx < total, jnp.where(item_r == n_experts, 2, 1), 0).astype(jnp.int32)
    last_expert = jnp.max(jnp.where(counts[:n_experts] > 0, jnp.arange(n_experts, dtype=jnp.int32), 0))
    item_e = jnp.where(item_r == n_experts, last_expert, item_r).astype(jnp.int32)
    in_tile = jnp.where(item_r == n_experts, (ends[n_experts - 1] - 1) // tm, item_tile).astype(jnp.int32)
    run_start = jnp.concatenate([jnp.ones((1,), bool), item_e[1:] != item_e[:-1]])
    next_start = lax.cummin(jnp.where(run_start, idx, n_items), reverse=True)
    next_start = jnp.concatenate([next_start[1:], jnp.full((1,), n_items, jnp.int32)])
    last_run = next_start >= n_items
    next_e = jnp.where(last_run, item_e[0], item_e[jnp.minimum(next_start, n_items - 1)])
    return (item_tile, item_e, kind, starts[item_r].astype(jnp.int32), ends[item_r].astype(jnp.int32),
            next_e.astype(jnp.int32), last_run.astype(jnp.int32), in_tile)


def _route(logits, n_experts):
    S = logits.shape[0]
    top_logit, top_idx = lax.top_k(logits, MOE_TOP_K)
    top_w = jax.nn.softmax(top_logit, axis=-1)
    n_assign = S * MOE_TOP_K
    n_rows = n_assign + n_experts * MOE_SUB
    flat_e = top_idx.reshape(n_assign).astype(jnp.int32)
    flat_idx = jnp.arange(n_assign, dtype=jnp.int32)
    counts = jnp.sum((flat_e[:, None] == jnp.arange(n_experts, dtype=jnp.int32)[None, :]).astype(jnp.int32), axis=0)
    dense_starts = jnp.cumsum(counts) - counts
    padded = (counts + MOE_SUB - 1) // MOE_SUB * MOE_SUB
    ends = jnp.cumsum(padded)
    starts = ends - padded
    order = jnp.sort(flat_e * n_assign + flat_idx) % n_assign
    _, dense_row = lax.sort((order, flat_idx), num_keys=1)
    pos = (starts[flat_e] + dense_row - dense_starts[flat_e]).astype(jnp.int32)
    tok_dense = order // MOE_TOP_K
    row = jnp.arange(n_rows, dtype=jnp.int32)
    row_e = jnp.minimum(jnp.sum((row[:, None] >= ends[None, :]).astype(jnp.int32), axis=1), n_experts - 1)
    offset = row - starts[row_e]
    is_real = offset < counts[row_e]
    tok_rows = jnp.where(is_real, tok_dense[jnp.clip(dense_starts[row_e] + offset, 0, n_assign - 1)], row % S)
    return tok_rows.astype(jnp.int32), pos, top_w, (starts, ends, padded)


def kernel(x, c, ada_w, ada_b, mix_norm, ffn_norm, conv_pw1_w, conv_pw1_b, conv_dw_w, conv_dw_b, conv_ln_g, conv_ln_b, conv_pw2_w, conv_pw2_b, fox_qkv_w, fox_o_w, fox_fgate_w, fox_fgate_b, fox_q_norm, fox_k_norm, ffn_gate_w, ffn_up_w, ffn_down_w, moe_router_w, moe_router_b, moe_gate_w, moe_up_w, moe_down_w):
    B, S, D = x.shape
    assert B == 1, "kernels are written for a single sequence"
    L = ada_w.shape[0]
    H = fox_fgate_w.shape[-1]
    E = moe_router_w.shape[-1]

    def as_rows(v):
        return v.reshape(v.shape[0], 1, v.shape[1])

    mod = _adaln(c.reshape(D, 1), ada_w, ada_b)
    mix_g, ffn_g = as_rows(mix_norm), as_rows(ffn_norm)
    zero_bias = jnp.zeros((1, 1, D), F32)
    xs = x.reshape(S, D)

    h_next = None
    for i in range(L):
        j = i // 2
        h = _norm_mod(xs, mix_g, mod, i, 0) if h_next is None else h_next
        h_next = None
        if i % 2 == 0:
            u = _pw1_glu(h, conv_pw1_w, as_rows(conv_pw1_b), j)
            v = _conv_ln(u, conv_dw_w, as_rows(conv_dw_b), as_rows(conv_ln_g), as_rows(conv_ln_b), j)
            xs, h = _mixer_out(v, conv_pw2_w, as_rows(conv_pw2_b), xs, mod, ffn_g, j, i, "conv_pw2")
            hid = _gate_up(h, ffn_gate_w, ffn_up_w, j)
            xs = _mm_res(hid, ffn_down_w, zero_bias, xs, mod, j, i, 5, 1024, 256, "ffn_down")
        else:
            q_gain = fox_q_norm[j] * (HEAD_DIM ** -0.5 * LOG2E)
            gain = jnp.concatenate([jnp.tile(q_gain, H), jnp.tile(fox_k_norm[j], H),
                                    jnp.ones((D,), F32)])[None, :]
            qkv = _qkv(h, fox_qkv_w, gain, j)
            cum = _fgate_cum(h, jnp.swapaxes(fox_fgate_w, 1, 2), fox_fgate_b, j)
            bound = 1.02 * HEAD_DIM * jnp.max(jnp.abs(q_gain)) * jnp.max(jnp.abs(fox_k_norm[j]))
            o = _attention(qkv, (cum * LOG2E).reshape(H, 1, S), bound)
            rw_pad = jnp.zeros((D, LANES), F32).at[:, :E].set(moe_router_w[j])
            rb_pad = jnp.zeros((1, LANES), F32).at[0, :E].set(moe_router_b[j])
            xs, logits = _mixer_out(o, fox_o_w, zero_bias, xs, mod, ffn_g, j, i, "fox_o", router=(rw_pad, rb_pad))
            tok_rows, pos, top_w, ranges = _route(logits[:, :E], E)
            items = _work_items(*ranges, tok_rows.shape[0], MOE_TM)
            xg = _gather_norm(tok_rows, xs, ffn_g, mod, i)
            hid = _moe_gate_up(items, xg, moe_gate_w, moe_up_w, j)
            y_sorted = _moe_down(items, hid, moe_down_w, j)
            if i + 1 < L:
                xs, h_next = _combine(pos, y_sorted, xs, mod, i, top_w, mix_g)
            else:
                xs = _combine(pos, y_sorted, xs, mod, i, top_w)
    return xs.reshape(B, S, D)
```

```python
import functools

import jax
import jax.numpy as jnp
from jax import lax
from jax.experimental import pallas as pl
from jax.experimental.pallas import tpu as pltpu

F32 = jnp.float32
BF16 = jnp.bfloat16

NORM_EPS = 1e-6
HEAD_DIM = 128
CONV_WIDTH = 31
CONV_HALO = 32
MOE_TOP_K = 2
LOG2E = 1.4426950408889634
MASK_VALUE = -1e30
ATTN_FAST_MAX_BOUND = 40.0
LANES = 128
SUBLANES = 8
BIG_VMEM_MB = 57
DENSE_TM = 2048
MOE_SUB = 256
MOE_TM = 512


def _cparams(n_axes, vmem_mb):
    return pltpu.CompilerParams(
        dimension_semantics=("arbitrary",) * n_axes,
        vmem_limit_bytes=vmem_mb << 20)


def _norm_modulate(x, g, sc, sh):
    ms = jnp.mean(x * x, axis=-1, keepdims=True)
    y = x * lax.rsqrt(ms + NORM_EPS) * g
    return y * (1.0 + sc) + sh


def _silu(x):
    return x * jax.nn.sigmoid(x)


def _adaln_kernel(c_ref, w_ref, b_ref, o_ref):
    c = c_ref[...]
    o_ref[0] = jnp.sum(w_ref[0] * _silu(c), axis=0, keepdims=True) + b_ref[0]


def _adaln(c_col, ada_w, ada_b):
    L, D, N = ada_w.shape
    tn = 1024
    return pl.pallas_call(
        _adaln_kernel,
        grid=(L, N // tn),
        in_specs=[pl.BlockSpec((D, 1), lambda l, j: (0, 0)),
                  pl.BlockSpec((1, D, tn), lambda l, j: (l, 0, j)),
                  pl.BlockSpec((1, 1, tn), lambda l, j: (l, 0, j))],
        out_specs=pl.BlockSpec((1, 1, tn), lambda l, j: (l, 0, j)),
        out_shape=jax.ShapeDtypeStruct((L, 1, N), F32),
        compiler_params=_cparams(2, 40),
        name="adaln",
    )(c_col, ada_w, ada_b.reshape(L, 1, N))


def _mod_spec(layer, which, D):
    return pl.BlockSpec((1, 1, D), lambda *_: (layer, 0, which))


def _vec_spec(layer, D):
    return pl.BlockSpec((1, 1, D), lambda *_: (layer, 0, 0))


def _norm_mod_kernel(x_ref, g_ref, sc_ref, sh_ref, o_ref):
    o_ref[...] = _norm_modulate(x_ref[...], g_ref[0], sc_ref[0], sh_ref[0]).astype(o_ref.dtype)


def _norm_mod(x, norm_g, mod, layer, which_shift):
    S, D = x.shape
    tm = 512
    return pl.pallas_call(
        _norm_mod_kernel,
        grid=(S // tm,),
        in_specs=[pl.BlockSpec((tm, D), lambda i: (i, 0)),
                  _vec_spec(layer, D),
                  _mod_spec(layer, which_shift + 1, D),
                  _mod_spec(layer, which_shift, D)],
        out_specs=pl.BlockSpec((tm, D), lambda i: (i, 0)),
        out_shape=jax.ShapeDtypeStruct((S, D), BF16),
        compiler_params=_cparams(1, 32),
        name="norm_mod",
    )(x, norm_g, mod, mod)


def _dot(a, w):
    return jnp.dot(a, w, preferred_element_type=F32)


def _pw1_glu_kernel(a_ref, wv_ref, wg_ref, bv_ref, bg_ref, o_ref):
    a = a_ref[...]
    val = _dot(a, wv_ref[...].astype(BF16)) + bv_ref[0]
    gate = _dot(a, wg_ref[...].astype(BF16)) + bg_ref[0]
    o_ref[...] = val * jax.nn.sigmoid(gate)


def _pw1_glu(h, pw1_w, pw1_b, j):
    S, D = h.shape
    tm, tn = DENSE_TM, 512
    nt = D // tn
    return pl.pallas_call(
        _pw1_glu_kernel,
        grid=(S // tm, nt),
        in_specs=[pl.BlockSpec((tm, D), lambda i, n: (i, 0)),
                  pl.BlockSpec((None, D, tn), lambda i, n: (j, 0, n)),
                  pl.BlockSpec((None, D, tn), lambda i, n: (j, 0, n + nt)),
                  pl.BlockSpec((1, 1, tn), lambda i, n: (j, 0, n)),
                  pl.BlockSpec((1, 1, tn), lambda i, n: (j, 0, n + nt))],
        out_specs=pl.BlockSpec((tm, tn), lambda i, n: (i, n)),
        out_shape=jax.ShapeDtypeStruct((S, D), F32),
        compiler_params=_cparams(2, BIG_VMEM_MB),
        name="pw1_glu",
    )(h, pw1_w, pw1_w, pw1_b, pw1_b)


def _gate_up_kernel(a_ref, wg_ref, wu_ref, o_ref):
    a = a_ref[...]
    g = _dot(a, wg_ref[...].astype(BF16))
    u = _dot(a, wu_ref[...].astype(BF16))
    o_ref[...] = (_silu(g) * u).astype(o_ref.dtype)


def _gate_up(h, gate_w, up_w, j):
    S, D = h.shape
    F = gate_w.shape[-1]
    tm, tn = DENSE_TM, 512
    return pl.pallas_call(
        _gate_up_kernel,
        grid=(S // tm, F // tn),
        in_specs=[pl.BlockSpec((tm, D), lambda i, n: (i, 0)),
                  pl.BlockSpec((None, D, tn), lambda i, n: (j, 0, n)),
                  pl.BlockSpec((None, D, tn), lambda i, n: (j, 0, n))],
        out_specs=pl.BlockSpec((tm, tn), lambda i, n: (i, n)),
        out_shape=jax.ShapeDtypeStruct((S, F), BF16),
        compiler_params=_cparams(2, BIG_VMEM_MB),
        name="ffn_gate_up",
    )(h, gate_w, up_w)


def _ffn_down_kernel(a_ref, w_hbm, x_ref, g_ref, o_ref, w_stage, sems, *, j):
    n_cols = pl.num_programs(1)
    tn = o_ref.shape[1]
    t = pl.program_id(0) * n_cols + pl.program_id(1)

    def copy(step, slot):
        cols = pl.ds(pl.multiple_of((step % n_cols) * tn, tn), tn)
        return pltpu.make_async_copy(w_hbm.at[j, :, cols], w_stage.at[slot], sems.at[slot])

    @pl.when(t == 0)
    def _():
        copy(0, 0).start(priority=1)

    @pl.when(t + 1 < pl.num_programs(0) * n_cols)
    def _():
        copy(t + 1, (t + 1) % 2).start(priority=1)

    slot = t % 2
    copy(t, slot).wait()
    y = _dot(a_ref[...], w_stage[slot].astype(BF16))
    o_ref[...] = x_ref[...] + g_ref[0] * y


def _ffn_down(a, w, x, mod, j, layer):
    S, K = a.shape
    D = w.shape[-1]
    tm, tn = 1024, 256
    return pl.pallas_call(
        functools.partial(_ffn_down_kernel, j=j),
        grid=(S // tm, D // tn),
        in_specs=[pl.BlockSpec((tm, K), lambda i, n: (i, 0)),
                  pl.BlockSpec(memory_space=pl.ANY),
                  pl.BlockSpec((tm, tn), lambda i, n: (i, n)),
                  pl.BlockSpec((1, 1, tn), lambda i, n: (layer, 0, 5 * (D // tn) + n))],
        out_specs=pl.BlockSpec((tm, tn), lambda i, n: (i, n)),
        out_shape=jax.ShapeDtypeStruct((S, D), F32),
        scratch_shapes=[pltpu.VMEM((2, K, tn), F32), pltpu.SemaphoreType.DMA((2,))],
        compiler_params=_cparams(2, BIG_VMEM_MB),
        name="ffn_down",
    )(a, w, x, mod)


def _router_logits(h, rw, rb):
    h_hi, w_hi = h.astype(BF16), rw.astype(BF16)
    h_lo = (h - h_hi.astype(F32)).astype(BF16)
    w_lo = (rw - w_hi.astype(F32)).astype(BF16)
    return _dot(h_hi, w_hi) + (_dot(h_hi, w_lo) + _dot(h_lo, w_hi)) + rb


def _mixer_out_kernel(a_ref, w_hbm, b_ref, x_ref, gate_ref, ng_ref, nsc_ref, nsh_ref, *rest, j, router):
    if router:
        rw_ref, rb_ref, o_ref, aux_ref, w_stage, wb, sem = rest
    else:
        o_ref, aux_ref, w_stage, wb, sem = rest

    @pl.when(pl.program_id(0) == 0)
    def _():
        width = w_stage.shape[1]
        for c in range(wb.shape[1] // width):
            cols = slice(c * width, (c + 1) * width)
            cp = pltpu.make_async_copy(w_hbm.at[j, :, cols], w_stage, sem.at[0])
            cp.start()
            cp.wait()
            wb[:, cols] = w_stage[...].astype(BF16)

    y = _dot(a_ref[...], wb[...]) + b_ref[0]
    x_new = x_ref[...] + gate_ref[0] * y
    o_ref[...] = x_new
    h = _norm_modulate(x_new, ng_ref[0], nsc_ref[0], nsh_ref[0])
    if router:
        aux_ref[...] = _router_logits(h, rw_ref[...], rb_ref[...])
    else:
        aux_ref[...] = h.astype(aux_ref.dtype)


def _mixer_out(a, w, bias, x, mod, norm_g, j, layer, name, router=None):
    S, K = a.shape
    D = w.shape[-1]
    tm = 512
    bj = j if bias.shape[0] > 1 else 0
    row = lambda width: pl.BlockSpec((tm, width), lambda i: (i, 0))
    in_specs = [row(K), pl.BlockSpec(memory_space=pl.ANY), _vec_spec(bj, D), row(D), _mod_spec(layer, 2, D),
                _vec_spec(layer, D), _mod_spec(layer, 4, D), _mod_spec(layer, 3, D)]
    args = (a, w, bias, x, mod, norm_g, mod, mod)
    if router is None:
        aux_spec, aux_shape = row(D), jax.ShapeDtypeStruct((S, D), BF16)
    else:
        in_specs += [pl.BlockSpec((D, LANES), lambda i: (0, 0)), pl.BlockSpec((1, LANES), lambda i: (0, 0))]
        args += router
        aux_spec, aux_shape = row(LANES), jax.ShapeDtypeStruct((S, LANES), F32)
    return pl.pallas_call(
        functools.partial(_mixer_out_kernel, j=j, router=router is not None),
        grid=(S // tm,),
        in_specs=in_specs,
        out_specs=[row(D), aux_spec],
        out_shape=[jax.ShapeDtypeStruct((S, D), F32), aux_shape],
        scratch_shapes=[pltpu.VMEM((K, D // 2), F32), pltpu.VMEM((K, D), BF16), pltpu.SemaphoreType.DMA((1,))],
        compiler_params=_cparams(1, BIG_VMEM_MB),
        name=name,
    )(*args)


def _conv_ln_kernel(halo_ref, u_ref, w_ref, b_ref, g_ref, beta_ref, o_ref, win_ref, acc_ref, *, tm):
    i = pl.program_id(0)

    @pl.when(i == 0)
    def _():
        win_ref[0:CONV_HALO, :] = jnp.zeros((CONV_HALO, win_ref.shape[1]), F32)

    @pl.when(i > 0)
    def _():
        win_ref[0:CONV_HALO, :] = halo_ref[...]

    win_ref[CONV_HALO:CONV_HALO + tm, :] = u_ref[...]
    first = CONV_HALO - (CONV_WIDTH - 1)
    for c in range(u_ref.shape[1] // LANES):
        cs = slice(c * LANES, (c + 1) * LANES)
        out = None
        for b in range(SUBLANES):
            rows = tm if b == 0 else tm + SUBLANES
            y = None
            for j in range(first, first + CONV_WIDTH):
                if j % SUBLANES != b:
                    continue
                term = win_ref[j - b:j - b + rows, cs] * w_ref[0, j - first:j - first + 1, cs]
                y = term if y is None else y + term
            yb = y if b == 0 else y[b:b + tm]
            out = yb if out is None else out + yb
        acc_ref[:, cs] = out + b_ref[0][:, cs]
    y = acc_ref[...]
    mu = jnp.mean(y, axis=-1, keepdims=True)
    yc = y - mu
    var = jnp.mean(yc * yc, axis=-1, keepdims=True)
    z = yc * lax.rsqrt(var + NORM_EPS) * g_ref[0] + beta_ref[0]
    o_ref[...] = _silu(z).astype(o_ref.dtype)


def _conv_ln(u, dw_w, dw_b, ln_g, ln_b, j):
    S, D = u.shape
    tm = 128
    hb = tm // CONV_HALO
    return pl.pallas_call(
        functools.partial(_conv_ln_kernel, tm=tm),
        grid=(S // tm,),
        in_specs=[pl.BlockSpec((CONV_HALO, D), lambda i: (jnp.maximum(i * hb - 1, 0), 0)),
                  pl.BlockSpec((tm, D), lambda i: (i, 0)),
                  pl.BlockSpec((1, CONV_WIDTH, D), lambda i: (j, 0, 0)),
                  _vec_spec(j, D), _vec_spec(j, D), _vec_spec(j, D)],
        out_specs=pl.BlockSpec((tm, D), lambda i: (i, 0)),
        out_shape=jax.ShapeDtypeStruct((S, D), BF16),
        scratch_shapes=[pltpu.VMEM((CONV_HALO + tm, D), F32), pltpu.VMEM((tm, D), F32)],
        compiler_params=_cparams(1, 32),
        name="conv_ln",
    )(u, u, dw_w, dw_b, ln_g, ln_b)


def _qkv_kernel(a_ref, w_ref, gain_ref, o_ref, *, n_norm_tiles):
    is_norm_tile = pl.program_id(1) < n_norm_tiles
    acc = _dot(a_ref[...], w_ref[...].astype(BF16))
    gain = gain_ref[...]
    for h in range(acc.shape[1] // HEAD_DIM):
        sl = slice(h * HEAD_DIM, (h + 1) * HEAD_DIM)
        t = acc[:, sl]
        ms = jnp.mean(t * t, axis=-1, keepdims=True)
        r = jnp.where(is_norm_tile, lax.rsqrt(ms + NORM_EPS), 1.0)
        o_ref[:, sl] = (t * r * gain[:, sl]).astype(o_ref.dtype)


def _qkv(h, qkv_w, gain, j):
    S, D = h.shape
    N = qkv_w.shape[-1]
    tm, tn = DENSE_TM, 1024
    return pl.pallas_call(
        functools.partial(_qkv_kernel, n_norm_tiles=2 * D // tn),
        grid=(S // tm, N // tn),
        in_specs=[pl.BlockSpec((tm, D), lambda i, n: (i, 0)),
                  pl.BlockSpec((None, D, tn), lambda i, n: (j, 0, n)),
                  pl.BlockSpec((1, tn), lambda i, n: (0, n))],
        out_specs=pl.BlockSpec((tm, tn), lambda i, n: (i, n)),
        out_shape=jax.ShapeDtypeStruct((S, N), BF16),
        compiler_params=_cparams(2, BIG_VMEM_MB),
        name="fox_qkv",
    )(h, qkv_w, gain)


def _fgate_kernel(h_ref, fwt_ref, fb_ref, tri_ref, o_ref, carry_ref):
    i = pl.program_id(0)

    @pl.when(i == 0)
    def _():
        carry_ref[...] = jnp.zeros(carry_ref.shape, F32)

    z = lax.dot_general(fwt_ref[...].astype(BF16), h_ref[...], (((1,), (1,)), ((), ())),
                        preferred_element_type=F32) + fb_ref[...][:, 0:1]
    lf = jnp.minimum(z, 0.0) - jnp.log1p(jnp.exp(-jnp.abs(z)))
    hi = lf.astype(BF16)
    r1 = lf - hi.astype(F32)
    mid = r1.astype(BF16)
    lo = (r1 - mid.astype(F32)).astype(BF16)
    tri = tri_ref[...]
    cs = _dot(hi, tri) + _dot(mid, tri) + _dot(lo, tri) + carry_ref[...][:, 0:1]
    o_ref[...] = cs
    carry_ref[...] = jnp.broadcast_to(cs[:, cs.shape[1] - 1:], carry_ref.shape)


def _fgate_cum(h, fgate_wt, fgate_b, j):
    S, D = h.shape
    H = fgate_wt.shape[1]
    tm = 512
    tri = jnp.triu(jnp.ones((tm, tm), F32)).astype(BF16)
    fb = jnp.broadcast_to(fgate_b[j][:, None], (H, LANES))
    return pl.pallas_call(
        _fgate_kernel,
        grid=(S // tm,),
        in_specs=[pl.BlockSpec((tm, D), lambda i: (i, 0)),
                  pl.BlockSpec((None, H, D), lambda i: (j, 0, 0)),
                  pl.BlockSpec((H, LANES), lambda i: (0, 0)),
                  pl.BlockSpec((tm, tm), lambda i: (0, 0))],
        out_specs=pl.BlockSpec((H, tm), lambda i: (0, i)),
        out_shape=jax.ShapeDtypeStruct((H, S), F32),
        scratch_shapes=[pltpu.VMEM((H, LANES), F32)],
        compiler_params=_cparams(1, 32),
        name="fox_fgate",
    )(h, fgate_wt, fb, tri)


def _attn_kernel(flag_ref, q_ref, k_ref, v_ref, ck_ref, b_ref, o_ref,
                 acc_ref, m_ref, vaug_ref, rt_ref, *, tq, heads):
    qi = pl.program_id(1)
    S = k_ref.shape[0]
    hs = [slice(g * HEAD_DIM, (g + 1) * HEAD_DIM) for g in range(heads)]
    acc_ref[...] = jnp.zeros(acc_ref.shape, F32)

    @pl.when(qi == 0)
    def _():
        for g in range(heads):
            vaug_ref[g, :, 0:HEAD_DIM] = v_ref[:, hs[g]]
            vaug_ref[g, :, HEAD_DIM:] = jnp.ones((S, HEAD_DIM), BF16)

    def scores(g, ks, width):
        return lax.dot_general(q_ref[:, hs[g]], k_ref[pl.ds(ks, width), hs[g]],
                               (((1,), (1,)), ((), ())), preferred_element_type=F32)

    def visible(c):
        row = lax.broadcasted_iota(jnp.int32, (tq, LANES), 0)
        col = lax.broadcasted_iota(jnp.int32, (tq, LANES), 1) + c * LANES
        return col <= row

    def fast_step(ki, width, diag_at=None):
        ks = pl.multiple_of(ki * tq, tq)
        for g in range(heads):
            s = scores(g, ks, width)
            ckr = ck_ref[g, :, pl.ds(ks, width)]
            rt = rt_ref[g]
            ps = []
            for c in range(width // LANES):
                sl = slice(c * LANES, (c + 1) * LANES)
                t = s[:, sl] + (rt - ckr[:, sl])
                if diag_at is not None and c >= diag_at:
                    t = jnp.where(visible(c - diag_at), t, MASK_VALUE)
                ps.append(jnp.exp2(t).astype(BF16))
            p = jnp.concatenate(ps, axis=1)
            acc_ref[g] += _dot(p, vaug_ref[g, pl.ds(ks, width), :])

    def slow_step(ki, width, diag_at=None):
        ks = pl.multiple_of(ki * tq, tq)
        for g in range(heads):
            t = scores(g, ks, width) - ck_ref[g, :, pl.ds(ks, width)]
            if diag_at is not None:
                t = jnp.concatenate(
                    [t[:, c * LANES:(c + 1) * LANES] if c < diag_at else
                     jnp.where(visible(c - diag_at), t[:, c * LANES:(c + 1) * LANES], MASK_VALUE)
                     for c in range(width // LANES)], axis=1)
            m_prev = m_ref[g]
            m_new = jnp.maximum(m_prev, jnp.max(t, axis=-1, keepdims=True))
            alpha = jnp.exp2(m_prev - m_new)
            p = jnp.exp2(t - m_new[:, 0:1])
            pv = _dot(p.astype(BF16), vaug_ref[g, pl.ds(ks, width), :])
            acc_ref[g] = acc_ref[g] * jnp.concatenate([alpha, alpha], axis=1) + pv
            m_ref[g] = m_new

    def sweep(step):
        def body(kk, carry):
            step(4 * kk, 4 * tq)
            return carry
        lax.fori_loop(0, qi // 4, body, 0)
        rem = qi % 4

        @pl.when(rem >= 2)
        def _():
            step(qi - rem, 2 * tq)

        @pl.when(rem % 2 == 1)
        def _():
            step(qi - 1, 2 * tq, diag_at=tq // LANES)

        @pl.when(rem % 2 == 0)
        def _():
            step(qi, tq, diag_at=0)

    @pl.when(flag_ref[0] == 1)
    def _():
        qs = pl.multiple_of(qi * tq, tq)
        for g in range(heads):
            ckq = jnp.broadcast_to(ck_ref[g, :, pl.ds(qs, tq)], (LANES, tq))
            rt_ref[g] = ckq.T - b_ref[...]
        sweep(fast_step)

    @pl.when(flag_ref[0] != 1)
    def _():
        m_ref[...] = jnp.full(m_ref.shape, MASK_VALUE, F32)
        sweep(slow_step)

    for g in range(heads):
        acc = acc_ref[g]
        o_ref[:, hs[g]] = (acc[:, 0:HEAD_DIM] / acc[:, HEAD_DIM:]).astype(o_ref.dtype)


def _attention(qkv, ck2, bound):
    S = qkv.shape[0]
    H = ck2.shape[0]
    tq, heads = 512, 2
    hw = heads * HEAD_DIM
    flag = (bound <= ATTN_FAST_MAX_BOUND).astype(jnp.int32).reshape(1)
    bvec = jnp.broadcast_to(bound.astype(F32), (1, LANES))
    grid_spec = pltpu.PrefetchScalarGridSpec(
        num_scalar_prefetch=1,
        grid=(H // heads, S // tq),
        in_specs=[pl.BlockSpec((tq, hw), lambda h, i, f: (i, h)),
                  pl.BlockSpec((S, hw), lambda h, i, f: (0, H // heads + h)),
                  pl.BlockSpec((S, hw), lambda h, i, f: (0, 2 * (H // heads) + h)),
                  pl.BlockSpec((heads, 1, S), lambda h, i, f: (h, 0, 0)),
                  pl.BlockSpec((1, LANES), lambda h, i, f: (0, 0))],
        out_specs=pl.BlockSpec((tq, hw), lambda h, i, f: (i, h)),
        scratch_shapes=[pltpu.VMEM((heads, tq, 2 * HEAD_DIM), F32),
                        pltpu.VMEM((heads, tq, HEAD_DIM), F32),
                        pltpu.VMEM((heads, S, 2 * HEAD_DIM), BF16),
                        pltpu.VMEM((heads, tq, LANES), F32)])
    return pl.pallas_call(
        functools.partial(_attn_kernel, tq=tq, heads=heads),
        grid_spec=grid_spec,
        out_shape=jax.ShapeDtypeStruct((S, H * HEAD_DIM), BF16),
        compiler_params=_cparams(2, BIG_VMEM_MB),
        name="fox_attn",
    )(flag, qkv, qkv, qkv, ck2, bvec)


def _row_copy(src_hbm, dst, src_row, dst_row, sem):
    return pltpu.make_async_copy(src_hbm.at[pl.ds(src_row, 1), :], dst.at[pl.ds(dst_row, 1), :], sem)


def _gathered_rows(idx_ref, src_hbm, buf, sem, tm, n_per_row):
    i = pl.program_id(0)

    def start_copies(step, slot):
        def body(half, c):
            for odd in range(2):
                r = 2 * half + odd
                for k in range(n_per_row):
                    src_row = idx_ref[(step * tm + r) * n_per_row + k]
                    _row_copy(src_hbm, buf.at[slot, k], src_row, r, sem.at[slot]).start(priority=(odd + k) % 2)
            return c
        lax.fori_loop(0, tm // 2, body, 0, unroll=4)

    @pl.when(i == 0)
    def _():
        start_copies(0, 0)

    @pl.when(i + 1 < pl.num_programs(0))
    def _():
        start_copies(i + 1, (i + 1) % 2)

    slot = i % 2
    for k in range(n_per_row):
        pltpu.make_async_copy(src_hbm.at[pl.ds(0, tm), :], buf.at[slot, k], sem.at[slot]).wait()
    return slot


def _gather_norm_kernel(tok_ref, x_hbm, g_ref, sc_ref, sh_ref, o_ref, buf, sem, *, tm):
    slot = _gathered_rows(tok_ref, x_hbm, buf, sem, tm, 1)
    o_ref[...] = _norm_modulate(buf[slot, 0], g_ref[0], sc_ref[0], sh_ref[0]).astype(o_ref.dtype)


def _gather_norm(tok_sorted, x, norm_g, mod, layer):
    S, D = x.shape
    n_rows = tok_sorted.shape[0]
    tm = 256
    grid_spec = pltpu.PrefetchScalarGridSpec(
        num_scalar_prefetch=1,
        grid=(n_rows // tm,),
        in_specs=[pl.BlockSpec(memory_space=pl.ANY),
                  pl.BlockSpec((1, 1, D), lambda i, t: (layer, 0, 0)),
                  pl.BlockSpec((1, 1, D), lambda i, t: (layer, 0, 4)),
                  pl.BlockSpec((1, 1, D), lambda i, t: (layer, 0, 3))],
        out_specs=pl.BlockSpec((tm, D), lambda i, t: (i, 0)),
        scratch_shapes=[pltpu.VMEM((2, 1, tm, D), F32), pltpu.SemaphoreType.DMA((2,))])
    return pl.pallas_call(
        functools.partial(_gather_norm_kernel, tm=tm),
        grid_spec=grid_spec,
        out_shape=jax.ShapeDtypeStruct((n_rows, D), BF16),
        compiler_params=_cparams(1, 32),
        name="moe_gather",
    )(tok_sorted, x, norm_g, mod, mod)


def _grouped_matmul_item(item_refs, a_ref, o_ref, w_hbms, stages, caches, sems, layer, product):
    tile_ref, exp_ref, kind_ref, lo_ref, hi_ref, nxt_ref, lastrun_ref = item_refs
    n, i = pl.program_id(0), pl.program_id(1)
    tm, tn = o_ref.shape
    tile, lo, hi, kind = tile_ref[i], lo_ref[i], hi_ref[i], kind_ref[i]
    run_start = (i == 0) | (exp_ref[i] != exp_ref[jnp.maximum(i - 1, 0)])
    sub_blocks = [(slice(sb * MOE_SUB, (sb + 1) * MOE_SUB), tile * tm + sb * MOE_SUB) for sb in range(tm // MOE_SUB)]

    def copy(k, e, col_tile):
        cols = pl.ds(pl.multiple_of(col_tile * tn, tn), tn)
        return pltpu.make_async_copy(w_hbms[k].at[layer, e, :, cols], stages[k], sems.at[k])

    @pl.when((n == 0) & (i == 0))
    def _():
        for k in range(len(w_hbms)):
            copy(k, exp_ref[0], 0).start(priority=1)

    @pl.when(run_start)
    def _():
        nn = jnp.where(lastrun_ref[i] == 1, n + 1, n)
        for k in range(len(w_hbms)):
            copy(k, 0, 0).wait()
            caches[k][...] = stages[k][...].astype(BF16)

            @pl.when(nn < pl.num_programs(0))
            def _(k=k):
                copy(k, nxt_ref[i], nn).start(priority=1)

    def rows(rs):
        return product(a_ref[rs, :], [cache[...] for cache in caches]).astype(o_ref.dtype)

    whole = (lo <= tile * tm) & (hi >= (tile + 1) * tm)

    @pl.when((kind == 1) & whole)
    def _():
        o_ref[...] = rows(slice(0, tm))

    for rs, start in sub_blocks:
        @pl.when((kind == 1) & jnp.logical_not(whole) & (lo <= start) & (start < hi))
        def _(rs=rs):
            o_ref[rs, :] = rows(rs)

        @pl.when((kind == 2) & (lo <= start) & (start < hi))
        def _(rs=rs):
            o_ref[rs, :] = jnp.zeros((MOE_SUB, tn), o_ref.dtype)


def _moe_gu_kernel(tile_ref, exp_ref, kind_ref, lo_ref, hi_ref, nxt_ref, lastrun_ref, in_tile_ref,
                   a_ref, wg_hbm, wu_hbm, o_ref, wg_stage, wu_stage, wgb, wub, sems, *, layer):
    _grouped_matmul_item((tile_ref, exp_ref, kind_ref, lo_ref, hi_ref, nxt_ref, lastrun_ref), a_ref, o_ref,
                         (wg_hbm, wu_hbm), (wg_stage, wu_stage), (wgb, wub), sems, layer,
                         lambda a, ws: _silu(_dot(a, ws[0])) * _dot(a, ws[1]))


def _moe_gate_up(items, xs, gate_w, up_w, j):
    n_rows, D = xs.shape
    F = gate_w.shape[-1]
    tm, tn = MOE_TM, 1024
    n_items = items[0].shape[0]
    grid_spec = pltpu.PrefetchScalarGridSpec(
        num_scalar_prefetch=len(items),
        grid=(F // tn, n_items),
        in_specs=[pl.BlockSpec((tm, D), lambda n, i, *refs: (refs[-1][i], 0)),
                  pl.BlockSpec(memory_space=pl.ANY), pl.BlockSpec(memory_space=pl.ANY)],
        out_specs=pl.BlockSpec((tm, tn), lambda n, i, tile, *_: (tile[i], n)),
        scratch_shapes=[pltpu.VMEM((D, tn), F32), pltpu.VMEM((D, tn), F32),
                        pltpu.VMEM((D, tn), BF16), pltpu.VMEM((D, tn), BF16),
                        pltpu.SemaphoreType.DMA((2,))])
    return pl.pallas_call(
        functools.partial(_moe_gu_kernel, layer=j),
        grid_spec=grid_spec,
        out_shape=jax.ShapeDtypeStruct((n_rows, F), BF16),
        compiler_params=_cparams(2, BIG_VMEM_MB),
        name="moe_gate_up",
    )(*items, xs, gate_w, up_w)


def _moe_down_kernel(tile_ref, exp_ref, kind_ref, lo_ref, hi_ref, nxt_ref, lastrun_ref, in_tile_ref,
                     a_ref, w_hbm, o_ref, w_stage, wb, sems, *, layer):
    _grouped_matmul_item((tile_ref, exp_ref, kind_ref, lo_ref, hi_ref, nxt_ref, lastrun_ref), a_ref, o_ref,
                         (w_hbm,), (w_stage,), (wb,), sems, layer, lambda a, ws: _dot(a, ws[0]))


def _moe_down(items, hid, down_w, j):
    n_rows, F = hid.shape
    D = down_w.shape[-1]
    tm, tn = MOE_TM, 512
    n_items = items[0].shape[0]
    grid_spec = pltpu.PrefetchScalarGridSpec(
        num_scalar_prefetch=len(items),
        grid=(D // tn, n_items),
        in_specs=[pl.BlockSpec((tm, F), lambda n, i, *refs: (refs[-1][i], 0)),
                  pl.BlockSpec(memory_space=pl.ANY)],
        out_specs=pl.BlockSpec((tm, tn), lambda n, i, tile, *_: (tile[i], n)),
        scratch_shapes=[pltpu.VMEM((F, tn), F32), pltpu.VMEM((F, tn), BF16), pltpu.SemaphoreType.DMA((1,))])
    return pl.pallas_call(
        functools.partial(_moe_down_kernel, layer=j),
        grid_spec=grid_spec,
        out_shape=jax.ShapeDtypeStruct((n_rows, D), F32),
        compiler_params=_cparams(2, BIG_VMEM_MB),
        name="moe_down",
    )(*items, hid, down_w)


def _combined_rows(pos_ref, y_hbm, x_ref, g_ref, w_ref, buf, sem, tm):
    slot = _gathered_rows(pos_ref, y_hbm, buf, sem, tm, MOE_TOP_K)
    w = w_ref[...]
    y = buf[slot, 0] * w[:, 0:1] + buf[slot, 1] * w[:, 1:2]
    return x_ref[...] + g_ref[0] * y


def _combine_kernel(pos_ref, y_hbm, x_ref, g_ref, w_ref, o_ref, buf, sem, *, tm):
    o_ref[...] = _combined_rows(pos_ref, y_hbm, x_ref, g_ref, w_ref, buf, sem, tm)


def _combine_norm_kernel(pos_ref, y_hbm, x_ref, g_ref, w_ref, ng_ref, nsc_ref, nsh_ref, o_ref, h_ref, buf, sem,
                         *, tm):
    x_new = _combined_rows(pos_ref, y_hbm, x_ref, g_ref, w_ref, buf, sem, tm)
    o_ref[...] = x_new
    h_ref[...] = _norm_modulate(x_new, ng_ref[0], nsc_ref[0], nsh_ref[0]).astype(h_ref.dtype)


def _combine(pos, y_sorted, x, mod, layer, top_w, next_norm_g=None):
    S, D = x.shape
    tm = 256
    row_spec = pl.BlockSpec((tm, D), lambda i, p: (i, 0))
    in_specs = [pl.BlockSpec(memory_space=pl.ANY),
                row_spec,
                pl.BlockSpec((1, 1, D), lambda i, p: (layer, 0, 5)),
                pl.BlockSpec((tm, MOE_TOP_K), lambda i, p: (i, 0))]
    args = (pos, y_sorted, x, mod, top_w)
    out_specs, out_shape, body = row_spec, jax.ShapeDtypeStruct((S, D), F32), _combine_kernel
    if next_norm_g is not None:
        in_specs += [pl.BlockSpec((1, 1, D), lambda i, p: (layer + 1, 0, 0)),
                     pl.BlockSpec((1, 1, D), lambda i, p: (layer + 1, 0, 1)),
                     pl.BlockSpec((1, 1, D), lambda i, p: (layer + 1, 0, 0))]
        args += (next_norm_g, mod, mod)
        out_specs = [row_spec, row_spec]
        out_shape = [out_shape, jax.ShapeDtypeStruct((S, D), BF16)]
        body = _combine_norm_kernel
    grid_spec = pltpu.PrefetchScalarGridSpec(
        num_scalar_prefetch=1,
        grid=(S // tm,),
        in_specs=in_specs,
        out_specs=out_specs,
        scratch_shapes=[pltpu.VMEM((2, MOE_TOP_K, tm, D), F32), pltpu.SemaphoreType.DMA((2,))])
    return pl.pallas_call(
        functools.partial(body, tm=tm),
        grid_spec=grid_spec,
        out_shape=out_shape,
        compiler_params=_cparams(1, 40),
        name="moe_combine",
    )(*args)


def _work_items(starts, ends, counts, n_rows, tm):
    n_experts = counts.shape[0]
    starts = jnp.concatenate([starts, ends[-1:]])
    ends = jnp.concatenate([ends, jnp.full((1,), n_rows, ends.dtype)])
    counts = jnp.concatenate([counts, n_rows - ends[-2:-1]])
    n_items = n_rows // tm + n_experts
    first_tile = starts // tm
    n_e = jnp.where(counts > 0, (ends + tm - 1) // tm - first_tile, 0)
    item_end = jnp.cumsum(n_e)
    item_start = item_end - n_e
    total = item_end[-1]
    idx = jnp.arange(n_items, dtype=jnp.int32)
    idx_c = jnp.minimum(idx, total - 1)
    item_r = jnp.minimum(jnp.sum((idx_c[:, None] >= item_end[None, :]).astype(jnp.int32), axis=1), n_experts)
    item_tile = (first_tile[item_r] + idx_c - item_start[item_r]).astype(jnp.int32)
    kind = jnp.where(idx < total, jnp.where(item_r == n_experts, 2, 1), 0).astype(jnp.int32)
    last_expert = jnp.max(jnp.where(counts[:n_experts] > 0, jnp.arange(n_experts, dtype=jnp.int32), 0))
    item_e = jnp.where(item_r == n_experts, last_expert, item_r).astype(jnp.int32)
    in_tile = jnp.where(item_r == n_experts, (ends[n_experts - 1] - 1) // tm, item_tile).astype(jnp.int32)
    run_start = jnp.concatenate([jnp.ones((1,), bool), item_e[1:] != item_e[:-1]])
    next_start = lax.cummin(jnp.where(run_start, idx, n_items), reverse=True)
    next_start = jnp.concatenate([next_start[1:], jnp.full((1,), n_items, jnp.int32)])
    last_run = next_start >= n_items
    next_e = jnp.where(last_run, item_e[0], item_e[jnp.minimum(next_start, n_items - 1)])
    return (item_tile, item_e, kind, starts[item_r].astype(jnp.int32), ends[item_r].astype(jnp.int32),
            next_e.astype(jnp.int32), last_run.astype(jnp.int32), in_tile)


def _route(logits, n_experts):
    S = logits.shape[0]
    top_logit, top_idx = lax.top_k(logits, MOE_TOP_K)
    top_w = jax.nn.softmax(top_logit, axis=-1)
    n_assign = S * MOE_TOP_K
    n_rows = n_assign + n_experts * MOE_SUB
    flat_e = top_idx.reshape(n_assign).astype(jnp.int32)
    flat_idx = jnp.arange(n_assign, dtype=jnp.int32)
    counts = jnp.sum((flat_e[:, None] == jnp.arange(n_experts, dtype=jnp.int32)[None, :]).astype(jnp.int32), axis=0)
    dense_starts = jnp.cumsum(counts) - counts
    padded = (counts + MOE_SUB - 1) // MOE_SUB * MOE_SUB
    ends = jnp.cumsum(padded)
    starts = ends - padded
    order = jnp.sort(flat_e * n_assign + flat_idx) % n_assign
    _, dense_row = lax.sort((order, flat_idx), num_keys=1)
    pos = (starts[flat_e] + dense_row - dense_starts[flat_e]).astype(jnp.int32)
    tok_dense = order // MOE_TOP_K
    row = jnp.arange(n_rows, dtype=jnp.int32)
    row_e = jnp.minimum(jnp.sum((row[:, None] >= ends[None, :]).astype(jnp.int32), axis=1), n_experts - 1)
    offset = row - starts[row_e]
    is_real = offset < counts[row_e]
    tok_rows = jnp.where(is_real, tok_dense[jnp.clip(dense_starts[row_e] + offset, 0, n_assign - 1)], row % S)
    return tok_rows.astype(jnp.int32), pos, top_w, (starts, ends, padded)


def kernel(x, c, ada_w, ada_b, mix_norm, ffn_norm, conv_pw1_w, conv_pw1_b, conv_dw_w, conv_dw_b, conv_ln_g, conv_ln_b, conv_pw2_w, conv_pw2_b, fox_qkv_w, fox_o_w, fox_fgate_w, fox_fgate_b, fox_q_norm, fox_k_norm, ffn_gate_w, ffn_up_w, ffn_down_w, moe_router_w, moe_router_b, moe_gate_w, moe_up_w, moe_down_w):
    B, S, D = x.shape
    assert B == 1, "kernels are written for a single sequence"
    L = ada_w.shape[0]
    H = fox_fgate_w.shape[-1]
    E = moe_router_w.shape[-1]

    def as_rows(v):
        return v.reshape(v.shape[0], 1, v.shape[1])

    mod = _adaln(c.reshape(D, 1), ada_w, ada_b)
    mix_g, ffn_g = as_rows(mix_norm), as_rows(ffn_norm)
    zero_bias = jnp.zeros((1, 1, D), F32)
    xs = x.reshape(S, D)

    h_next = None
    for i in range(L):
        j = i // 2
        h = _norm_mod(xs, mix_g, mod, i, 0) if h_next is None else h_next
        h_next = None
        if i % 2 == 0:
            u = _pw1_glu(h, conv_pw1_w, as_rows(conv_pw1_b), j)
            v = _conv_ln(u, conv_dw_w, as_rows(conv_dw_b), as_rows(conv_ln_g), as_rows(conv_ln_b), j)
            xs, h = _mixer_out(v, conv_pw2_w, as_rows(conv_pw2_b), xs, mod, ffn_g, j, i, "conv_pw2")
            hid = _gate_up(h, ffn_gate_w, ffn_up_w, j)
            xs = _ffn_down(hid, ffn_down_w, xs, mod, j, i)
        else:
            q_gain = fox_q_norm[j] * (HEAD_DIM ** -0.5 * LOG2E)
            gain = jnp.concatenate([jnp.tile(q_gain, H), jnp.tile(fox_k_norm[j], H),
                                    jnp.ones((D,), F32)])[None, :]
            qkv = _qkv(h, fox_qkv_w, gain, j)
            cum = _fgate_cum(h, jnp.swapaxes(fox_fgate_w, 1, 2), fox_fgate_b, j)
            bound = 1.02 * HEAD_DIM * jnp.max(jnp.abs(q_gain)) * jnp.max(jnp.abs(fox_k_norm[j]))
            o = _attention(qkv, (cum * LOG2E).reshape(H, 1, S), bound)
            rw_pad = jnp.zeros((D, LANES), F32).at[:, :E].set(moe_router_w[j])
            rb_pad = jnp.zeros((1, LANES), F32).at[0, :E].set(moe_router_b[j])
            xs, logits = _mixer_out(o, fox_o_w, zero_bias, xs, mod, ffn_g, j, i, "fox_o", router=(rw_pad, rb_pad))
            tok_rows, pos, top_w, ranges = _route(logits[:, :E], E)
            items = _work_items(*ranges, tok_rows.shape[0], MOE_TM)
            xg = _gather_norm(tok_rows, xs, ffn_g, mod, i)
            hid = _moe_gate_up(items, xg, moe_gate_w, moe_up_w, j)
            y_sorted = _moe_down(items, hid, moe_down_w, j)
            if i + 1 < L:
                xs, h_next = _combine(pos, y_sorted, xs, mod, i, top_w, mix_g)
            else:
                xs = _combine(pos, y_sorted, xs, mod, i, top_w)
    return xs.reshape(B, S, D)
```

```python
import functools

import jax
import jax.numpy as jnp
from jax import lax
from jax.experimental import pallas as pl
from jax.experimental.pallas import tpu as pltpu

F32 = jnp.float32
BF16 = jnp.bfloat16

NORM_EPS = 1e-6
HEAD_DIM = 128
CONV_WIDTH = 31
CONV_HALO = 32
MOE_TOP_K = 2
LOG2E = 1.4426950408889634
MASK_VALUE = -1e30
ATTN_FAST_MAX_BOUND = 40.0
LANES = 128
SUBLANES = 8
BIG_VMEM_MB = 57
DENSE_TM = 2048
MOE_SUB = 256
MOE_TM = 512


def _cparams(n_axes, vmem_mb):
    return pltpu.CompilerParams(
        dimension_semantics=("arbitrary",) * n_axes,
        vmem_limit_bytes=vmem_mb << 20)


def _norm_modulate(x, g, sc, sh):
    ms = jnp.mean(x * x, axis=-1, keepdims=True)
    y = x * lax.rsqrt(ms + NORM_EPS) * g
    return y * (1.0 + sc) + sh


def _silu(x):
    return x * jax.nn.sigmoid(x)


def _adaln_kernel(c_ref, w_ref, b_ref, o_ref):
    c = c_ref[...]
    o_ref[0] = jnp.sum(w_ref[0] * _silu(c), axis=0, keepdims=True) + b_ref[0]


def _adaln(c_col, ada_w, ada_b):
    L, D, N = ada_w.shape
    tn = 1024
    return pl.pallas_call(
        _adaln_kernel,
        grid=(L, N // tn),
        in_specs=[pl.BlockSpec((D, 1), lambda l, j: (0, 0)),
                  pl.BlockSpec((1, D, tn), lambda l, j: (l, 0, j)),
                  pl.BlockSpec((1, 1, tn), lambda l, j: (l, 0, j))],
        out_specs=pl.BlockSpec((1, 1, tn), lambda l, j: (l, 0, j)),
        out_shape=jax.ShapeDtypeStruct((L, 1, N), F32),
        compiler_params=_cparams(2, 40),
        name="adaln",
    )(c_col, ada_w, ada_b.reshape(L, 1, N))


def _mod_spec(layer, which, D):
    return pl.BlockSpec((1, 1, D), lambda *_: (layer, 0, which))


def _vec_spec(layer, D):
    return pl.BlockSpec((1, 1, D), lambda *_: (layer, 0, 0))


def _norm_mod_kernel(x_ref, g_ref, sc_ref, sh_ref, o_ref):
    o_ref[...] = _norm_modulate(x_ref[...], g_ref[0], sc_ref[0], sh_ref[0]).astype(o_ref.dtype)


def _norm_mod(x, norm_g, mod, layer, which_shift):
    S, D = x.shape
    tm = 512
    return pl.pallas_call(
        _norm_mod_kernel,
        grid=(S // tm,),
        in_specs=[pl.BlockSpec((tm, D), lambda i: (i, 0)),
                  _vec_spec(layer, D),
                  _mod_spec(layer, which_shift + 1, D),
                  _mod_spec(layer, which_shift, D)],
        out_specs=pl.BlockSpec((tm, D), lambda i: (i, 0)),
        out_shape=jax.ShapeDtypeStruct((S, D), BF16),
        compiler_params=_cparams(1, 32),
        name="norm_mod",
    )(x, norm_g, mod, mod)


def _dot(a, w):
    return jnp.dot(a, w, preferred_element_type=F32)


def _pw1_glu_kernel(a_ref, wv_ref, wg_ref, bv_ref, bg_ref, o_ref):
    a = a_ref[...]
    val = _dot(a, wv_ref[...].astype(BF16)) + bv_ref[0]
    gate = _dot(a, wg_ref[...].astype(BF16)) + bg_ref[0]
    o_ref[...] = val * jax.nn.sigmoid(gate)


def _pw1_glu(h, pw1_w, pw1_b, j):
    S, D = h.shape
    tm, tn = DENSE_TM, 512
    nt = D // tn
    return pl.pallas_call(
        _pw1_glu_kernel,
        grid=(S // tm, nt),
        in_specs=[pl.BlockSpec((tm, D), lambda i, n: (i, 0)),
                  pl.BlockSpec((None, D, tn), lambda i, n: (j, 0, n)),
                  pl.BlockSpec((None, D, tn), lambda i, n: (j, 0, n + nt)),
                  pl.BlockSpec((1, 1, tn), lambda i, n: (j, 0, n)),
                  pl.BlockSpec((1, 1, tn), lambda i, n: (j, 0, n + nt))],
        out_specs=pl.BlockSpec((tm, tn), lambda i, n: (i, n)),
        out_shape=jax.ShapeDtypeStruct((S, D), F32),
        compiler_params=_cparams(2, BIG_VMEM_MB),
        name="pw1_glu",
    )(h, pw1_w, pw1_w, pw1_b, pw1_b)


def _gate_up_kernel(a_ref, wg_ref, wu_ref, o_ref):
    a = a_ref[...]
    g = _dot(a, wg_ref[...].astype(BF16))
    u = _dot(a, wu_ref[...].astype(BF16))
    o_ref[...] = (_silu(g) * u).astype(o_ref.dtype)


def _gate_up(h, gate_w, up_w, j):
    S, D = h.shape
    F = gate_w.shape[-1]
    tm, tn = DENSE_TM, 512
    return pl.pallas_call(
        _gate_up_kernel,
        grid=(S // tm, F // tn),
        in_specs=[pl.BlockSpec((tm, D), lambda i, n: (i, 0)),
                  pl.BlockSpec((None, D, tn), lambda i, n: (j, 0, n)),
                  pl.BlockSpec((None, D, tn), lambda i, n: (j, 0, n))],
        out_specs=pl.BlockSpec((tm, tn), lambda i, n: (i, n)),
        out_shape=jax.ShapeDtypeStruct((S, F), BF16),
        compiler_params=_cparams(2, BIG_VMEM_MB),
        name="ffn_gate_up",
    )(h, gate_w, up_w)


def _mm_res_kernel(a_ref, w_ref, b_ref, x_ref, g_ref, o_ref):
    y = _dot(a_ref[...], w_ref[...].astype(BF16)) + b_ref[0]
    o_ref[...] = x_ref[...] + g_ref[0] * y


def _mm_res(a, w, bias, x, mod, j, layer, which_gate, tm, tn, name):
    S, K = a.shape
    D = w.shape[-1]
    bj = j if bias.shape[0] > 1 else 0
    return pl.pallas_call(
        _mm_res_kernel,
        grid=(S // tm, D // tn),
        in_specs=[pl.BlockSpec((tm, K), lambda i, n: (i, 0)),
                  pl.BlockSpec((None, K, tn), lambda i, n: (j, 0, n)),
                  pl.BlockSpec((1, 1, tn), lambda i, n: (bj, 0, n)),
                  pl.BlockSpec((tm, tn), lambda i, n: (i, n)),
                  pl.BlockSpec((1, 1, tn), lambda i, n: (layer, 0, which_gate * (D // tn) + n))],
        out_specs=pl.BlockSpec((tm, tn), lambda i, n: (i, n)),
        out_shape=jax.ShapeDtypeStruct((S, D), F32),
        compiler_params=_cparams(2, BIG_VMEM_MB),
        name=name,
    )(a, w, bias, x, mod)


def _router_logits(h, rw, rb):
    h_hi, w_hi = h.astype(BF16), rw.astype(BF16)
    h_lo = (h - h_hi.astype(F32)).astype(BF16)
    w_lo = (rw - w_hi.astype(F32)).astype(BF16)
    return _dot(h_hi, w_hi) + (_dot(h_hi, w_lo) + _dot(h_lo, w_hi)) + rb


def _mixer_out_kernel(a_ref, w_hbm, b_ref, x_ref, gate_ref, ng_ref, nsc_ref, nsh_ref, *rest, j, router):
    if router:
        rw_ref, rb_ref, o_ref, aux_ref, w_stage, wb, sem = rest
    else:
        o_ref, aux_ref, w_stage, wb, sem = rest

    @pl.when(pl.program_id(0) == 0)
    def _():
        width = w_stage.shape[1]
        for c in range(wb.shape[1] // width):
            cols = slice(c * width, (c + 1) * width)
            cp = pltpu.make_async_copy(w_hbm.at[j, :, cols], w_stage, sem.at[0])
            cp.start()
            cp.wait()
            wb[:, cols] = w_stage[...].astype(BF16)

    y = _dot(a_ref[...], wb[...]) + b_ref[0]
    x_new = x_ref[...] + gate_ref[0] * y
    o_ref[...] = x_new
    h = _norm_modulate(x_new, ng_ref[0], nsc_ref[0], nsh_ref[0])
    if router:
        aux_ref[...] = _router_logits(h, rw_ref[...], rb_ref[...])
    else:
        aux_ref[...] = h.astype(aux_ref.dtype)


def _mixer_out(a, w, bias, x, mod, norm_g, j, layer, name, router=None):
    S, K = a.shape
    D = w.shape[-1]
    tm = 512
    bj = j if bias.shape[0] > 1 else 0
    row = lambda width: pl.BlockSpec((tm, width), lambda i: (i, 0))
    in_specs = [row(K), pl.BlockSpec(memory_space=pl.ANY), _vec_spec(bj, D), row(D), _mod_spec(layer, 2, D),
                _vec_spec(layer, D), _mod_spec(layer, 4, D), _mod_spec(layer, 3, D)]
    args = (a, w, bias, x, mod, norm_g, mod, mod)
    if router is None:
        aux_spec, aux_shape = row(D), jax.ShapeDtypeStruct((S, D), BF16)
    else:
        in_specs += [pl.BlockSpec((D, LANES), lambda i: (0, 0)), pl.BlockSpec((1, LANES), lambda i: (0, 0))]
        args += router
        aux_spec, aux_shape = row(LANES), jax.ShapeDtypeStruct((S, LANES), F32)
    return pl.pallas_call(
        functools.partial(_mixer_out_kernel, j=j, router=router is not None),
        grid=(S // tm,),
        in_specs=in_specs,
        out_specs=[row(D), aux_spec],
        out_shape=[jax.ShapeDtypeStruct((S, D), F32), aux_shape],
        scratch_shapes=[pltpu.VMEM((K, D // 2), F32), pltpu.VMEM((K, D), BF16), pltpu.SemaphoreType.DMA((1,))],
        compiler_params=_cparams(1, BIG_VMEM_MB),
        name=name,
    )(*args)


def _conv_ln_kernel(halo_ref, u_ref, w_ref, b_ref, g_ref, beta_ref, o_ref, win_ref, acc_ref, *, tm):
    i = pl.program_id(0)

    @pl.when(i == 0)
    def _():
        win_ref[0:CONV_HALO, :] = jnp.zeros((CONV_HALO, win_ref.shape[1]), F32)

    @pl.when(i > 0)
    def _():
        win_ref[0:CONV_HALO, :] = halo_ref[...]

    win_ref[CONV_HALO:CONV_HALO + tm, :] = u_ref[...]
    first = CONV_HALO - (CONV_WIDTH - 1)
    for c in range(u_ref.shape[1] // LANES):
        cs = slice(c * LANES, (c + 1) * LANES)
        out = None
        for b in range(SUBLANES):
            rows = tm if b == 0 else tm + SUBLANES
            y = None
            for j in range(first, first + CONV_WIDTH):
                if j % SUBLANES != b:
                    continue
                term = win_ref[j - b:j - b + rows, cs] * w_ref[0, j - first:j - first + 1, cs]
                y = term if y is None else y + term
            yb = y if b == 0 else y[b:b + tm]
            out = yb if out is None else out + yb
        acc_ref[:, cs] = out + b_ref[0][:, cs]
    y = acc_ref[...]
    mu = jnp.mean(y, axis=-1, keepdims=True)
    yc = y - mu
    var = jnp.mean(yc * yc, axis=-1, keepdims=True)
    z = yc * lax.rsqrt(var + NORM_EPS) * g_ref[0] + beta_ref[0]
    o_ref[...] = _silu(z).astype(o_ref.dtype)


def _conv_ln(u, dw_w, dw_b, ln_g, ln_b, j):
    S, D = u.shape
    tm = 128
    hb = tm // CONV_HALO
    return pl.pallas_call(
        functools.partial(_conv_ln_kernel, tm=tm),
        grid=(S // tm,),
        in_specs=[pl.BlockSpec((CONV_HALO, D), lambda i: (jnp.maximum(i * hb - 1, 0), 0)),
                  pl.BlockSpec((tm, D), lambda i: (i, 0)),
                  pl.BlockSpec((1, CONV_WIDTH, D), lambda i: (j, 0, 0)),
                  _vec_spec(j, D), _vec_spec(j, D), _vec_spec(j, D)],
        out_specs=pl.BlockSpec((tm, D), lambda i: (i, 0)),
        out_shape=jax.ShapeDtypeStruct((S, D), BF16),
        scratch_shapes=[pltpu.VMEM((CONV_HALO + tm, D), F32), pltpu.VMEM((tm, D), F32)],
        compiler_params=_cparams(1, 32),
        name="conv_ln",
    )(u, u, dw_w, dw_b, ln_g, ln_b)


def _qkv_kernel(a_ref, w_ref, gain_ref, o_ref, *, n_norm_tiles):
    is_norm_tile = pl.program_id(1) < n_norm_tiles
    acc = _dot(a_ref[...], w_ref[...].astype(BF16))
    gain = gain_ref[...]
    for h in range(acc.shape[1] // HEAD_DIM):
        sl = slice(h * HEAD_DIM, (h + 1) * HEAD_DIM)
        t = acc[:, sl]
        ms = jnp.mean(t * t, axis=-1, keepdims=True)
        r = jnp.where(is_norm_tile, lax.rsqrt(ms + NORM_EPS), 1.0)
        o_ref[:, sl] = (t * r * gain[:, sl]).astype(o_ref.dtype)


def _qkv(h, qkv_w, gain, j):
    S, D = h.shape
    N = qkv_w.shape[-1]
    tm, tn = DENSE_TM, 1024
    return pl.pallas_call(
        functools.partial(_qkv_kernel, n_norm_tiles=2 * D // tn),
        grid=(S // tm, N // tn),
        in_specs=[pl.BlockSpec((tm, D), lambda i, n: (i, 0)),
                  pl.BlockSpec((None, D, tn), lambda i, n: (j, 0, n)),
                  pl.BlockSpec((1, tn), lambda i, n: (0, n))],
        out_specs=pl.BlockSpec((tm, tn), lambda i, n: (i, n)),
        out_shape=jax.ShapeDtypeStruct((S, N), BF16),
        compiler_params=_cparams(2, BIG_VMEM_MB),
        name="fox_qkv",
    )(h, qkv_w, gain)


def _fgate_kernel(h_ref, fwt_ref, fb_ref, tri_ref, o_ref, carry_ref):
    i = pl.program_id(0)

    @pl.when(i == 0)
    def _():
        carry_ref[...] = jnp.zeros(carry_ref.shape, F32)

    z = lax.dot_general(fwt_ref[...].astype(BF16), h_ref[...], (((1,), (1,)), ((), ())),
                        preferred_element_type=F32) + fb_ref[...][:, 0:1]
    lf = jnp.minimum(z, 0.0) - jnp.log1p(jnp.exp(-jnp.abs(z)))
    hi = lf.astype(BF16)
    r1 = lf - hi.astype(F32)
    mid = r1.astype(BF16)
    lo = (r1 - mid.astype(F32)).astype(BF16)
    tri = tri_ref[...]
    cs = _dot(hi, tri) + _dot(mid, tri) + _dot(lo, tri) + carry_ref[...][:, 0:1]
    o_ref[...] = cs
    carry_ref[...] = jnp.broadcast_to(cs[:, cs.shape[1] - 1:], carry_ref.shape)


def _fgate_cum(h, fgate_wt, fgate_b, j):
    S, D = h.shape
    H = fgate_wt.shape[1]
    tm = 512
    tri = jnp.triu(jnp.ones((tm, tm), F32)).astype(BF16)
    fb = jnp.broadcast_to(fgate_b[j][:, None], (H, LANES))
    return pl.pallas_call(
        _fgate_kernel,
        grid=(S // tm,),
        in_specs=[pl.BlockSpec((tm, D), lambda i: (i, 0)),
                  pl.BlockSpec((None, H, D), lambda i: (j, 0, 0)),
                  pl.BlockSpec((H, LANES), lambda i: (0, 0)),
                  pl.BlockSpec((tm, tm), lambda i: (0, 0))],
        out_specs=pl.BlockSpec((H, tm), lambda i: (0, i)),
        out_shape=jax.ShapeDtypeStruct((H, S), F32),
        scratch_shapes=[pltpu.VMEM((H, LANES), F32)],
        compiler_params=_cparams(1, 32),
        name="fox_fgate",
    )(h, fgate_wt, fb, tri)


def _attn_kernel(flag_ref, q_ref, k_ref, v_ref, ck_ref, b_ref, o_ref,
                 acc_ref, m_ref, vaug_ref, rt_ref, *, tq, heads):
    qi = pl.program_id(1)
    S = k_ref.shape[0]
    hs = [slice(g * HEAD_DIM, (g + 1) * HEAD_DIM) for g in range(heads)]
    acc_ref[...] = jnp.zeros(acc_ref.shape, F32)

    @pl.when(qi == 0)
    def _():
        for g in range(heads):
            vaug_ref[g, :, 0:HEAD_DIM] = v_ref[:, hs[g]]
            vaug_ref[g, :, HEAD_DIM:] = jnp.ones((S, HEAD_DIM), BF16)

    def scores(g, ks, width):
        return lax.dot_general(q_ref[:, hs[g]], k_ref[pl.ds(ks, width), hs[g]],
                               (((1,), (1,)), ((), ())), preferred_element_type=F32)

    def visible(c):
        row = lax.broadcasted_iota(jnp.int32, (tq, LANES), 0)
        col = lax.broadcasted_iota(jnp.int32, (tq, LANES), 1) + c * LANES
        return col <= row

    def fast_step(ki, width, diag_at=None):
        ks = pl.multiple_of(ki * tq, tq)
        for g in range(heads):
            s = scores(g, ks, width)
            ckr = ck_ref[g, :, pl.ds(ks, width)]
            rt = rt_ref[g]
            ps = []
            for c in range(width // LANES):
                sl = slice(c * LANES, (c + 1) * LANES)
                t = s[:, sl] + (rt - ckr[:, sl])
                if diag_at is not None and c >= diag_at:
                    t = jnp.where(visible(c - diag_at), t, MASK_VALUE)
                ps.append(jnp.exp2(t).astype(BF16))
            p = jnp.concatenate(ps, axis=1)
            acc_ref[g] += _dot(p, vaug_ref[g, pl.ds(ks, width), :])

    def slow_step(ki, width, diag_at=None):
        ks = pl.multiple_of(ki * tq, tq)
        for g in range(heads):
            t = scores(g, ks, width) - ck_ref[g, :, pl.ds(ks, width)]
            if diag_at is not None:
                t = jnp.concatenate(
                    [t[:, c * LANES:(c + 1) * LANES] if c < diag_at else
                     jnp.where(visible(c - diag_at), t[:, c * LANES:(c + 1) * LANES], MASK_VALUE)
                     for c in range(width // LANES)], axis=1)
            m_prev = m_ref[g]
            m_new = jnp.maximum(m_prev, jnp.max(t, axis=-1, keepdims=True))
            alpha = jnp.exp2(m_prev - m_new)
            p = jnp.exp2(t - m_new[:, 0:1])
            pv = _dot(p.astype(BF16), vaug_ref[g, pl.ds(ks, width), :])
            acc_ref[g] = acc_ref[g] * jnp.concatenate([alpha, alpha], axis=1) + pv
            m_ref[g] = m_new

    def sweep(step):
        def body(kk, carry):
            step(4 * kk, 4 * tq)
            return carry
        lax.fori_loop(0, qi // 4, body, 0)
        rem = qi % 4

        @pl.when(rem >= 2)
        def _():
            step(qi - rem, 2 * tq)

        @pl.when(rem % 2 == 1)
        def _():
            step(qi - 1, 2 * tq, diag_at=tq // LANES)

        @pl.when(rem % 2 == 0)
        def _():
            step(qi, tq, diag_at=0)

    @pl.when(flag_ref[0] == 1)
    def _():
        qs = pl.multiple_of(qi * tq, tq)
        for g in range(heads):
            ckq = jnp.broadcast_to(ck_ref[g, :, pl.ds(qs, tq)], (LANES, tq))
            rt_ref[g] = ckq.T - b_ref[...]
        sweep(fast_step)

    @pl.when(flag_ref[0] != 1)
    def _():
        m_ref[...] = jnp.full(m_ref.shape, MASK_VALUE, F32)
        sweep(slow_step)

    for g in range(heads):
        acc = acc_ref[g]
        o_ref[:, hs[g]] = (acc[:, 0:HEAD_DIM] / acc[:, HEAD_DIM:]).astype(o_ref.dtype)


def _attention(qkv, ck2, bound):
    S = qkv.shape[0]
    H = ck2.shape[0]
    tq, heads = 512, 2
    hw = heads * HEAD_DIM
    flag = (bound <= ATTN_FAST_MAX_BOUND).astype(jnp.int32).reshape(1)
    bvec = jnp.broadcast_to(bound.astype(F32), (1, LANES))
    grid_spec = pltpu.PrefetchScalarGridSpec(
        num_scalar_prefetch=1,
        grid=(H // heads, S // tq),
        in_specs=[pl.BlockSpec((tq, hw), lambda h, i, f: (i, h)),
                  pl.BlockSpec((S, hw), lambda h, i, f: (0, H // heads + h)),
                  pl.BlockSpec((S, hw), lambda h, i, f: (0, 2 * (H // heads) + h)),
                  pl.BlockSpec((heads, 1, S), lambda h, i, f: (h, 0, 0)),
                  pl.BlockSpec((1, LANES), lambda h, i, f: (0, 0))],
        out_specs=pl.BlockSpec((tq, hw), lambda h, i, f: (i, h)),
        scratch_shapes=[pltpu.VMEM((heads, tq, 2 * HEAD_DIM), F32),
                        pltpu.VMEM((heads, tq, HEAD_DIM), F32),
                        pltpu.VMEM((heads, S, 2 * HEAD_DIM), BF16),
                        pltpu.VMEM((heads, tq, LANES), F32)])
    return pl.pallas_call(
        functools.partial(_attn_kernel, tq=tq, heads=heads),
        grid_spec=grid_spec,
        out_shape=jax.ShapeDtypeStruct((S, H * HEAD_DIM), BF16),
        compiler_params=_cparams(2, BIG_VMEM_MB),
        name="fox_attn",
    )(flag, qkv, qkv, qkv, ck2, bvec)


def _row_copy(src_hbm, dst, src_row, dst_row, sem):
    return pltpu.make_async_copy(src_hbm.at[pl.ds(src_row, 1), :], dst.at[pl.ds(dst_row, 1), :], sem)


def _gathered_rows(idx_ref, src_hbm, buf, sem, tm, n_per_row):
    i = pl.program_id(0)

    def start_copies(step, slot):
        def body(half, c):
            for odd in range(2):
                r = 2 * half + odd
                for k in range(n_per_row):
                    src_row = idx_ref[(step * tm + r) * n_per_row + k]
                    _row_copy(src_hbm, buf.at[slot, k], src_row, r, sem.at[slot]).start(priority=(odd + k) % 2)
            return c
        lax.fori_loop(0, tm // 2, body, 0, unroll=4)

    @pl.when(i == 0)
    def _():
        start_copies(0, 0)

    @pl.when(i + 1 < pl.num_programs(0))
    def _():
        start_copies(i + 1, (i + 1) % 2)

    slot = i % 2
    for k in range(n_per_row):
        pltpu.make_async_copy(src_hbm.at[pl.ds(0, tm), :], buf.at[slot, k], sem.at[slot]).wait()
    return slot


def _gather_norm_kernel(tok_ref, x_hbm, g_ref, sc_ref, sh_ref, o_ref, buf, sem, *, tm):
    slot = _gathered_rows(tok_ref, x_hbm, buf, sem, tm, 1)
    o_ref[...] = _norm_modulate(buf[slot, 0], g_ref[0], sc_ref[0], sh_ref[0]).astype(o_ref.dtype)


def _gather_norm(tok_sorted, x, norm_g, mod, layer):
    S, D = x.shape
    n_rows = tok_sorted.shape[0]
    tm = 256
    grid_spec = pltpu.PrefetchScalarGridSpec(
        num_scalar_prefetch=1,
        grid=(n_rows // tm,),
        in_specs=[pl.BlockSpec(memory_space=pl.ANY),
                  pl.BlockSpec((1, 1, D), lambda i, t: (layer, 0, 0)),
                  pl.BlockSpec((1, 1, D), lambda i, t: (layer, 0, 4)),
                  pl.BlockSpec((1, 1, D), lambda i, t: (layer, 0, 3))],
        out_specs=pl.BlockSpec((tm, D), lambda i, t: (i, 0)),
        scratch_shapes=[pltpu.VMEM((2, 1, tm, D), F32), pltpu.SemaphoreType.DMA((2,))])
    return pl.pallas_call(
        functools.partial(_gather_norm_kernel, tm=tm),
        grid_spec=grid_spec,
        out_shape=jax.ShapeDtypeStruct((n_rows, D), BF16),
        compiler_params=_cparams(1, 32),
        name="moe_gather",
    )(tok_sorted, x, norm_g, mod, mod)


def _grouped_matmul_item(item_refs, a_ref, o_ref, w_hbms, stages, caches, sems, layer, product):
    tile_ref, exp_ref, kind_ref, lo_ref, hi_ref, nxt_ref, lastrun_ref, runpar_ref, runsodd_ref = item_refs
    n, i = pl.program_id(0), pl.program_id(1)
    tm, tn = o_ref.shape
    tile, lo, hi, kind = tile_ref[i], lo_ref[i], hi_ref[i], kind_ref[i]
    run_start = (i == 0) | (exp_ref[i] != exp_ref[jnp.maximum(i - 1, 0)])
    sub_blocks = [(slice(sb * MOE_SUB, (sb + 1) * MOE_SUB), tile * tm + sb * MOE_SUB) for sb in range(tm // MOE_SUB)]
    two_slots = stages[0].shape[0] == 2
    slot = (n * runsodd_ref[0] + runpar_ref[i]) % 2 if two_slots else 0
    next_slot = 1 - slot if two_slots else 0

    def copy(k, e, col_tile, s):
        cols = pl.ds(pl.multiple_of(col_tile * tn, tn), tn)
        return pltpu.make_async_copy(w_hbms[k].at[layer, e, :, cols], stages[k].at[s], sems.at[k, s])

    @pl.when((n == 0) & (i == 0))
    def _():
        for k in range(len(w_hbms)):
            copy(k, exp_ref[0], 0, 0).start(priority=1)

    @pl.when(run_start)
    def _():
        nn = jnp.where(lastrun_ref[i] == 1, n + 1, n)
        for k in range(len(w_hbms)):
            copy(k, 0, 0, slot).wait()
            if not two_slots:
                caches[k][...] = stages[k][0].astype(BF16)

            @pl.when(nn < pl.num_programs(0))
            def _(k=k):
                copy(k, nxt_ref[i], nn, next_slot).start(priority=1)

    def rows(rs, fresh):
        if fresh:
            ws = [stage[slot].astype(BF16) for stage in stages]
            for cache, w in zip(caches, ws):
                cache[...] = w
        else:
            ws = [cache[...] for cache in caches]
        return product(a_ref[rs, :], ws).astype(o_ref.dtype)

    whole = (lo <= tile * tm) & (hi >= (tile + 1) * tm)
    for fresh in ((True, False) if two_slots else (False,)):
        mine = kind == 1
        if two_slots:
            mine = mine & (run_start if fresh else jnp.logical_not(run_start))

        @pl.when(mine & whole)
        def _(fresh=fresh):
            o_ref[...] = rows(slice(0, tm), fresh)

        for rs, start in sub_blocks:
            @pl.when(mine & jnp.logical_not(whole) & (lo <= start) & (start < hi))
            def _(rs=rs, fresh=fresh):
                o_ref[rs, :] = rows(rs, fresh)

    for rs, start in sub_blocks:
        @pl.when((kind == 2) & (lo <= start) & (start < hi))
        def _(rs=rs):
            o_ref[rs, :] = jnp.zeros((MOE_SUB, tn), o_ref.dtype)


def _moe_gu_kernel(*refs, layer):
    item_refs, (a_ref, wg_hbm, wu_hbm, o_ref, wg_stage, wu_stage, wgb, wub, sems) = refs[:9], refs[10:]
    _grouped_matmul_item(item_refs, a_ref, o_ref, (wg_hbm, wu_hbm), (wg_stage, wu_stage), (wgb, wub), sems, layer,
                         lambda a, ws: _silu(_dot(a, ws[0])) * _dot(a, ws[1]))


def _moe_gate_up(items, xs, gate_w, up_w, j):
    n_rows, D = xs.shape
    F = gate_w.shape[-1]
    tm, tn = MOE_TM, 1024
    n_items = items[0].shape[0]
    grid_spec = pltpu.PrefetchScalarGridSpec(
        num_scalar_prefetch=len(items),
        grid=(F // tn, n_items),
        in_specs=[pl.BlockSpec((tm, D), lambda n, i, *refs: (refs[-1][i], 0)),
                  pl.BlockSpec(memory_space=pl.ANY), pl.BlockSpec(memory_space=pl.ANY)],
        out_specs=pl.BlockSpec((tm, tn), lambda n, i, tile, *_: (tile[i], n)),
        scratch_shapes=[pltpu.VMEM((2, D, tn), F32), pltpu.VMEM((2, D, tn), F32),
                        pltpu.VMEM((D, tn), BF16), pltpu.VMEM((D, tn), BF16),
                        pltpu.SemaphoreType.DMA((2, 2))])
    return pl.pallas_call(
        functools.partial(_moe_gu_kernel, layer=j),
        grid_spec=grid_spec,
        out_shape=jax.ShapeDtypeStruct((n_rows, F), BF16),
        compiler_params=_cparams(2, BIG_VMEM_MB),
        name="moe_gate_up",
    )(*items, xs, gate_w, up_w)


def _moe_down_kernel(*refs, layer):
    item_refs, (a_ref, w_hbm, o_ref, w_stage, wb, sems) = refs[:9], refs[10:]
    _grouped_matmul_item(item_refs, a_ref, o_ref, (w_hbm,), (w_stage,), (wb,), sems, layer,
                         lambda a, ws: _dot(a, ws[0]))


def _moe_down(items, hid, down_w, j):
    n_rows, F = hid.shape
    D = down_w.shape[-1]
    tm, tn = MOE_TM, 512
    n_items = items[0].shape[0]
    grid_spec = pltpu.PrefetchScalarGridSpec(
        num_scalar_prefetch=len(items),
        grid=(D // tn, n_items),
        in_specs=[pl.BlockSpec((tm, F), lambda n, i, *refs: (refs[-1][i], 0)),
                  pl.BlockSpec(memory_space=pl.ANY)],
        out_specs=pl.BlockSpec((tm, tn), lambda n, i, tile, *_: (tile[i], n)),
        scratch_shapes=[pltpu.VMEM((1, F, tn), F32), pltpu.VMEM((F, tn), BF16), pltpu.SemaphoreType.DMA((1, 1))])
    return pl.pallas_call(
        functools.partial(_moe_down_kernel, layer=j),
        grid_spec=grid_spec,
        out_shape=jax.ShapeDtypeStruct((n_rows, D), F32),
        compiler_params=_cparams(2, BIG_VMEM_MB),
        name="moe_down",
    )(*items, hid, down_w)


def _combined_rows(pos_ref, y_hbm, x_ref, g_ref, w_ref, buf, sem, tm):
    slot = _gathered_rows(pos_ref, y_hbm, buf, sem, tm, MOE_TOP_K)
    w = w_ref[...]
    y = buf[slot, 0] * w[:, 0:1] + buf[slot, 1] * w[:, 1:2]
    return x_ref[...] + g_ref[0] * y


def _combine_kernel(pos_ref, y_hbm, x_ref, g_ref, w_ref, o_ref, buf, sem, *, tm):
    o_ref[...] = _combined_rows(pos_ref, y_hbm, x_ref, g_ref, w_ref, buf, sem, tm)


def _combine_norm_kernel(pos_ref, y_hbm, x_ref, g_ref, w_ref, ng_ref, nsc_ref, nsh_ref, o_ref, h_ref, buf, sem,
                         *, tm):
    x_new = _combined_rows(pos_ref, y_hbm, x_ref, g_ref, w_ref, buf, sem, tm)
    o_ref[...] = x_new
    h_ref[...] = _norm_modulate(x_new, ng_ref[0], nsc_ref[0], nsh_ref[0]).astype(h_ref.dtype)


def _combine(pos, y_sorted, x, mod, layer, top_w, next_norm_g=None):
    S, D = x.shape
    tm = 256
    row_spec = pl.BlockSpec((tm, D), lambda i, p: (i, 0))
    in_specs = [pl.BlockSpec(memory_space=pl.ANY),
                row_spec,
                pl.BlockSpec((1, 1, D), lambda i, p: (layer, 0, 5)),
                pl.BlockSpec((tm, MOE_TOP_K), lambda i, p: (i, 0))]
    args = (pos, y_sorted, x, mod, top_w)
    out_specs, out_shape, body = row_spec, jax.ShapeDtypeStruct((S, D), F32), _combine_kernel
    if next_norm_g is not None:
        in_specs += [pl.BlockSpec((1, 1, D), lambda i, p: (layer + 1, 0, 0)),
                     pl.BlockSpec((1, 1, D), lambda i, p: (layer + 1, 0, 1)),
                     pl.BlockSpec((1, 1, D), lambda i, p: (layer + 1, 0, 0))]
        args += (next_norm_g, mod, mod)
        out_specs = [row_spec, row_spec]
        out_shape = [out_shape, jax.ShapeDtypeStruct((S, D), BF16)]
        body = _combine_norm_kernel
    grid_spec = pltpu.PrefetchScalarGridSpec(
        num_scalar_prefetch=1,
        grid=(S // tm,),
        in_specs=in_specs,
        out_specs=out_specs,
        scratch_shapes=[pltpu.VMEM((2, MOE_TOP_K, tm, D), F32), pltpu.SemaphoreType.DMA((2,))])
    return pl.pallas_call(
        functools.partial(body, tm=tm),
        grid_spec=grid_spec,
        out_shape=out_shape,
        compiler_params=_cparams(1, 40),
        name="moe_combine",
    )(*args)


def _work_items(starts, ends, counts, n_rows, tm):
    n_experts = counts.shape[0]
    starts = jnp.concatenate([starts, ends[-1:]])
    ends = jnp.concatenate([ends, jnp.full((1,), n_rows, ends.dtype)])
    counts = jnp.concatenate([counts, n_rows - ends[-2:-1]])
    n_items = n_rows // tm + n_experts
    first_tile = starts // tm
    n_e = jnp.where(counts > 0, (ends + tm - 1) // tm - first_tile, 0)
    item_end = jnp.cumsum(n_e)
    item_start = item_end - n_e
    total = item_end[-1]
    idx = jnp.arange(n_items, dtype=jnp.int32)
    idx_c = jnp.minimum(idx, total - 1)
    item_r = jnp.minimum(jnp.sum((idx_c[:, None] >= item_end[None, :]).astype(jnp.int32), axis=1), n_experts)
    item_tile = (first_tile[item_r] + idx_c - item_start[item_r]).astype(jnp.int32)
    kind = jnp.where(idx < total, jnp.where(item_r == n_experts, 2, 1), 0).astype(jnp.int32)
    last_expert = jnp.max(jnp.where(counts[:n_experts] > 0, jnp.arange(n_experts, dtype=jnp.int32), 0))
    item_e = jnp.where(item_r == n_experts, last_expert, item_r).astype(jnp.int32)
    in_tile = jnp.where(item_r == n_experts, (ends[n_experts - 1] - 1) // tm, item_tile).astype(jnp.int32)
    run_start = jnp.concatenate([jnp.ones((1,), bool), item_e[1:] != item_e[:-1]])
    next_start = lax.cummin(jnp.where(run_start, idx, n_items), reverse=True)
    next_start = jnp.concatenate([next_start[1:], jnp.full((1,), n_items, jnp.int32)])
    last_run = next_start >= n_items
    next_e = jnp.where(last_run, item_e[0], item_e[jnp.minimum(next_start, n_items - 1)])
    run_idx = jnp.cumsum(run_start.astype(jnp.int32)) - 1
    return (item_tile, item_e, kind, starts[item_r].astype(jnp.int32), ends[item_r].astype(jnp.int32),
            next_e.astype(jnp.int32), last_run.astype(jnp.int32), (run_idx % 2).astype(jnp.int32),
            ((run_idx[-1:] + 1) % 2).astype(jnp.int32), in_tile)


def _route(logits, n_experts):
    S = logits.shape[0]
    top_logit, top_idx = lax.top_k(logits, MOE_TOP_K)
    top_w = jax.nn.softmax(top_logit, axis=-1)
    n_assign = S * MOE_TOP_K
    n_rows = n_assign + n_experts * MOE_SUB
    flat_e = top_idx.reshape(n_assign).astype(jnp.int32)
    flat_idx = jnp.arange(n_assign, dtype=jnp.int32)
    counts = jnp.sum((flat_e[:, None] == jnp.arange(n_experts, dtype=jnp.int32)[None, :]).astype(jnp.int32), axis=0)
    dense_starts = jnp.cumsum(counts) - counts
    padded = (counts + MOE_SUB - 1) // MOE_SUB * MOE_SUB
    ends = jnp.cumsum(padded)
    starts = ends - padded
    order = jnp.sort(flat_e * n_assign + flat_idx) % n_assign
    _, dense_row = lax.sort((order, flat_idx), num_keys=1)
    pos = (starts[flat_e] + dense_row - dense_starts[flat_e]).astype(jnp.int32)
    tok_dense = order // MOE_TOP_K
    row = jnp.arange(n_rows, dtype=jnp.int32)
    row_e = jnp.minimum(jnp.sum((row[:, None] >= ends[None, :]).astype(jnp.int32), axis=1), n_experts - 1)
    offset = row - starts[row_e]
    is_real = offset < counts[row_e]
    tok_rows = jnp.where(is_real, tok_dense[jnp.clip(dense_starts[row_e] + offset, 0, n_assign - 1)], row % S)
    return tok_rows.astype(jnp.int32), pos, top_w, (starts, ends, padded)


def kernel(x, c, ada_w, ada_b, mix_norm, ffn_norm, conv_pw1_w, conv_pw1_b, conv_dw_w, conv_dw_b, conv_ln_g, conv_ln_b, conv_pw2_w, conv_pw2_b, fox_qkv_w, fox_o_w, fox_fgate_w, fox_fgate_b, fox_q_norm, fox_k_norm, ffn_gate_w, ffn_up_w, ffn_down_w, moe_router_w, moe_router_b, moe_gate_w, moe_up_w, moe_down_w):
    B, S, D = x.shape
    assert B == 1, "kernels are written for a single sequence"
    L = ada_w.shape[0]
    H = fox_fgate_w.shape[-1]
    E = moe_router_w.shape[-1]

    def as_rows(v):
        return v.reshape(v.shape[0], 1, v.shape[1])

    mod = _adaln(c.reshape(D, 1), ada_w, ada_b)
    mix_g, ffn_g = as_rows(mix_norm), as_rows(ffn_norm)
    zero_bias = jnp.zeros((1, 1, D), F32)
    xs = x.reshape(S, D)

    h_next = None
    for i in range(L):
        j = i // 2
        h = _norm_mod(xs, mix_g, mod, i, 0) if h_next is None else h_next
        h_next = None
        if i % 2 == 0:
            u = _pw1_glu(h, conv_pw1_w, as_rows(conv_pw1_b), j)
            v = _conv_ln(u, conv_dw_w, as_rows(conv_dw_b), as_rows(conv_ln_g), as_rows(conv_ln_b), j)
            xs, h = _mixer_out(v, conv_pw2_w, as_rows(conv_pw2_b), xs, mod, ffn_g, j, i, "conv_pw2")
            hid = _gate_up(h, ffn_gate_w, ffn_up_w, j)
            xs = _mm_res(hid, ffn_down_w, zero_bias, xs, mod, j, i, 5, 1024, 256, "ffn_down")
        else:
            q_gain = fox_q_norm[j] * (HEAD_DIM ** -0.5 * LOG2E)
            gain = jnp.concatenate([jnp.tile(q_gain, H), jnp.tile(fox_k_norm[j], H),
                                    jnp.ones((D,), F32)])[None, :]
            qkv = _qkv(h, fox_qkv_w, gain, j)
            cum = _fgate_cum(h, jnp.swapaxes(fox_fgate_w, 1, 2), fox_fgate_b, j)
            bound = 1.02 * HEAD_DIM * jnp.max(jnp.abs(q_gain)) * jnp.max(jnp.abs(fox_k_norm[j]))
            o = _attention(qkv, (cum * LOG2E).reshape(H, 1, S), bound)
            rw_pad = jnp.zeros((D, LANES), F32).at[:, :E].set(moe_router_w[j])
            rb_pad = jnp.zeros((1, LANES), F32).at[0, :E].set(moe_router_b[j])
            xs, logits = _mixer_out(o, fox_o_w, zero_bias, xs, mod, ffn_g, j, i, "fox_o", router=(rw_pad, rb_pad))
            tok_rows, pos, top_w, ranges = _route(logits[:, :E], E)
            items = _work_items(*ranges, tok_rows.shape[0], MOE_TM)
            xg = _gather_norm(tok_rows, xs, ffn_g, mod, i)
            hid = _moe_gate_up(items, xg, moe_gate_w, moe_up_w, j)
            y_sorted = _moe_down(items, hid, moe_down_w, j)
            if i + 1 < L:
                xs, h_next = _combine(pos, y_sorted, xs, mod, i, top_w, mix_g)
            else:
                xs = _combine(pos, y_sorted, xs, mod, i, top_w)
    return xs.reshape(B, S, D)
```

```python
import functools

import jax
import jax.numpy as jnp
from jax import lax
from jax.experimental import pallas as pl
from jax.experimental.pallas import tpu as pltpu

F32 = jnp.float32
BF16 = jnp.bfloat16

NORM_EPS = 1e-6
HEAD_DIM = 128
CONV_WIDTH = 31
CONV_HALO = 32
MOE_TOP_K = 2
LOG2E = 1.4426950408889634
MASK_VALUE = -1e30
ATTN_FAST_MAX_BOUND = 40.0
LANES = 128
SUBLANES = 8
BIG_VMEM_MB = 57
DENSE_TM = 2048
MOE_SUB = 256
MOE_TM = 512


def _cparams(n_axes, vmem_mb):
    return pltpu.CompilerParams(
        dimension_semantics=("arbitrary",) * n_axes,
        vmem_limit_bytes=vmem_mb << 20)


def _norm_modulate(x, g, sc, sh):
    ms = jnp.mean(x * x, axis=-1, keepdims=True)
    y = x * lax.rsqrt(ms + NORM_EPS) * g
    return y * (1.0 + sc) + sh


def _silu(x):
    return x * jax.nn.sigmoid(x)


def _adaln_kernel(c_ref, w_ref, b_ref, o_ref):
    c = c_ref[...]
    o_ref[0] = jnp.sum(w_ref[0] * _silu(c), axis=0, keepdims=True) + b_ref[0]


def _adaln(c_col, ada_w, ada_b):
    L, D, N = ada_w.shape
    tn = 1024
    return pl.pallas_call(
        _adaln_kernel,
        grid=(L, N // tn),
        in_specs=[pl.BlockSpec((D, 1), lambda l, j: (0, 0)),
                  pl.BlockSpec((1, D, tn), lambda l, j: (l, 0, j)),
                  pl.BlockSpec((1, 1, tn), lambda l, j: (l, 0, j))],
        out_specs=pl.BlockSpec((1, 1, tn), lambda l, j: (l, 0, j)),
        out_shape=jax.ShapeDtypeStruct((L, 1, N), F32),
        compiler_params=_cparams(2, 40),
        name="adaln",
    )(c_col, ada_w, ada_b.reshape(L, 1, N))


def _mod_spec(layer, which, D):
    return pl.BlockSpec((1, 1, D), lambda *_: (layer, 0, which))


def _vec_spec(layer, D):
    return pl.BlockSpec((1, 1, D), lambda *_: (layer, 0, 0))


def _norm_mod_kernel(x_ref, g_ref, sc_ref, sh_ref, o_ref):
    o_ref[...] = _norm_modulate(x_ref[...], g_ref[0], sc_ref[0], sh_ref[0]).astype(o_ref.dtype)


def _norm_mod(x, norm_g, mod, layer, which_shift):
    S, D = x.shape
    tm = 512
    return pl.pallas_call(
        _norm_mod_kernel,
        grid=(S // tm,),
        in_specs=[pl.BlockSpec((tm, D), lambda i: (i, 0)),
                  _vec_spec(layer, D),
                  _mod_spec(layer, which_shift + 1, D),
                  _mod_spec(layer, which_shift, D)],
        out_specs=pl.BlockSpec((tm, D), lambda i: (i, 0)),
        out_shape=jax.ShapeDtypeStruct((S, D), BF16),
        compiler_params=_cparams(1, 32),
        name="norm_mod",
    )(x, norm_g, mod, mod)


def _dot(a, w):
    return jnp.dot(a, w, preferred_element_type=F32)


def _pw1_glu_kernel(a_ref, wv_ref, wg_ref, bv_ref, bg_ref, o_ref):
    a = a_ref[...]
    val = _dot(a, wv_ref[...].astype(BF16)) + bv_ref[0]
    gate = _dot(a, wg_ref[...].astype(BF16)) + bg_ref[0]
    o_ref[...] = val * jax.nn.sigmoid(gate)


def _pw1_glu(h, pw1_w, pw1_b, j):
    S, D = h.shape
    tm, tn = DENSE_TM, 512
    nt = D // tn
    return pl.pallas_call(
        _pw1_glu_kernel,
        grid=(S // tm, nt),
        in_specs=[pl.BlockSpec((tm, D), lambda i, n: (i, 0)),
                  pl.BlockSpec((None, D, tn), lambda i, n: (j, 0, n)),
                  pl.BlockSpec((None, D, tn), lambda i, n: (j, 0, n + nt)),
                  pl.BlockSpec((1, 1, tn), lambda i, n: (j, 0, n)),
                  pl.BlockSpec((1, 1, tn), lambda i, n: (j, 0, n + nt))],
        out_specs=pl.BlockSpec((tm, tn), lambda i, n: (i, n)),
        out_shape=jax.ShapeDtypeStruct((S, D), F32),
        compiler_params=_cparams(2, BIG_VMEM_MB),
        name="pw1_glu",
    )(h, pw1_w, pw1_w, pw1_b, pw1_b)


def _gate_up_kernel(a_ref, wg_ref, wu_ref, o_ref):
    a = a_ref[...]
    g = _dot(a, wg_ref[...].astype(BF16))
    u = _dot(a, wu_ref[...].astype(BF16))
    o_ref[...] = (_silu(g) * u).astype(o_ref.dtype)


def _gate_up(h, gate_w, up_w, j):
    S, D = h.shape
    F = gate_w.shape[-1]
    tm, tn = DENSE_TM, 512
    return pl.pallas_call(
        _gate_up_kernel,
        grid=(S // tm, F // tn),
        in_specs=[pl.BlockSpec((tm, D), lambda i, n: (i, 0)),
                  pl.BlockSpec((None, D, tn), lambda i, n: (j, 0, n)),
                  pl.BlockSpec((None, D, tn), lambda i, n: (j, 0, n))],
        out_specs=pl.BlockSpec((tm, tn), lambda i, n: (i, n)),
        out_shape=jax.ShapeDtypeStruct((S, F), BF16),
        compiler_params=_cparams(2, BIG_VMEM_MB),
        name="ffn_gate_up",
    )(h, gate_w, up_w)


def _mm_res_kernel(a_ref, w_ref, b_ref, x_ref, g_ref, o_ref):
    y = _dot(a_ref[...], w_ref[...].astype(BF16)) + b_ref[0]
    o_ref[...] = x_ref[...] + g_ref[0] * y


def _mm_res(a, w, bias, x, mod, j, layer, which_gate, tm, tn, name):
    S, K = a.shape
    D = w.shape[-1]
    bj = j if bias.shape[0] > 1 else 0
    return pl.pallas_call(
        _mm_res_kernel,
        grid=(S // tm, D // tn),
        in_specs=[pl.BlockSpec((tm, K), lambda i, n: (i, 0)),
                  pl.BlockSpec((None, K, tn), lambda i, n: (j, 0, n)),
                  pl.BlockSpec((1, 1, tn), lambda i, n: (bj, 0, n)),
                  pl.BlockSpec((tm, tn), lambda i, n: (i, n)),
                  pl.BlockSpec((1, 1, tn), lambda i, n: (layer, 0, which_gate * (D // tn) + n))],
        out_specs=pl.BlockSpec((tm, tn), lambda i, n: (i, n)),
        out_shape=jax.ShapeDtypeStruct((S, D), F32),
        compiler_params=_cparams(2, BIG_VMEM_MB),
        name=name,
    )(a, w, bias, x, mod)


def _router_logits(h, rw, rb):
    h_hi, w_hi = h.astype(BF16), rw.astype(BF16)
    h_lo = (h - h_hi.astype(F32)).astype(BF16)
    w_lo = (rw - w_hi.astype(F32)).astype(BF16)
    return _dot(h_hi, w_hi) + (_dot(h_hi, w_lo) + _dot(h_lo, w_hi)) + rb


def _mixer_out_kernel(a_ref, w_hbm, b_ref, x_ref, gate_ref, ng_ref, nsc_ref, nsh_ref, *rest, j, router):
    if router:
        rw_ref, rb_ref, o_ref, aux_ref, w_stage, wb, sem = rest
    else:
        o_ref, aux_ref, w_stage, wb, sem = rest

    @pl.when(pl.program_id(0) == 0)
    def _():
        width = w_stage.shape[1]
        for c in range(wb.shape[1] // width):
            cols = slice(c * width, (c + 1) * width)
            cp = pltpu.make_async_copy(w_hbm.at[j, :, cols], w_stage, sem.at[0])
            cp.start()
            cp.wait()
            wb[:, cols] = w_stage[...].astype(BF16)

    y = _dot(a_ref[...], wb[...]) + b_ref[0]
    x_new = x_ref[...] + gate_ref[0] * y
    o_ref[...] = x_new
    h = _norm_modulate(x_new, ng_ref[0], nsc_ref[0], nsh_ref[0])
    if router:
        aux_ref[...] = _router_logits(h, rw_ref[...], rb_ref[...])
    else:
        aux_ref[...] = h.astype(aux_ref.dtype)


def _mixer_out(a, w, bias, x, mod, norm_g, j, layer, name, router=None):
    S, K = a.shape
    D = w.shape[-1]
    tm = 512
    bj = j if bias.shape[0] > 1 else 0
    row = lambda width: pl.BlockSpec((tm, width), lambda i: (i, 0))
    in_specs = [row(K), pl.BlockSpec(memory_space=pl.ANY), _vec_spec(bj, D), row(D), _mod_spec(layer, 2, D),
                _vec_spec(layer, D), _mod_spec(layer, 4, D), _mod_spec(layer, 3, D)]
    args = (a, w, bias, x, mod, norm_g, mod, mod)
    if router is None:
        aux_spec, aux_shape = row(D), jax.ShapeDtypeStruct((S, D), BF16)
    else:
        in_specs += [pl.BlockSpec((D, LANES), lambda i: (0, 0)), pl.BlockSpec((1, LANES), lambda i: (0, 0))]
        args += router
        aux_spec, aux_shape = row(LANES), jax.ShapeDtypeStruct((S, LANES), F32)
    return pl.pallas_call(
        functools.partial(_mixer_out_kernel, j=j, router=router is not None),
        grid=(S // tm,),
        in_specs=in_specs,
        out_specs=[row(D), aux_spec],
        out_shape=[jax.ShapeDtypeStruct((S, D), F32), aux_shape],
        scratch_shapes=[pltpu.VMEM((K, D // 2), F32), pltpu.VMEM((K, D), BF16), pltpu.SemaphoreType.DMA((1,))],
        compiler_params=_cparams(1, BIG_VMEM_MB),
        name=name,
    )(*args)


def _conv_ln_kernel(halo_ref, u_ref, w_ref, b_ref, g_ref, beta_ref, o_ref, win_ref, acc_ref, *, tm):
    i = pl.program_id(0)

    @pl.when(i == 0)
    def _():
        win_ref[0:CONV_HALO, :] = jnp.zeros((CONV_HALO, win_ref.shape[1]), F32)

    @pl.when(i > 0)
    def _():
        win_ref[0:CONV_HALO, :] = halo_ref[...]

    win_ref[CONV_HALO:CONV_HALO + tm, :] = u_ref[...]
    first = CONV_HALO - (CONV_WIDTH - 1)
    for c in range(u_ref.shape[1] // LANES):
        cs = slice(c * LANES, (c + 1) * LANES)
        out = None
        for b in range(SUBLANES):
            rows = tm if b == 0 else tm + SUBLANES
            y = None
            for j in range(first, first + CONV_WIDTH):
                if j % SUBLANES != b:
                    continue
                term = win_ref[j - b:j - b + rows, cs] * w_ref[0, j - first:j - first + 1, cs]
                y = term if y is None else y + term
            yb = y if b == 0 else y[b:b + tm]
            out = yb if out is None else out + yb
        acc_ref[:, cs] = out + b_ref[0][:, cs]
    y = acc_ref[...]
    mu = jnp.mean(y, axis=-1, keepdims=True)
    yc = y - mu
    var = jnp.mean(yc * yc, axis=-1, keepdims=True)
    z = yc * lax.rsqrt(var + NORM_EPS) * g_ref[0] + beta_ref[0]
    o_ref[...] = _silu(z).astype(o_ref.dtype)


def _conv_ln(u, dw_w, dw_b, ln_g, ln_b, j):
    S, D = u.shape
    tm = 128
    hb = tm // CONV_HALO
    return pl.pallas_call(
        functools.partial(_conv_ln_kernel, tm=tm),
        grid=(S // tm,),
        in_specs=[pl.BlockSpec((CONV_HALO, D), lambda i: (jnp.maximum(i * hb - 1, 0), 0)),
                  pl.BlockSpec((tm, D), lambda i: (i, 0)),
                  pl.BlockSpec((1, CONV_WIDTH, D), lambda i: (j, 0, 0)),
                  _vec_spec(j, D), _vec_spec(j, D), _vec_spec(j, D)],
        out_specs=pl.BlockSpec((tm, D), lambda i: (i, 0)),
        out_shape=jax.ShapeDtypeStruct((S, D), BF16),
        scratch_shapes=[pltpu.VMEM((CONV_HALO + tm, D), F32), pltpu.VMEM((tm, D), F32)],
        compiler_params=_cparams(1, 32),
        name="conv_ln",
    )(u, u, dw_w, dw_b, ln_g, ln_b)


def _qkv_kernel(a_ref, w_ref, gain_ref, o_ref, *, n_norm_tiles):
    is_norm_tile = pl.program_id(1) < n_norm_tiles
    acc = _dot(a_ref[...], w_ref[...].astype(BF16))
    gain = gain_ref[...]
    for h in range(acc.shape[1] // HEAD_DIM):
        sl = slice(h * HEAD_DIM, (h + 1) * HEAD_DIM)
        t = acc[:, sl]
        ms = jnp.mean(t * t, axis=-1, keepdims=True)
        r = jnp.where(is_norm_tile, lax.rsqrt(ms + NORM_EPS), 1.0)
        o_ref[:, sl] = (t * r * gain[:, sl]).astype(o_ref.dtype)


def _qkv(h, qkv_w, gain, j):
    S, D = h.shape
    N = qkv_w.shape[-1]
    tm, tn = DENSE_TM, 1024
    return pl.pallas_call(
        functools.partial(_qkv_kernel, n_norm_tiles=2 * D // tn),
        grid=(S // tm, N // tn),
        in_specs=[pl.BlockSpec((tm, D), lambda i, n: (i, 0)),
                  pl.BlockSpec((None, D, tn), lambda i, n: (j, 0, n)),
                  pl.BlockSpec((1, tn), lambda i, n: (0, n))],
        out_specs=pl.BlockSpec((tm, tn), lambda i, n: (i, n)),
        out_shape=jax.ShapeDtypeStruct((S, N), BF16),
        compiler_params=_cparams(2, BIG_VMEM_MB),
        name="fox_qkv",
    )(h, qkv_w, gain)


def _fgate_kernel(h_ref, fwt_ref, fb_ref, tri_ref, o_ref, carry_ref):
    i = pl.program_id(0)

    @pl.when(i == 0)
    def _():
        carry_ref[...] = jnp.zeros(carry_ref.shape, F32)

    z = lax.dot_general(fwt_ref[...].astype(BF16), h_ref[...], (((1,), (1,)), ((), ())),
                        preferred_element_type=F32) + fb_ref[...][:, 0:1]
    lf = jnp.minimum(z, 0.0) - jnp.log1p(jnp.exp(-jnp.abs(z)))
    hi = lf.astype(BF16)
    r1 = lf - hi.astype(F32)
    mid = r1.astype(BF16)
    lo = (r1 - mid.astype(F32)).astype(BF16)
    tri = tri_ref[...]
    cs = _dot(hi, tri) + _dot(mid, tri) + _dot(lo, tri) + carry_ref[...][:, 0:1]
    o_ref[...] = cs
    carry_ref[...] = jnp.broadcast_to(cs[:, cs.shape[1] - 1:], carry_ref.shape)


def _fgate_cum(h, fgate_wt, fgate_b, j):
    S, D = h.shape
    H = fgate_wt.shape[1]
    tm = 512
    tri = jnp.triu(jnp.ones((tm, tm), F32)).astype(BF16)
    fb = jnp.broadcast_to(fgate_b[j][:, None], (H, LANES))
    return pl.pallas_call(
        _fgate_kernel,
        grid=(S // tm,),
        in_specs=[pl.BlockSpec((tm, D), lambda i: (i, 0)),
                  pl.BlockSpec((None, H, D), lambda i: (j, 0, 0)),
                  pl.BlockSpec((H, LANES), lambda i: (0, 0)),
                  pl.BlockSpec((tm, tm), lambda i: (0, 0))],
        out_specs=pl.BlockSpec((H, tm), lambda i: (0, i)),
        out_shape=jax.ShapeDtypeStruct((H, S), F32),
        scratch_shapes=[pltpu.VMEM((H, LANES), F32)],
        compiler_params=_cparams(1, 32),
        name="fox_fgate",
    )(h, fgate_wt, fb, tri)


def _attn_kernel(flag_ref, q_ref, k_ref, v_ref, ck_ref, b_ref, o_ref,
                 acc_ref, m_ref, vaug_ref, rt_ref, *, tq, heads):
    qi = pl.program_id(1)
    S = k_ref.shape[0]
    hs = [slice(g * HEAD_DIM, (g + 1) * HEAD_DIM) for g in range(heads)]
    acc_ref[...] = jnp.zeros(acc_ref.shape, F32)

    @pl.when(qi == 0)
    def _():
        for g in range(heads):
            vaug_ref[g, :, 0:HEAD_DIM] = v_ref[:, hs[g]]
            vaug_ref[g, :, HEAD_DIM:] = jnp.ones((S, HEAD_DIM), BF16)

    def scores(g, ks, width):
        return lax.dot_general(q_ref[:, hs[g]], k_ref[pl.ds(ks, width), hs[g]],
                               (((1,), (1,)), ((), ())), preferred_element_type=F32)

    def visible(c):
        row = lax.broadcasted_iota(jnp.int32, (tq, LANES), 0)
        col = lax.broadcasted_iota(jnp.int32, (tq, LANES), 1) + c * LANES
        return col <= row

    def fast_step(ki, width, diag_at=None):
        ks = pl.multiple_of(ki * tq, tq)
        for g in range(heads):
            s = scores(g, ks, width)
            ckr = ck_ref[g, :, pl.ds(ks, width)]
            rt = rt_ref[g]
            ps = []
            for c in range(width // LANES):
                sl = slice(c * LANES, (c + 1) * LANES)
                t = s[:, sl] + (rt - ckr[:, sl])
                if diag_at is not None and c >= diag_at:
                    t = jnp.where(visible(c - diag_at), t, MASK_VALUE)
                ps.append(jnp.exp2(t).astype(BF16))
            p = jnp.concatenate(ps, axis=1)
            acc_ref[g] += _dot(p, vaug_ref[g, pl.ds(ks, width), :])

    def slow_step(ki, width, diag_at=None):
        ks = pl.multiple_of(ki * tq, tq)
        for g in range(heads):
            t = scores(g, ks, width) - ck_ref[g, :, pl.ds(ks, width)]
            if diag_at is not None:
                t = jnp.concatenate(
                    [t[:, c * LANES:(c + 1) * LANES] if c < diag_at else
                     jnp.where(visible(c - diag_at), t[:, c * LANES:(c + 1) * LANES], MASK_VALUE)
                     for c in range(width // LANES)], axis=1)
            m_prev = m_ref[g]
            m_new = jnp.maximum(m_prev, jnp.max(t, axis=-1, keepdims=True))
            alpha = jnp.exp2(m_prev - m_new)
            p = jnp.exp2(t - m_new[:, 0:1])
            pv = _dot(p.astype(BF16), vaug_ref[g, pl.ds(ks, width), :])
            acc_ref[g] = acc_ref[g] * jnp.concatenate([alpha, alpha], axis=1) + pv
            m_ref[g] = m_new

    def sweep(step):
        def body(kk, carry):
            step(4 * kk, 4 * tq)
            return carry
        lax.fori_loop(0, qi // 4, body, 0)
        rem = qi % 4

        @pl.when(rem >= 2)
        def _():
            step(qi - rem, 2 * tq)

        @pl.when(rem % 2 == 1)
        def _():
            step(qi - 1, 2 * tq, diag_at=tq // LANES)

        @pl.when(rem % 2 == 0)
        def _():
            step(qi, tq, diag_at=0)

    @pl.when(flag_ref[0] == 1)
    def _():
        qs = pl.multiple_of(qi * tq, tq)
        for g in range(heads):
            ckq = jnp.broadcast_to(ck_ref[g, :, pl.ds(qs, tq)], (LANES, tq))
            rt_ref[g] = ckq.T - b_ref[...]
        sweep(fast_step)

    @pl.when(flag_ref[0] != 1)
    def _():
        m_ref[...] = jnp.full(m_ref.shape, MASK_VALUE, F32)
        sweep(slow_step)

    for g in range(heads):
        acc = acc_ref[g]
        o_ref[:, hs[g]] = (acc[:, 0:HEAD_DIM] / acc[:, HEAD_DIM:]).astype(o_ref.dtype)


def _attention(qkv, ck2, bound):
    S = qkv.shape[0]
    H = ck2.shape[0]
    tq, heads = 512, 2
    hw = heads * HEAD_DIM
    flag = (bound <= ATTN_FAST_MAX_BOUND).astype(jnp.int32).reshape(1)
    bvec = jnp.broadcast_to(bound.astype(F32), (1, LANES))
    grid_spec = pltpu.PrefetchScalarGridSpec(
        num_scalar_prefetch=1,
        grid=(H // heads, S // tq),
        in_specs=[pl.BlockSpec((tq, hw), lambda h, i, f: (i, h)),
                  pl.BlockSpec((S, hw), lambda h, i, f: (0, H // heads + h)),
                  pl.BlockSpec((S, hw), lambda h, i, f: (0, 2 * (H // heads) + h)),
                  pl.BlockSpec((heads, 1, S), lambda h, i, f: (h, 0, 0)),
                  pl.BlockSpec((1, LANES), lambda h, i, f: (0, 0))],
        out_specs=pl.BlockSpec((tq, hw), lambda h, i, f: (i, h)),
        scratch_shapes=[pltpu.VMEM((heads, tq, 2 * HEAD_DIM), F32),
                        pltpu.VMEM((heads, tq, HEAD_DIM), F32),
                        pltpu.VMEM((heads, S, 2 * HEAD_DIM), BF16),
                        pltpu.VMEM((heads, tq, LANES), F32)])
    return pl.pallas_call(
        functools.partial(_attn_kernel, tq=tq, heads=heads),
        grid_spec=grid_spec,
        out_shape=jax.ShapeDtypeStruct((S, H * HEAD_DIM), BF16),
        compiler_params=_cparams(2, BIG_VMEM_MB),
        name="fox_attn",
    )(flag, qkv, qkv, qkv, ck2, bvec)


def _row_copy(src_hbm, dst, src_row, dst_row, sem):
    return pltpu.make_async_copy(src_hbm.at[pl.ds(src_row, 1), :], dst.at[pl.ds(dst_row, 1), :], sem)


def _gathered_rows(idx_ref, src_hbm, buf, sem, tm, n_per_row):
    i = pl.program_id(0)

    def start_copies(step, slot):
        def body(half, c):
            for odd in range(2):
                r = 2 * half + odd
                for k in range(n_per_row):
                    src_row = idx_ref[(step * tm + r) * n_per_row + k]
                    _row_copy(src_hbm, buf.at[slot, k], src_row, r, sem.at[slot]).start(priority=(odd + k) % 2)
            return c
        lax.fori_loop(0, tm // 2, body, 0, unroll=4)

    @pl.when(i == 0)
    def _():
        start_copies(0, 0)

    @pl.when(i + 1 < pl.num_programs(0))
    def _():
        start_copies(i + 1, (i + 1) % 2)

    slot = i % 2
    for k in range(n_per_row):
        pltpu.make_async_copy(src_hbm.at[pl.ds(0, tm), :], buf.at[slot, k], sem.at[slot]).wait()
    return slot


def _gather_norm_kernel(tok_ref, x_hbm, g_ref, sc_ref, sh_ref, o_ref, buf, sem, *, tm):
    slot = _gathered_rows(tok_ref, x_hbm, buf, sem, tm, 1)
    o_ref[...] = _norm_modulate(buf[slot, 0], g_ref[0], sc_ref[0], sh_ref[0]).astype(o_ref.dtype)


def _gather_norm(tok_sorted, x, norm_g, mod, layer):
    S, D = x.shape
    n_rows = tok_sorted.shape[0]
    tm = 256
    grid_spec = pltpu.PrefetchScalarGridSpec(
        num_scalar_prefetch=1,
        grid=(n_rows // tm,),
        in_specs=[pl.BlockSpec(memory_space=pl.ANY),
                  pl.BlockSpec((1, 1, D), lambda i, t: (layer, 0, 0)),
                  pl.BlockSpec((1, 1, D), lambda i, t: (layer, 0, 4)),
                  pl.BlockSpec((1, 1, D), lambda i, t: (layer, 0, 3))],
        out_specs=pl.BlockSpec((tm, D), lambda i, t: (i, 0)),
        scratch_shapes=[pltpu.VMEM((2, 1, tm, D), F32), pltpu.SemaphoreType.DMA((2,))])
    return pl.pallas_call(
        functools.partial(_gather_norm_kernel, tm=tm),
        grid_spec=grid_spec,
        out_shape=jax.ShapeDtypeStruct((n_rows, D), BF16),
        compiler_params=_cparams(1, 32),
        name="moe_gather",
    )(tok_sorted, x, norm_g, mod, mod)


def _grouped_matmul_item(item_refs, a_ref, o_ref, w_hbms, stages, caches, sems, layer, product):
    tile_ref, exp_ref, kind_ref, lo_ref, hi_ref, nxt_ref, lastrun_ref = item_refs
    n, i = pl.program_id(0), pl.program_id(1)
    tm, tn = o_ref.shape
    tile, lo, hi, kind = tile_ref[i], lo_ref[i], hi_ref[i], kind_ref[i]
    run_start = (i == 0) | (exp_ref[i] != exp_ref[jnp.maximum(i - 1, 0)])
    sub_blocks = [(slice(sb * MOE_SUB, (sb + 1) * MOE_SUB), tile * tm + sb * MOE_SUB) for sb in range(tm // MOE_SUB)]

    def copy(k, e, col_tile):
        cols = pl.ds(pl.multiple_of(col_tile * tn, tn), tn)
        return pltpu.make_async_copy(w_hbms[k].at[layer, e, :, cols], stages[k], sems.at[k])

    @pl.when((n == 0) & (i == 0))
    def _():
        for k in range(len(w_hbms)):
            copy(k, exp_ref[0], 0).start(priority=(k + 1) % 2)

    @pl.when(run_start)
    def _():
        nn = jnp.where(lastrun_ref[i] == 1, n + 1, n)
        for k in range(len(w_hbms)):
            copy(k, 0, 0).wait()
            caches[k][...] = stages[k][...].astype(BF16)

            @pl.when(nn < pl.num_programs(0))
            def _(k=k):
                copy(k, nxt_ref[i], nn).start(priority=(k + 1) % 2)

    def rows(rs):
        return product(a_ref[rs, :], [cache[...] for cache in caches]).astype(o_ref.dtype)

    whole = (lo <= tile * tm) & (hi >= (tile + 1) * tm)

    @pl.when((kind == 1) & whole)
    def _():
        o_ref[...] = rows(slice(0, tm))

    for rs, start in sub_blocks:
        @pl.when((kind == 1) & jnp.logical_not(whole) & (lo <= start) & (start < hi))
        def _(rs=rs):
            o_ref[rs, :] = rows(rs)

        @pl.when((kind == 2) & (lo <= start) & (start < hi))
        def _(rs=rs):
            o_ref[rs, :] = jnp.zeros((MOE_SUB, tn), o_ref.dtype)


def _moe_gu_kernel(tile_ref, exp_ref, kind_ref, lo_ref, hi_ref, nxt_ref, lastrun_ref, in_tile_ref,
                   a_ref, wg_hbm, wu_hbm, o_ref, wg_stage, wu_stage, wgb, wub, sems, *, layer):
    _grouped_matmul_item((tile_ref, exp_ref, kind_ref, lo_ref, hi_ref, nxt_ref, lastrun_ref), a_ref, o_ref,
                         (wg_hbm, wu_hbm), (wg_stage, wu_stage), (wgb, wub), sems, layer,
                         lambda a, ws: _silu(_dot(a, ws[0])) * _dot(a, ws[1]))


def _moe_gate_up(items, xs, gate_w, up_w, j):
    n_rows, D = xs.shape
    F = gate_w.shape[-1]
    tm, tn = MOE_TM, 1024
    n_items = items[0].shape[0]
    grid_spec = pltpu.PrefetchScalarGridSpec(
        num_scalar_prefetch=len(items),
        grid=(F // tn, n_items),
        in_specs=[pl.BlockSpec((tm, D), lambda n, i, *refs: (refs[-1][i], 0)),
                  pl.BlockSpec(memory_space=pl.ANY), pl.BlockSpec(memory_space=pl.ANY)],
        out_specs=pl.BlockSpec((tm, tn), lambda n, i, tile, *_: (tile[i], n)),
        scratch_shapes=[pltpu.VMEM((D, tn), F32), pltpu.VMEM((D, tn), F32),
                        pltpu.VMEM((D, tn), BF16), pltpu.VMEM((D, tn), BF16),
                        pltpu.SemaphoreType.DMA((2,))])
    return pl.pallas_call(
        functools.partial(_moe_gu_kernel, layer=j),
        grid_spec=grid_spec,
        out_shape=jax.ShapeDtypeStruct((n_rows, F), BF16),
        compiler_params=_cparams(2, BIG_VMEM_MB),
        name="moe_gate_up",
    )(*items, xs, gate_w, up_w)


def _moe_down_kernel(tile_ref, exp_ref, kind_ref, lo_ref, hi_ref, nxt_ref, lastrun_ref, in_tile_ref,
                     a_ref, w_hbm, o_ref, w_stage, wb, sems, *, layer):
    _grouped_matmul_item((tile_ref, exp_ref, kind_ref, lo_ref, hi_ref, nxt_ref, lastrun_ref), a_ref, o_ref,
                         (w_hbm,), (w_stage,), (wb,), sems, layer, lambda a, ws: _dot(a, ws[0]))


def _moe_down(items, hid, down_w, j):
    n_rows, F = hid.shape
    D = down_w.shape[-1]
    tm, tn = MOE_TM, 512
    n_items = items[0].shape[0]
    grid_spec = pltpu.PrefetchScalarGridSpec(
        num_scalar_prefetch=len(items),
        grid=(D // tn, n_items),
        in_specs=[pl.BlockSpec((tm, F), lambda n, i, *refs: (refs[-1][i], 0)),
                  pl.BlockSpec(memory_space=pl.ANY)],
        out_specs=pl.BlockSpec((tm, tn), lambda n, i, tile, *_: (tile[i], n)),
        scratch_shapes=[pltpu.VMEM((F, tn), F32), pltpu.VMEM((F, tn), BF16), pltpu.SemaphoreType.DMA((1,))])
    return pl.pallas_call(
        functools.partial(_moe_down_kernel, layer=j),
        grid_spec=grid_spec,
        out_shape=jax.ShapeDtypeStruct((n_rows, D), F32),
        compiler_params=_cparams(2, BIG_VMEM_MB),
        name="moe_down",
    )(*items, hid, down_w)


def _combined_rows(pos_ref, y_hbm, x_ref, g_ref, w_ref, buf, sem, tm):
    slot = _gathered_rows(pos_ref, y_hbm, buf, sem, tm, MOE_TOP_K)
    w = w_ref[...]
    y = buf[slot, 0] * w[:, 0:1] + buf[slot, 1] * w[:, 1:2]
    return x_ref[...] + g_ref[0] * y


def _combine_kernel(pos_ref, y_hbm, x_ref, g_ref, w_ref, o_ref, buf, sem, *, tm):
    o_ref[...] = _combined_rows(pos_ref, y_hbm, x_ref, g_ref, w_ref, buf, sem, tm)


def _combine_norm_kernel(pos_ref, y_hbm, x_ref, g_ref, w_ref, ng_ref, nsc_ref, nsh_ref, o_ref, h_ref, buf, sem,
                         *, tm):
    x_new = _combined_rows(pos_ref, y_hbm, x_ref, g_ref, w_ref, buf, sem, tm)
    o_ref[...] = x_new
    h_ref[...] = _norm_modulate(x_new, ng_ref[0], nsc_ref[0], nsh_ref[0]).astype(h_ref.dtype)


def _combine(pos, y_sorted, x, mod, layer, top_w, next_norm_g=None):
    S, D = x.shape
    tm = 256
    row_spec = pl.BlockSpec((tm, D), lambda i, p: (i, 0))
    in_specs = [pl.BlockSpec(memory_space=pl.ANY),
                row_spec,
                pl.BlockSpec((1, 1, D), lambda i, p: (layer, 0, 5)),
                pl.BlockSpec((tm, MOE_TOP_K), lambda i, p: (i, 0))]
    args = (pos, y_sorted, x, mod, top_w)
    out_specs, out_shape, body = row_spec, jax.ShapeDtypeStruct((S, D), F32), _combine_kernel
    if next_norm_g is not None:
        in_specs += [pl.BlockSpec((1, 1, D), lambda i, p: (layer + 1, 0, 0)),
                     pl.BlockSpec((1, 1, D), lambda i, p: (layer + 1, 0, 1)),
                     pl.BlockSpec((1, 1, D), lambda i, p: (layer + 1, 0, 0))]
        args += (next_norm_g, mod, mod)
        out_specs = [row_spec, row_spec]
        out_shape = [out_shape, jax.ShapeDtypeStruct((S, D), BF16)]
        body = _combine_norm_kernel
    grid_spec = pltpu.PrefetchScalarGridSpec(
        num_scalar_prefetch=1,
        grid=(S // tm,),
        in_specs=in_specs,
        out_specs=out_specs,
        scratch_shapes=[pltpu.VMEM((2, MOE_TOP_K, tm, D), F32), pltpu.SemaphoreType.DMA((2,))])
    return pl.pallas_call(
        functools.partial(body, tm=tm),
        grid_spec=grid_spec,
        out_shape=out_shape,
        compiler_params=_cparams(1, 40),
        name="moe_combine",
    )(*args)


def _work_items(starts, ends, counts, n_rows, tm):
    n_experts = counts.shape[0]
    starts = jnp.concatenate([starts, ends[-1:]])
    ends = jnp.concatenate([ends, jnp.full((1,), n_rows, ends.dtype)])
    counts = jnp.concatenate([counts, n_rows - ends[-2:-1]])
    n_items = n_rows // tm + n_experts
    first_tile = starts // tm
    n_e = jnp.where(counts > 0, (ends + tm - 1) // tm - first_tile, 0)
    item_end = jnp.cumsum(n_e)
    item_start = item_end - n_e
    total = item_end[-1]
    idx = jnp.arange(n_items, dtype=jnp.int32)
    idx_c = jnp.minimum(idx, total - 1)
    item_r = jnp.minimum(jnp.sum((idx_c[:, None] >= item_end[None, :]).astype(jnp.int32), axis=1), n_experts)
    item_tile = (first_tile[item_r] + idx_c - item_start[item_r]).astype(jnp.int32)
    kind = jnp.where(idx < total, jnp.where(item_r == n_experts, 2, 1), 0).astype(jnp.int32)
    last_expert = jnp.max(jnp.where(counts[:n_experts] > 0, jnp.arange(n_experts, dtype=jnp.int32), 0))
    item_e = jnp.where(item_r == n_experts, last_expert, item_r).astype(jnp.int32)
    in_tile = jnp.where(item_r == n_experts, (ends[n_experts - 1] - 1) // tm, item_tile).astype(jnp.int32)
    run_start = jnp.concatenate([jnp.ones((1,), bool), item_e[1:] != item_e[:-1]])
    next_start = lax.cummin(jnp.where(run_start, idx, n_items), reverse=True)
    next_start = jnp.concatenate([next_start[1:], jnp.full((1,), n_items, jnp.int32)])
    last_run = next_start >= n_items
    next_e = jnp.where(last_run, item_e[0], item_e[jnp.minimum(next_start, n_items - 1)])
    return (item_tile, item_e, kind, starts[item_r].astype(jnp.int32), ends[item_r].astype(jnp.int32),
            next_e.astype(jnp.int32), last_run.astype(jnp.int32), in_tile)


def _route(logits, n_experts):
    S = logits.shape[0]
    top_logit, top_idx = lax.top_k(logits, MOE_TOP_K)
    top_w = jax.nn.softmax(top_logit, axis=-1)
    n_assign = S * MOE_TOP_K
    n_rows = n_assign + n_experts * MOE_SUB
    flat_e = top_idx.reshape(n_assign).astype(jnp.int32)
    flat_idx = jnp.arange(n_assign, dtype=jnp.int32)
    counts = jnp.sum((flat_e[:, None] == jnp.arange(n_experts, dtype=jnp.int32)[None, :]).astype(jnp.int32), axis=0)
    dense_starts = jnp.cumsum(counts) - counts
    padded = (counts + MOE_SUB - 1) // MOE_SUB * MOE_SUB
    ends = jnp.cumsum(padded)
    starts = ends - padded
    order = jnp.sort(flat_e * n_assign + flat_idx) % n_assign
    _, dense_row = lax.sort((order, flat_idx), num_keys=1)
    pos = (starts[flat_e] + dense_row - dense_starts[flat_e]).astype(jnp.int32)
    tok_dense = order // MOE_TOP_K
    row = jnp.arange(n_rows, dtype=jnp.int32)
    row_e = jnp.minimum(jnp.sum((row[:, None] >= ends[None, :]).astype(jnp.int32), axis=1), n_experts - 1)
    offset = row - starts[row_e]
    is_real = offset < counts[row_e]
    tok_rows = jnp.where(is_real, tok_dense[jnp.clip(dense_starts[row_e] + offset, 0, n_assign - 1)], row % S)
    return tok_rows.astype(jnp.int32), pos, top_w, (starts, ends, padded)


def kernel(x, c, ada_w, ada_b, mix_norm, ffn_norm, conv_pw1_w, conv_pw1_b, conv_dw_w, conv_dw_b, conv_ln_g, conv_ln_b, conv_pw2_w, conv_pw2_b, fox_qkv_w, fox_o_w, fox_fgate_w, fox_fgate_b, fox_q_norm, fox_k_norm, ffn_gate_w, ffn_up_w, ffn_down_w, moe_router_w, moe_router_b, moe_gate_w, moe_up_w, moe_down_w):
    B, S, D = x.shape
    assert B == 1, "kernels are written for a single sequence"
    L = ada_w.shape[0]
    H = fox_fgate_w.shape[-1]
    E = moe_router_w.shape[-1]

    def as_rows(v):
        return v.reshape(v.shape[0], 1, v.shape[1])

    mod = _adaln(c.reshape(D, 1), ada_w, ada_b)
    mix_g, ffn_g = as_rows(mix_norm), as_rows(ffn_norm)
    zero_bias = jnp.zeros((1, 1, D), F32)
    xs = x.reshape(S, D)

    h_next = None
    for i in range(L):
        j = i // 2
        h = _norm_mod(xs, mix_g, mod, i, 0) if h_next is None else h_next
        h_next = None
        if i % 2 == 0:
            u = _pw1_glu(h, conv_pw1_w, as_rows(conv_pw1_b), j)
            v = _conv_ln(u, conv_dw_w, as_rows(conv_dw_b), as_rows(conv_ln_g), as_rows(conv_ln_b), j)
            xs, h = _mixer_out(v, conv_pw2_w, as_rows(conv_pw2_b), xs, mod, ffn_g, j, i, "conv_pw2")
            hid = _gate_up(h, ffn_gate_w, ffn_up_w, j)
            xs = _mm_res(hid, ffn_down_w, zero_bias, xs, mod, j, i, 5, 1024, 256, "ffn_down")
        else:
            q_gain = fox_q_norm[j] * (HEAD_DIM ** -0.5 * LOG2E)
            gain = jnp.concatenate([jnp.tile(q_gain, H), jnp.tile(fox_k_norm[j], H),
                                    jnp.ones((D,), F32)])[None, :]
            qkv = _qkv(h, fox_qkv_w, gain, j)
            cum = _fgate_cum(h, jnp.swapaxes(fox_fgate_w, 1, 2), fox_fgate_b, j)
            bound = 1.02 * HEAD_DIM * jnp.max(jnp.abs(q_gain)) * jnp.max(jnp.abs(fox_k_norm[j]))
            o = _attention(qkv, (cum * LOG2E).reshape(H, 1, S), bound)
            rw_pad = jnp.zeros((D, LANES), F32).at[:, :E].set(moe_router_w[j])
            rb_pad = jnp.zeros((1, LANES), F32).at[0, :E].set(moe_router_b[j])
            xs, logits = _mixer_out(o, fox_o_w, zero_bias, xs, mod, ffn_g, j, i, "fox_o", router=(rw_pad, rb_pad))
            tok_rows, pos, top_w, ranges = _route(logits[:, :E], E)
            items = _work_items(*ranges, tok_rows.shape[0], MOE_TM)
            xg = _gather_norm(tok_rows, xs, ffn_g, mod, i)
            hid = _moe_gate_up(items, xg, moe_gate_w, moe_up_w, j)
            y_sorted = _moe_down(items, hid, moe_down_w, j)
            if i + 1 < L:
                xs, h_next = _combine(pos, y_sorted, xs, mod, i, top_w, mix_g)
            else:
                xs = _combine(pos, y_sorted, xs, mod, i, top_w)
    return xs.reshape(B, S, D)
```

```python
import functools

import jax
import jax.numpy as jnp
from jax import lax
from jax.experimental import pallas as pl
from jax.experimental.pallas import tpu as pltpu

F32 = jnp.float32
BF16 = jnp.bfloat16

NORM_EPS = 1e-6
HEAD_DIM = 128
CONV_WIDTH = 31
CONV_HALO = 32
MOE_TOP_K = 2
LOG2E = 1.4426950408889634
MASK_VALUE = -1e30
ATTN_FAST_MAX_BOUND = 40.0
LANES = 128
SUBLANES = 8
BIG_VMEM_MB = 57
DENSE_TM = 2048
MOE_SUB = 256
MOE_TM = 512


def _cparams(n_axes, vmem_mb):
    return pltpu.CompilerParams(
        dimension_semantics=("arbitrary",) * n_axes,
        vmem_limit_bytes=vmem_mb << 20)


def _norm_modulate(x, g, sc, sh):
    ms = jnp.mean(x * x, axis=-1, keepdims=True)
    y = x * lax.rsqrt(ms + NORM_EPS) * g
    return y * (1.0 + sc) + sh


def _silu(x):
    return x * jax.nn.sigmoid(x)


def _adaln_kernel(c_ref, w_ref, b_ref, o_ref):
    c = c_ref[...]
    o_ref[0] = jnp.sum(w_ref[0] * _silu(c), axis=0, keepdims=True) + b_ref[0]


def _adaln(c_col, ada_w, ada_b):
    L, D, N = ada_w.shape
    tn = 1024
    return pl.pallas_call(
        _adaln_kernel,
        grid=(L, N // tn),
        in_specs=[pl.BlockSpec((D, 1), lambda l, j: (0, 0)),
                  pl.BlockSpec((1, D, tn), lambda l, j: (l, 0, j)),
                  pl.BlockSpec((1, 1, tn), lambda l, j: (l, 0, j))],
        out_specs=pl.BlockSpec((1, 1, tn), lambda l, j: (l, 0, j)),
        out_shape=jax.ShapeDtypeStruct((L, 1, N), F32),
        compiler_params=_cparams(2, 40),
        name="adaln",
    )(c_col, ada_w, ada_b.reshape(L, 1, N))


def _mod_spec(layer, which, D):
    return pl.BlockSpec((1, 1, D), lambda *_: (layer, 0, which))


def _vec_spec(layer, D):
    return pl.BlockSpec((1, 1, D), lambda *_: (layer, 0, 0))


def _norm_mod_kernel(x_ref, g_ref, sc_ref, sh_ref, o_ref):
    o_ref[...] = _norm_modulate(x_ref[...], g_ref[0], sc_ref[0], sh_ref[0]).astype(o_ref.dtype)


def _norm_mod(x, norm_g, mod, layer, which_shift):
    S, D = x.shape
    tm = 512
    return pl.pallas_call(
        _norm_mod_kernel,
        grid=(S // tm,),
        in_specs=[pl.BlockSpec((tm, D), lambda i: (i, 0)),
                  _vec_spec(layer, D),
                  _mod_spec(layer, which_shift + 1, D),
                  _mod_spec(layer, which_shift, D)],
        out_specs=pl.BlockSpec((tm, D), lambda i: (i, 0)),
        out_shape=jax.ShapeDtypeStruct((S, D), BF16),
        compiler_params=_cparams(1, 32),
        name="norm_mod",
    )(x, norm_g, mod, mod)


def _dot(a, w):
    return jnp.dot(a, w, preferred_element_type=F32)


def _pw1_glu_kernel(a_ref, wv_ref, wg_ref, bv_ref, bg_ref, o_ref):
    a = a_ref[...]
    val = _dot(a, wv_ref[...].astype(BF16)) + bv_ref[0]
    gate = _dot(a, wg_ref[...].astype(BF16)) + bg_ref[0]
    o_ref[...] = val * jax.nn.sigmoid(gate)


def _pw1_glu(h, pw1_w, pw1_b, j):
    S, D = h.shape
    tm, tn = DENSE_TM, 512
    nt = D // tn
    return pl.pallas_call(
        _pw1_glu_kernel,
        grid=(S // tm, nt),
        in_specs=[pl.BlockSpec((tm, D), lambda i, n: (i, 0)),
                  pl.BlockSpec((None, D, tn), lambda i, n: (j, 0, n)),
                  pl.BlockSpec((None, D, tn), lambda i, n: (j, 0, n + nt)),
                  pl.BlockSpec((1, 1, tn), lambda i, n: (j, 0, n)),
                  pl.BlockSpec((1, 1, tn), lambda i, n: (j, 0, n + nt))],
        out_specs=pl.BlockSpec((tm, tn), lambda i, n: (i, n)),
        out_shape=jax.ShapeDtypeStruct((S, D), F32),
        compiler_params=_cparams(2, BIG_VMEM_MB),
        name="pw1_glu",
    )(h, pw1_w, pw1_w, pw1_b, pw1_b)


def _gate_up_kernel(a_ref, wg_ref, wu_ref, o_ref):
    a = a_ref[...]
    g = _dot(a, wg_ref[...].astype(BF16))
    u = _dot(a, wu_ref[...].astype(BF16))
    o_ref[...] = (_silu(g) * u).astype(o_ref.dtype)


def _gate_up(h, gate_w, up_w, j):
    S, D = h.shape
    F = gate_w.shape[-1]
    tm, tn = 1024, 1024
    return pl.pallas_call(
        _gate_up_kernel,
        grid=(S // tm, F // tn),
        in_specs=[pl.BlockSpec((tm, D), lambda i, n: (i, 0)),
                  pl.BlockSpec((None, D, tn), lambda i, n: (j, 0, n)),
                  pl.BlockSpec((None, D, tn), lambda i, n: (j, 0, n))],
        out_specs=pl.BlockSpec((tm, tn), lambda i, n: (i, n)),
        out_shape=jax.ShapeDtypeStruct((S, F), BF16),
        compiler_params=_cparams(2, BIG_VMEM_MB),
        name="ffn_gate_up",
    )(h, gate_w, up_w)


def _mm_res_kernel(a_ref, w_ref, b_ref, x_ref, g_ref, o_ref):
    y = _dot(a_ref[...], w_ref[...].astype(BF16)) + b_ref[0]
    o_ref[...] = x_ref[...] + g_ref[0] * y


def _mm_res(a, w, bias, x, mod, j, layer, which_gate, tm, tn, name):
    S, K = a.shape
    D = w.shape[-1]
    bj = j if bias.shape[0] > 1 else 0
    return pl.pallas_call(
        _mm_res_kernel,
        grid=(S // tm, D // tn),
        in_specs=[pl.BlockSpec((tm, K), lambda i, n: (i, 0)),
                  pl.BlockSpec((None, K, tn), lambda i, n: (j, 0, n)),
                  pl.BlockSpec((1, 1, tn), lambda i, n: (bj, 0, n)),
                  pl.BlockSpec((tm, tn), lambda i, n: (i, n)),
                  pl.BlockSpec((1, 1, tn), lambda i, n: (layer, 0, which_gate * (D // tn) + n))],
        out_specs=pl.BlockSpec((tm, tn), lambda i, n: (i, n)),
        out_shape=jax.ShapeDtypeStruct((S, D), F32),
        compiler_params=_cparams(2, BIG_VMEM_MB),
        name=name,
    )(a, w, bias, x, mod)


def _router_logits(h, rw, rb):
    h_hi, w_hi = h.astype(BF16), rw.astype(BF16)
    h_lo = (h - h_hi.astype(F32)).astype(BF16)
    w_lo = (rw - w_hi.astype(F32)).astype(BF16)
    return _dot(h_hi, w_hi) + (_dot(h_hi, w_lo) + _dot(h_lo, w_hi)) + rb


def _mixer_out_kernel(a_ref, w_hbm, b_ref, x_ref, gate_ref, ng_ref, nsc_ref, nsh_ref, *rest, j, router):
    if router:
        rw_ref, rb_ref, o_ref, aux_ref, w_stage, wb, sem = rest
    else:
        o_ref, aux_ref, w_stage, wb, sem = rest

    @pl.when(pl.program_id(0) == 0)
    def _():
        width = w_stage.shape[1]
        for c in range(wb.shape[1] // width):
            cols = slice(c * width, (c + 1) * width)
            cp = pltpu.make_async_copy(w_hbm.at[j, :, cols], w_stage, sem.at[0])
            cp.start()
            cp.wait()
            wb[:, cols] = w_stage[...].astype(BF16)

    y = _dot(a_ref[...], wb[...]) + b_ref[0]
    x_new = x_ref[...] + gate_ref[0] * y
    o_ref[...] = x_new
    h = _norm_modulate(x_new, ng_ref[0], nsc_ref[0], nsh_ref[0])
    if router:
        aux_ref[...] = _router_logits(h, rw_ref[...], rb_ref[...])
    else:
        aux_ref[...] = h.astype(aux_ref.dtype)


def _mixer_out(a, w, bias, x, mod, norm_g, j, layer, name, router=None):
    S, K = a.shape
    D = w.shape[-1]
    tm = 512
    bj = j if bias.shape[0] > 1 else 0
    row = lambda width: pl.BlockSpec((tm, width), lambda i: (i, 0))
    in_specs = [row(K), pl.BlockSpec(memory_space=pl.ANY), _vec_spec(bj, D), row(D), _mod_spec(layer, 2, D),
                _vec_spec(layer, D), _mod_spec(layer, 4, D), _mod_spec(layer, 3, D)]
    args = (a, w, bias, x, mod, norm_g, mod, mod)
    if router is None:
        aux_spec, aux_shape = row(D), jax.ShapeDtypeStruct((S, D), BF16)
    else:
        in_specs += [pl.BlockSpec((D, LANES), lambda i: (0, 0)), pl.BlockSpec((1, LANES), lambda i: (0, 0))]
        args += router
        aux_spec, aux_shape = row(LANES), jax.ShapeDtypeStruct((S, LANES), F32)
    return pl.pallas_call(
        functools.partial(_mixer_out_kernel, j=j, router=router is not None),
        grid=(S // tm,),
        in_specs=in_specs,
        out_specs=[row(D), aux_spec],
        out_shape=[jax.ShapeDtypeStruct((S, D), F32), aux_shape],
        scratch_shapes=[pltpu.VMEM((K, D // 2), F32), pltpu.VMEM((K, D), BF16), pltpu.SemaphoreType.DMA((1,))],
        compiler_params=_cparams(1, BIG_VMEM_MB),
        name=name,
    )(*args)


def _conv_ln_kernel(halo_ref, u_ref, w_ref, b_ref, g_ref, beta_ref, o_ref, win_ref, acc_ref, *, tm):
    i = pl.program_id(0)

    @pl.when(i == 0)
    def _():
        win_ref[0:CONV_HALO, :] = jnp.zeros((CONV_HALO, win_ref.shape[1]), F32)

    @pl.when(i > 0)
    def _():
        win_ref[0:CONV_HALO, :] = halo_ref[...]

    win_ref[CONV_HALO:CONV_HALO + tm, :] = u_ref[...]
    first = CONV_HALO - (CONV_WIDTH - 1)
    for c in range(u_ref.shape[1] // LANES):
        cs = slice(c * LANES, (c + 1) * LANES)
        out = None
        for b in range(SUBLANES):
            rows = tm if b == 0 else tm + SUBLANES
            y = None
            for j in range(first, first + CONV_WIDTH):
                if j % SUBLANES != b:
                    continue
                term = win_ref[j - b:j - b + rows, cs] * w_ref[0, j - first:j - first + 1, cs]
                y = term if y is None else y + term
            yb = y if b == 0 else y[b:b + tm]
            out = yb if out is None else out + yb
        acc_ref[:, cs] = out + b_ref[0][:, cs]
    y = acc_ref[...]
    mu = jnp.mean(y, axis=-1, keepdims=True)
    yc = y - mu
    var = jnp.mean(yc * yc, axis=-1, keepdims=True)
    z = yc * lax.rsqrt(var + NORM_EPS) * g_ref[0] + beta_ref[0]
    o_ref[...] = _silu(z).astype(o_ref.dtype)


def _conv_ln(u, dw_w, dw_b, ln_g, ln_b, j):
    S, D = u.shape
    tm = 128
    hb = tm // CONV_HALO
    return pl.pallas_call(
        functools.partial(_conv_ln_kernel, tm=tm),
        grid=(S // tm,),
        in_specs=[pl.BlockSpec((CONV_HALO, D), lambda i: (jnp.maximum(i * hb - 1, 0), 0)),
                  pl.BlockSpec((tm, D), lambda i: (i, 0)),
                  pl.BlockSpec((1, CONV_WIDTH, D), lambda i: (j, 0, 0)),
                  _vec_spec(j, D), _vec_spec(j, D), _vec_spec(j, D)],
        out_specs=pl.BlockSpec((tm, D), lambda i: (i, 0)),
        out_shape=jax.ShapeDtypeStruct((S, D), BF16),
        scratch_shapes=[pltpu.VMEM((CONV_HALO + tm, D), F32), pltpu.VMEM((tm, D), F32)],
        compiler_params=_cparams(1, 32),
        name="conv_ln",
    )(u, u, dw_w, dw_b, ln_g, ln_b)


def _qkv_kernel(a_ref, w_ref, gain_ref, o_ref, *, n_norm_tiles):
    is_norm_tile = pl.program_id(1) < n_norm_tiles
    acc = _dot(a_ref[...], w_ref[...].astype(BF16))
    gain = gain_ref[...]
    for h in range(acc.shape[1] // HEAD_DIM):
        sl = slice(h * HEAD_DIM, (h + 1) * HEAD_DIM)
        t = acc[:, sl]
        ms = jnp.mean(t * t, axis=-1, keepdims=True)
        r = jnp.where(is_norm_tile, lax.rsqrt(ms + NORM_EPS), 1.0)
        o_ref[:, sl] = (t * r * gain[:, sl]).astype(o_ref.dtype)


def _qkv(h, qkv_w, gain, j):
    S, D = h.shape
    N = qkv_w.shape[-1]
    tm, tn = DENSE_TM, 1024
    return pl.pallas_call(
        functools.partial(_qkv_kernel, n_norm_tiles=2 * D // tn),
        grid=(S // tm, N // tn),
        in_specs=[pl.BlockSpec((tm, D), lambda i, n: (i, 0)),
                  pl.BlockSpec((None, D, tn), lambda i, n: (j, 0, n)),
                  pl.BlockSpec((1, tn), lambda i, n: (0, n))],
        out_specs=pl.BlockSpec((tm, tn), lambda i, n: (i, n)),
        out_shape=jax.ShapeDtypeStruct((S, N), BF16),
        compiler_params=_cparams(2, BIG_VMEM_MB),
        name="fox_qkv",
    )(h, qkv_w, gain)


def _fgate_kernel(h_ref, fwt_ref, fb_ref, tri_ref, o_ref, carry_ref):
    i = pl.program_id(0)

    @pl.when(i == 0)
    def _():
        carry_ref[...] = jnp.zeros(carry_ref.shape, F32)

    z = lax.dot_general(fwt_ref[...].astype(BF16), h_ref[...], (((1,), (1,)), ((), ())),
                        preferred_element_type=F32) + fb_ref[...][:, 0:1]
    lf = jnp.minimum(z, 0.0) - jnp.log1p(jnp.exp(-jnp.abs(z)))
    hi = lf.astype(BF16)
    r1 = lf - hi.astype(F32)
    mid = r1.astype(BF16)
    lo = (r1 - mid.astype(F32)).astype(BF16)
    tri = tri_ref[...]
    cs = _dot(hi, tri) + _dot(mid, tri) + _dot(lo, tri) + carry_ref[...][:, 0:1]
    o_ref[...] = cs
    carry_ref[...] = jnp.broadcast_to(cs[:, cs.shape[1] - 1:], carry_ref.shape)


def _fgate_cum(h, fgate_wt, fgate_b, j):
    S, D = h.shape
    H = fgate_wt.shape[1]
    tm = 512
    tri = jnp.triu(jnp.ones((tm, tm), F32)).astype(BF16)
    fb = jnp.broadcast_to(fgate_b[j][:, None], (H, LANES))
    return pl.pallas_call(
        _fgate_kernel,
        grid=(S // tm,),
        in_specs=[pl.BlockSpec((tm, D), lambda i: (i, 0)),
                  pl.BlockSpec((None, H, D), lambda i: (j, 0, 0)),
                  pl.BlockSpec((H, LANES), lambda i: (0, 0)),
                  pl.BlockSpec((tm, tm), lambda i: (0, 0))],
        out_specs=pl.BlockSpec((H, tm), lambda i: (0, i)),
        out_shape=jax.ShapeDtypeStruct((H, S), F32),
        scratch_shapes=[pltpu.VMEM((H, LANES), F32)],
        compiler_params=_cparams(1, 32),
        name="fox_fgate",
    )(h, fgate_wt, fb, tri)


def _attn_kernel(flag_ref, q_ref, k_ref, v_ref, ck_ref, b_ref, o_ref,
                 acc_ref, m_ref, vaug_ref, rt_ref, *, tq, heads):
    qi = pl.program_id(1)
    S = k_ref.shape[0]
    hs = [slice(g * HEAD_DIM, (g + 1) * HEAD_DIM) for g in range(heads)]
    acc_ref[...] = jnp.zeros(acc_ref.shape, F32)

    @pl.when(qi == 0)
    def _():
        for g in range(heads):
            vaug_ref[g, :, 0:HEAD_DIM] = v_ref[:, hs[g]]
            vaug_ref[g, :, HEAD_DIM:] = jnp.ones((S, HEAD_DIM), BF16)

    def scores(g, ks, width):
        return lax.dot_general(q_ref[:, hs[g]], k_ref[pl.ds(ks, width), hs[g]],
                               (((1,), (1,)), ((), ())), preferred_element_type=F32)

    def visible(c):
        row = lax.broadcasted_iota(jnp.int32, (tq, LANES), 0)
        col = lax.broadcasted_iota(jnp.int32, (tq, LANES), 1) + c * LANES
        return col <= row

    def fast_step(ki, width, diag_at=None):
        ks = pl.multiple_of(ki * tq, tq)
        for g in range(heads):
            s = scores(g, ks, width)
            ckr = ck_ref[g, :, pl.ds(ks, width)]
            rt = rt_ref[g]
            ps = []
            for c in range(width // LANES):
                sl = slice(c * LANES, (c + 1) * LANES)
                t = s[:, sl] + (rt - ckr[:, sl])
                if diag_at is not None and c >= diag_at:
                    t = jnp.where(visible(c - diag_at), t, MASK_VALUE)
                ps.append(jnp.exp2(t).astype(BF16))
            p = jnp.concatenate(ps, axis=1)
            acc_ref[g] += _dot(p, vaug_ref[g, pl.ds(ks, width), :])

    def slow_step(ki, width, diag_at=None):
        ks = pl.multiple_of(ki * tq, tq)
        for g in range(heads):
            t = scores(g, ks, width) - ck_ref[g, :, pl.ds(ks, width)]
            if diag_at is not None:
                t = jnp.concatenate(
                    [t[:, c * LANES:(c + 1) * LANES] if c < diag_at else
                     jnp.where(visible(c - diag_at), t[:, c * LANES:(c + 1) * LANES], MASK_VALUE)
                     for c in range(width // LANES)], axis=1)
            m_prev = m_ref[g]
            m_new = jnp.maximum(m_prev, jnp.max(t, axis=-1, keepdims=True))
            alpha = jnp.exp2(m_prev - m_new)
            p = jnp.exp2(t - m_new[:, 0:1])
            pv = _dot(p.astype(BF16), vaug_ref[g, pl.ds(ks, width), :])
            acc_ref[g] = acc_ref[g] * jnp.concatenate([alpha, alpha], axis=1) + pv
            m_ref[g] = m_new

    def sweep(step):
        def body(kk, carry):
            step(4 * kk, 4 * tq)
            return carry
        lax.fori_loop(0, qi // 4, body, 0)
        rem = qi % 4

        @pl.when(rem >= 2)
        def _():
            step(qi - rem, 2 * tq)

        @pl.when(rem % 2 == 1)
        def _():
            step(qi - 1, 2 * tq, diag_at=tq // LANES)

        @pl.when(rem % 2 == 0)
        def _():
            step(qi, tq, diag_at=0)

    @pl.when(flag_ref[0] == 1)
    def _():
        qs = pl.multiple_of(qi * tq, tq)
        for g in range(heads):
            ckq = jnp.broadcast_to(ck_ref[g, :, pl.ds(qs, tq)], (LANES, tq))
            rt_ref[g] = ckq.T - b_ref[...]
        sweep(fast_step)

    @pl.when(flag_ref[0] != 1)
    def _():
        m_ref[...] = jnp.full(m_ref.shape, MASK_VALUE, F32)
        sweep(slow_step)

    for g in range(heads):
        acc = acc_ref[g]
        o_ref[:, hs[g]] = (acc[:, 0:HEAD_DIM] / acc[:, HEAD_DIM:]).astype(o_ref.dtype)


def _attention(qkv, ck2, bound):
    S = qkv.shape[0]
    H = ck2.shape[0]
    tq, heads = 512, 2
    hw = heads * HEAD_DIM
    flag = (bound <= ATTN_FAST_MAX_BOUND).astype(jnp.int32).reshape(1)
    bvec = jnp.broadcast_to(bound.astype(F32), (1, LANES))
    grid_spec = pltpu.PrefetchScalarGridSpec(
        num_scalar_prefetch=1,
        grid=(H // heads, S // tq),
        in_specs=[pl.BlockSpec((tq, hw), lambda h, i, f: (i, h)),
                  pl.BlockSpec((S, hw), lambda h, i, f: (0, H // heads + h)),
                  pl.BlockSpec((S, hw), lambda h, i, f: (0, 2 * (H // heads) + h)),
                  pl.BlockSpec((heads, 1, S), lambda h, i, f: (h, 0, 0)),
                  pl.BlockSpec((1, LANES), lambda h, i, f: (0, 0))],
        out_specs=pl.BlockSpec((tq, hw), lambda h, i, f: (i, h)),
        scratch_shapes=[pltpu.VMEM((heads, tq, 2 * HEAD_DIM), F32),
                        pltpu.VMEM((heads, tq, HEAD_DIM), F32),
                        pltpu.VMEM((heads, S, 2 * HEAD_DIM), BF16),
                        pltpu.VMEM((heads, tq, LANES), F32)])
    return pl.pallas_call(
        functools.partial(_attn_kernel, tq=tq, heads=heads),
        grid_spec=grid_spec,
        out_shape=jax.ShapeDtypeStruct((S, H * HEAD_DIM), BF16),
        compiler_params=_cparams(2, BIG_VMEM_MB),
        name="fox_attn",
    )(flag, qkv, qkv, qkv, ck2, bvec)


def _row_copy(src_hbm, dst, src_row, dst_row, sem):
    return pltpu.make_async_copy(src_hbm.at[pl.ds(src_row, 1), :], dst.at[pl.ds(dst_row, 1), :], sem)


def _gathered_rows(idx_ref, src_hbm, buf, sem, tm, n_per_row):
    i = pl.program_id(0)

    def start_copies(step, slot):
        def body(half, c):
            for odd in range(2):
                r = 2 * half + odd
                for k in range(n_per_row):
                    src_row = idx_ref[(step * tm + r) * n_per_row + k]
                    _row_copy(src_hbm, buf.at[slot, k], src_row, r, sem.at[slot]).start(priority=(odd + k) % 2)
            return c
        lax.fori_loop(0, tm // 2, body, 0, unroll=4)

    @pl.when(i == 0)
    def _():
        start_copies(0, 0)

    @pl.when(i + 1 < pl.num_programs(0))
    def _():
        start_copies(i + 1, (i + 1) % 2)

    slot = i % 2
    for k in range(n_per_row):
        pltpu.make_async_copy(src_hbm.at[pl.ds(0, tm), :], buf.at[slot, k], sem.at[slot]).wait()
    return slot


def _gather_norm_kernel(tok_ref, x_hbm, g_ref, sc_ref, sh_ref, o_ref, buf, sem, *, tm):
    slot = _gathered_rows(tok_ref, x_hbm, buf, sem, tm, 1)
    o_ref[...] = _norm_modulate(buf[slot, 0], g_ref[0], sc_ref[0], sh_ref[0]).astype(o_ref.dtype)


def _gather_norm(tok_sorted, x, norm_g, mod, layer):
    S, D = x.shape
    n_rows = tok_sorted.shape[0]
    tm = 256
    grid_spec = pltpu.PrefetchScalarGridSpec(
        num_scalar_prefetch=1,
        grid=(n_rows // tm,),
        in_specs=[pl.BlockSpec(memory_space=pl.ANY),
                  pl.BlockSpec((1, 1, D), lambda i, t: (layer, 0, 0)),
                  pl.BlockSpec((1, 1, D), lambda i, t: (layer, 0, 4)),
                  pl.BlockSpec((1, 1, D), lambda i, t: (layer, 0, 3))],
        out_specs=pl.BlockSpec((tm, D), lambda i, t: (i, 0)),
        scratch_shapes=[pltpu.VMEM((2, 1, tm, D), F32), pltpu.SemaphoreType.DMA((2,))])
    return pl.pallas_call(
        functools.partial(_gather_norm_kernel, tm=tm),
        grid_spec=grid_spec,
        out_shape=jax.ShapeDtypeStruct((n_rows, D), BF16),
        compiler_params=_cparams(1, 32),
        name="moe_gather",
    )(tok_sorted, x, norm_g, mod, mod)


def _grouped_matmul_item(item_refs, a_ref, o_ref, w_hbms, stages, caches, sems, layer, product):
    tile_ref, exp_ref, kind_ref, lo_ref, hi_ref, nxt_ref, lastrun_ref = item_refs
    n, i = pl.program_id(0), pl.program_id(1)
    tm, tn = o_ref.shape
    tile, lo, hi, kind = tile_ref[i], lo_ref[i], hi_ref[i], kind_ref[i]
    run_start = (i == 0) | (exp_ref[i] != exp_ref[jnp.maximum(i - 1, 0)])
    sub_blocks = [(slice(sb * MOE_SUB, (sb + 1) * MOE_SUB), tile * tm + sb * MOE_SUB) for sb in range(tm // MOE_SUB)]

    def copy(k, e, col_tile):
        cols = pl.ds(pl.multiple_of(col_tile * tn, tn), tn)
        return pltpu.make_async_copy(w_hbms[k].at[layer, e, :, cols], stages[k], sems.at[k])

    @pl.when((n == 0) & (i == 0))
    def _():
        for k in range(len(w_hbms)):
            copy(k, exp_ref[0], 0).start(priority=1)

    @pl.when(run_start)
    def _():
        nn = jnp.where(lastrun_ref[i] == 1, n + 1, n)
        for k in range(len(w_hbms)):
            copy(k, 0, 0).wait()
            caches[k][...] = stages[k][...].astype(BF16)

            @pl.when(nn < pl.num_programs(0))
            def _(k=k):
                copy(k, nxt_ref[i], nn).start(priority=1)

    def rows(rs):
        return product(a_ref[rs, :], [cache[...] for cache in caches]).astype(o_ref.dtype)

    whole = (lo <= tile * tm) & (hi >= (tile + 1) * tm)

    @pl.when((kind == 1) & whole)
    def _():
        o_ref[...] = rows(slice(0, tm))

    for rs, start in sub_blocks:
        @pl.when((kind == 1) & jnp.logical_not(whole) & (lo <= start) & (start < hi))
        def _(rs=rs):
            o_ref[rs, :] = rows(rs)

        @pl.when((kind == 2) & (lo <= start) & (start < hi))
        def _(rs=rs):
            o_ref[rs, :] = jnp.zeros((MOE_SUB, tn), o_ref.dtype)


def _moe_gu_kernel(tile_ref, exp_ref, kind_ref, lo_ref, hi_ref, nxt_ref, lastrun_ref, in_tile_ref,
                   a_ref, wg_hbm, wu_hbm, o_ref, wg_stage, wu_stage, wgb, wub, sems, *, layer):
    _grouped_matmul_item((tile_ref, exp_ref, kind_ref, lo_ref, hi_ref, nxt_ref, lastrun_ref), a_ref, o_ref,
                         (wg_hbm, wu_hbm), (wg_stage, wu_stage), (wgb, wub), sems, layer,
                         lambda a, ws: _silu(_dot(a, ws[0])) * _dot(a, ws[1]))


def _moe_gate_up(items, xs, gate_w, up_w, j):
    n_rows, D = xs.shape
    F = gate_w.shape[-1]
    tm, tn = MOE_TM, 1024
    n_items = items[0].shape[0]
    grid_spec = pltpu.PrefetchScalarGridSpec(
        num_scalar_prefetch=len(items),
        grid=(F // tn, n_items),
        in_specs=[pl.BlockSpec((tm, D), lambda n, i, *refs: (refs[-1][i], 0)),
                  pl.BlockSpec(memory_space=pl.ANY), pl.BlockSpec(memory_space=pl.ANY)],
        out_specs=pl.BlockSpec((tm, tn), lambda n, i, tile, *_: (tile[i], n)),
        scratch_shapes=[pltpu.VMEM((D, tn), F32), pltpu.VMEM((D, tn), F32),
                        pltpu.VMEM((D, tn), BF16), pltpu.VMEM((D, tn), BF16),
                        pltpu.SemaphoreType.DMA((2,))])
    return pl.pallas_call(
        functools.partial(_moe_gu_kernel, layer=j),
        grid_spec=grid_spec,
        out_shape=jax.ShapeDtypeStruct((n_rows, F), BF16),
        compiler_params=_cparams(2, BIG_VMEM_MB),
        name="moe_gate_up",
    )(*items, xs, gate_w, up_w)


def _moe_down_kernel(tile_ref, exp_ref, kind_ref, lo_ref, hi_ref, nxt_ref, lastrun_ref, in_tile_ref,
                     a_ref, w_hbm, o_ref, w_stage, wb, sems, *, layer):
    _grouped_matmul_item((tile_ref, exp_ref, kind_ref, lo_ref, hi_ref, nxt_ref, lastrun_ref), a_ref, o_ref,
                         (w_hbm,), (w_stage,), (wb,), sems, layer, lambda a, ws: _dot(a, ws[0]))


def _moe_down(items, hid, down_w, j):
    n_rows, F = hid.shape
    D = down_w.shape[-1]
    tm, tn = MOE_TM, 512
    n_items = items[0].shape[0]
    grid_spec = pltpu.PrefetchScalarGridSpec(
        num_scalar_prefetch=len(items),
        grid=(D // tn, n_items),
        in_specs=[pl.BlockSpec((tm, F), lambda n, i, *refs: (refs[-1][i], 0)),
                  pl.BlockSpec(memory_space=pl.ANY)],
        out_specs=pl.BlockSpec((tm, tn), lambda n, i, tile, *_: (tile[i], n)),
        scratch_shapes=[pltpu.VMEM((F, tn), F32), pltpu.VMEM((F, tn), BF16), pltpu.SemaphoreType.DMA((1,))])
    return pl.pallas_call(
        functools.partial(_moe_down_kernel, layer=j),
        grid_spec=grid_spec,
        out_shape=jax.ShapeDtypeStruct((n_rows, D), F32),
        compiler_params=_cparams(2, BIG_VMEM_MB),
        name="moe_down",
    )(*items, hid, down_w)


def _combined_rows(pos_ref, y_hbm, x_ref, g_ref, w_ref, buf, sem, tm):
    slot = _gathered_rows(pos_ref, y_hbm, buf, sem, tm, MOE_TOP_K)
    w = w_ref[...]
    y = buf[slot, 0] * w[:, 0:1] + buf[slot, 1] * w[:, 1:2]
    return x_ref[...] + g_ref[0] * y


def _combine_kernel(pos_ref, y_hbm, x_ref, g_ref, w_ref, o_ref, buf, sem, *, tm):
    o_ref[...] = _combined_rows(pos_ref, y_hbm, x_ref, g_ref, w_ref, buf, sem, tm)


def _combine_norm_kernel(pos_ref, y_hbm, x_ref, g_ref, w_ref, ng_ref, nsc_ref, nsh_ref, o_ref, h_ref, buf, sem,
                         *, tm):
    x_new = _combined_rows(pos_ref, y_hbm, x_ref, g_ref, w_ref, buf, sem, tm)
    o_ref[...] = x_new
    h_ref[...] = _norm_modulate(x_new, ng_ref[0], nsc_ref[0], nsh_ref[0]).astype(h_ref.dtype)


def _combine(pos, y_sorted, x, mod, layer, top_w, next_norm_g=None):
    S, D = x.shape
    tm = 256
    row_spec = pl.BlockSpec((tm, D), lambda i, p: (i, 0))
    in_specs = [pl.BlockSpec(memory_space=pl.ANY),
                row_spec,
                pl.BlockSpec((1, 1, D), lambda i, p: (layer, 0, 5)),
                pl.BlockSpec((tm, MOE_TOP_K), lambda i, p: (i, 0))]
    args = (pos, y_sorted, x, mod, top_w)
    out_specs, out_shape, body = row_spec, jax.ShapeDtypeStruct((S, D), F32), _combine_kernel
    if next_norm_g is not None:
        in_specs += [pl.BlockSpec((1, 1, D), lambda i, p: (layer + 1, 0, 0)),
                     pl.BlockSpec((1, 1, D), lambda i, p: (layer + 1, 0, 1)),
                     pl.BlockSpec((1, 1, D), lambda i, p: (layer + 1, 0, 0))]
        args += (next_norm_g, mod, mod)
        out_specs = [row_spec, row_spec]
        out_shape = [out_shape, jax.ShapeDtypeStruct((S, D), BF16)]
        body = _combine_norm_kernel
    grid_spec = pltpu.PrefetchScalarGridSpec(
        num_scalar_prefetch=1,
        grid=(S // tm,),
        in_specs=in_specs,
        out_specs=out_specs,
        scratch_shapes=[pltpu.VMEM((2, MOE_TOP_K, tm, D), F32), pltpu.SemaphoreType.DMA((2,))])
    return pl.pallas_call(
        functools.partial(body, tm=tm),
        grid_spec=grid_spec,
        out_shape=out_shape,
        compiler_params=_cparams(1, 40),
        name="moe_combine",
    )(*args)


def _work_items(starts, ends, counts, n_rows, tm):
    n_experts = counts.shape[0]
    starts = jnp.concatenate([starts, ends[-1:]])
    ends = jnp.concatenate([ends, jnp.full((1,), n_rows, ends.dtype)])
    counts = jnp.concatenate([counts, n_rows - ends[-2:-1]])
    n_items = n_rows // tm + n_experts
    first_tile = starts // tm
    n_e = jnp.where(counts > 0, (ends + tm - 1) // tm - first_tile, 0)
    item_end = jnp.cumsum(n_e)
    item_start = item_end - n_e
    total = item_end[-1]
    idx = jnp.arange(n_items, dtype=jnp.int32)
    idx_c = jnp.minimum(idx, total - 1)
    item_r = jnp.minimum(jnp.sum((idx_c[:, None] >= item_end[None, :]).astype(jnp.int32), axis=1), n_experts)
    item_tile = (first_tile[item_r] + idx_c - item_start[item_r]).astype(jnp.int32)
    kind = jnp.where(idx < total, jnp.where(item_r == n_experts, 2, 1), 0).astype(jnp.int32)
    last_expert = jnp.max(jnp.where(counts[:n_experts] > 0, jnp.arange(n_experts, dtype=jnp.int32), 0))
    item_e = jnp.where(item_r == n_experts, last_expert, item_r).astype(jnp.int32)
    in_tile = jnp.where(item_r == n_experts, (ends[n_experts - 1] - 1) // tm, item_tile).astype(jnp.int32)
    run_start = jnp.concatenate([jnp.ones((1,), bool), item_e[1:] != item_e[:-1]])
    next_start = lax.cummin(jnp.where(run_start, idx, n_items), reverse=True)
    next_start = jnp.concatenate([next_start[1:], jnp.full((1,), n_items, jnp.int32)])
    last_run = next_start >= n_items
    next_e = jnp.where(last_run, item_e[0], item_e[jnp.minimum(next_start, n_items - 1)])
    return (item_tile, item_e, kind, starts[item_r].astype(jnp.int32), ends[item_r].astype(jnp.int32),
            next_e.astype(jnp.int32), last_run.astype(jnp.int32), in_tile)


def _route(logits, n_experts):
    S = logits.shape[0]
    top_logit, top_idx = lax.top_k(logits, MOE_TOP_K)
    top_w = jax.nn.softmax(top_logit, axis=-1)
    n_assign = S * MOE_TOP_K
    n_rows = n_assign + n_experts * MOE_SUB
    flat_e = top_idx.reshape(n_assign).astype(jnp.int32)
    flat_idx = jnp.arange(n_assign, dtype=jnp.int32)
    counts = jnp.sum((flat_e[:, None] == jnp.arange(n_experts, dtype=jnp.int32)[None, :]).astype(jnp.int32), axis=0)
    dense_starts = jnp.cumsum(counts) - counts
    padded = (counts + MOE_SUB - 1) // MOE_SUB * MOE_SUB
    ends = jnp.cumsum(padded)
    starts = ends - padded
    order = jnp.sort(flat_e * n_assign + flat_idx) % n_assign
    _, dense_row = lax.sort((order, flat_idx), num_keys=1)
    pos = (starts[flat_e] + dense_row - dense_starts[flat_e]).astype(jnp.int32)
    tok_dense = order // MOE_TOP_K
    row = jnp.arange(n_rows, dtype=jnp.int32)
    row_e = jnp.minimum(jnp.sum((row[:, None] >= ends[None, :]).astype(jnp.int32), axis=1), n_experts - 1)
    offset = row - starts[row_e]
    is_real = offset < counts[row_e]
    tok_rows = jnp.where(is_real, tok_dense[jnp.clip(dense_starts[row_e] + offset, 0, n_assign - 1)], row % S)
    return tok_rows.astype(jnp.int32), pos, top_w, (starts, ends, padded)


def kernel(x, c, ada_w, ada_b, mix_norm, ffn_norm, conv_pw1_w, conv_pw1_b, conv_dw_w, conv_dw_b, conv_ln_g, conv_ln_b, conv_pw2_w, conv_pw2_b, fox_qkv_w, fox_o_w, fox_fgate_w, fox_fgate_b, fox_q_norm, fox_k_norm, ffn_gate_w, ffn_up_w, ffn_down_w, moe_router_w, moe_router_b, moe_gate_w, moe_up_w, moe_down_w):
    B, S, D = x.shape
    assert B == 1, "kernels are written for a single sequence"
    L = ada_w.shape[0]
    H = fox_fgate_w.shape[-1]
    E = moe_router_w.shape[-1]

    def as_rows(v):
        return v.reshape(v.shape[0], 1, v.shape[1])

    mod = _adaln(c.reshape(D, 1), ada_w, ada_b)
    mix_g, ffn_g = as_rows(mix_norm), as_rows(ffn_norm)
    zero_bias = jnp.zeros((1, 1, D), F32)
    xs = x.reshape(S, D)

    h_next = None
    for i in range(L):
        j = i // 2
        h = _norm_mod(xs, mix_g, mod, i, 0) if h_next is None else h_next
        h_next = None
        if i % 2 == 0:
            u = _pw1_glu(h, conv_pw1_w, as_rows(conv_pw1_b), j)
            v = _conv_ln(u, conv_dw_w, as_rows(conv_dw_b), as_rows(conv_ln_g), as_rows(conv_ln_b), j)
            xs, h = _mixer_out(v, conv_pw2_w, as_rows(conv_pw2_b), xs, mod, ffn_g, j, i, "conv_pw2")
            hid = _gate_up(h, ffn_gate_w, ffn_up_w, j)
            xs = _mm_res(hid, ffn_down_w, zero_bias, xs, mod, j, i, 5, 1024, 256, "ffn_down")
        else:
            q_gain = fox_q_norm[j] * (HEAD_DIM ** -0.5 * LOG2E)
            gain = jnp.concatenate([jnp.tile(q_gain, H), jnp.tile(fox_k_norm[j], H),
                                    jnp.ones((D,), F32)])[None, :]
            qkv = _qkv(h, fox_qkv_w, gain, j)
            cum = _fgate_cum(h, jnp.swapaxes(fox_fgate_w, 1, 2), fox_fgate_b, j)
            bound = 1.02 * HEAD_DIM * jnp.max(jnp.abs(q_gain)) * jnp.max(jnp.abs(fox_k_norm[j]))
            o = _attention(qkv, (cum * LOG2E).reshape(H, 1, S), bound)
            rw_pad = jnp.zeros((D, LANES), F32).at[:, :E].set(moe_router_w[j])
            rb_pad = jnp.zeros((1, LANES), F32).at[0, :E].set(moe_router_b[j])
            xs, logits = _mixer_out(o, fox_o_w, zero_bias, xs, mod, ffn_g, j, i, "fox_o", router=(rw_pad, rb_pad))
            tok_rows, pos, top_w, ranges = _route(logits[:, :E], E)
            items = _work_items(*ranges, tok_rows.shape[0], MOE_TM)
            xg = _gather_norm(tok_rows, xs, ffn_g, mod, i)
            hid = _moe_gate_up(items, xg, moe_gate_w, moe_up_w, j)
            y_sorted = _moe_down(items, hid, moe_down_w, j)
            if i + 1 < L:
                xs, h_next = _combine(pos, y_sorted, xs, mod, i, top_w, mix_g)
            else:
                xs = _combine(pos, y_sorted, xs, mod, i, top_w)
    return xs.reshape(B, S, D)
```
